```python
import math
import jax, jax.numpy as jnp
from jax import lax
import numpy as np

D_MODEL = 1024
BATCH = 4
SEQ = 4096
DEPTH = 4
DEC_BATCH = 128
DEC_SEQ = 1
PAST_LEN = 8192
PAGE_SIZE = 128

N_MIXERS = 2
N_SWA_LAYERS = (DEPTH + 1) // 2
N_SSM_LAYERS = DEPTH // 2
HEAD_DIM = 64
N_HEADS = D_MODEL // HEAD_DIM
N_KV_HEADS = 4
GQA_GROUP = N_HEADS // N_KV_HEADS
WINDOW = 128
ATTN_BLOCK = WINDOW
CACHE_WIN = min(WINDOW, PAST_LEN)
PROMPT_WIN = min(WINDOW, SEQ)
SSM_GROUP = 16
N_SSM_GROUPS = D_MODEL // SSM_GROUP
SSM_STATE = 64
D_FF = 2816
CONV_W = 3
RMS_EPS = 1e-6
DT_MIN = 1e-3
DT_MAX = 1e-1

kernel_name = "swa_sink_s5_convffn_hybrid_step"

F32 = jnp.float32


def rmsnorm(x, g):
    xf = x.astype(F32)
    y = xf * lax.rsqrt(jnp.mean(xf * xf, axis=-1, keepdims=True) + RMS_EPS)
    return (y * g.astype(F32)).astype(x.dtype)


def qkv_project(h, w_qkv, q_gain, k_gain):
    n, l, _ = h.shape
    qkv = h @ w_qkv
    q, k, v = jnp.split(qkv, [N_HEADS * HEAD_DIM, (N_HEADS + N_KV_HEADS) * HEAD_DIM], axis=-1)
    q = rmsnorm(q.reshape(n, l, N_KV_HEADS, GQA_GROUP, HEAD_DIM), q_gain)
    k = rmsnorm(k.reshape(n, l, N_KV_HEADS, HEAD_DIM), k_gain)
    v = v.reshape(n, l, N_KV_HEADS, HEAD_DIM)
    return q, k, v


def sink_attend(scores, valid, sink, v, eq):
    scores = jnp.where(valid, scores, -jnp.inf)
    sink_col = jnp.broadcast_to(sink.astype(F32).reshape(N_KV_HEADS, GQA_GROUP, 1, 1),
                                scores.shape[:-1] + (1,))
    probs = jax.nn.softmax(jnp.concatenate([scores, sink_col], axis=-1), axis=-1)[..., :-1]
    return jnp.einsum(eq, probs.astype(v.dtype), v)


def swa_prompt(h, w_qkv, w_o, q_gain, k_gain, sink):
    n, l, _ = h.shape
    nb = l // ATTN_BLOCK
    q, k, v = qkv_project(h, w_qkv, q_gain, k_gain)
    qb = q.reshape(n, nb, ATTN_BLOCK, N_KV_HEADS, GQA_GROUP, HEAD_DIM)
    kb = k.reshape(n, nb, ATTN_BLOCK, N_KV_HEADS, HEAD_DIM)
    vb = v.reshape(n, nb, ATTN_BLOCK, N_KV_HEADS, HEAD_DIM)

    def band(t):
        prev = jnp.concatenate([jnp.zeros_like(t[:, :1]), t[:, :-1]], axis=1)
        return jnp.concatenate([prev, t], axis=2)

    kk, vv = band(kb), band(vb)
    scores = jnp.einsum('nbqkgd,nbskd->nbkgqs', qb.astype(F32), kk.astype(F32)) * HEAD_DIM ** -0.5
    blk = jnp.arange(nb)[:, None, None]
    qpos = blk * ATTN_BLOCK + jnp.arange(ATTN_BLOCK)[None, :, None]
    kpos = (blk - 1) * ATTN_BLOCK + jnp.arange(2 * ATTN_BLOCK)[None, None, :]
    diff = qpos - kpos
    valid = (diff >= 0) & (diff < WINDOW) & (kpos >= 0)
    o = sink_attend(scores, valid[:, None, None], sink, vv, 'nbkgqs,nbskd->nbqkgd')
    o = o.reshape(n, l, N_HEADS * HEAD_DIM) @ w_o
    return o, k[:, -PROMPT_WIN:], v[:, -PROMPT_WIN:]


def swa_sample(h, cache_k, cache_v, w_qkv, w_o, q_gain, k_gain, sink):
    n, s, _ = h.shape
    q, k, v = qkv_project(h, w_qkv, q_gain, k_gain)
    kk = jnp.concatenate([cache_k.astype(k.dtype), k], axis=1)
    vv = jnp.concatenate([cache_v.astype(v.dtype), v], axis=1)
    scores = jnp.einsum('nqkgd,nskd->nkgqs', q.astype(F32), kk.astype(F32)) * HEAD_DIM ** -0.5
    qpos = PAST_LEN + jnp.arange(s)[:, None]
    kpos = jnp.concatenate([PAST_LEN - CACHE_WIN + jnp.arange(CACHE_WIN),
                            PAST_LEN + jnp.arange(s)])[None, :]
    diff = qpos - kpos
    valid = (diff >= 0) & (diff < WINDOW)
    o = sink_attend(scores, valid, sink, vv, 'nkgqs,nskd->nqkgd')
    o = o.reshape(n, s, N_HEADS * HEAD_DIM) @ w_o
    return o, kk[:, -CACHE_WIN:], vv[:, -CACHE_WIN:]


def s5_discretize(a_re, a_im, log_dt, b_re, b_im):
    a_re, a_im = a_re.astype(F32), a_im.astype(F32)
    dt = jnp.exp(log_dt.astype(F32))[:, None]
    mag = jnp.exp(a_re * dt)
    ab_re, ab_im = mag * jnp.cos(a_im * dt), mag * jnp.sin(a_im * dt)
    den = a_re * a_re + a_im * a_im
    c_re = ((ab_re - 1.0) * a_re + ab_im * a_im) / den
    c_im = (ab_im * a_re - (ab_re - 1.0) * a_im) / den
    b_re, b_im = b_re.astype(F32), b_im.astype(F32)
    bb_re = c_re[..., None] * b_re - c_im[..., None] * b_im
    bb_im = c_re[..., None] * b_im + c_im[..., None] * b_re
    return ab_re, ab_im, bb_re, bb_im


def complex_affine_combine(e1, e2):
    a1r, a1i, b1r, b1i = e1
    a2r, a2i, b2r, b2i = e2
    return (a2r * a1r - a2i * a1i, a2r * a1i + a2i * a1r,
            a2r * b1r - a2i * b1i + b2r, a2r * b1i + a2i * b1r + b2i)


def s5_mixer(h, st_re, st_im, a_re, a_im, log_dt, b_re, b_im, c_re, c_im, d_skip, w_glu):
    n, l, _ = h.shape
    ab_re, ab_im, bb_re, bb_im = s5_discretize(a_re, a_im, log_dt, b_re, b_im)
    u = h.astype(F32).reshape(n, l, N_SSM_GROUPS, SSM_GROUP)
    bu_re = jnp.einsum('nlgc,gpc->nlgp', u, bb_re)
    bu_im = jnp.einsum('nlgc,gpc->nlgp', u, bb_im)
    h0_re, h0_im = st_re.astype(F32), st_im.astype(F32)
    bu_re = bu_re.at[:, 0].add(ab_re * h0_re - ab_im * h0_im)
    bu_im = bu_im.at[:, 0].add(ab_re * h0_im + ab_im * h0_re)
    a_r = jnp.broadcast_to(ab_re, bu_re.shape)
    a_i = jnp.broadcast_to(ab_im, bu_im.shape)
    _, _, xr, xi = lax.associative_scan(complex_affine_combine, (a_r, a_i, bu_re, bu_im), axis=1)
    y = (jnp.einsum('nlgp,gcp->nlgc', xr, c_re.astype(F32))
         - jnp.einsum('nlgp,gcp->nlgc', xi, c_im.astype(F32)))
    y = y.reshape(n, l, D_MODEL) + d_skip.astype(F32) * h.astype(F32)
    z = jax.nn.gelu(y).astype(h.dtype)
    zz = z @ w_glu
    out = zz[..., :D_MODEL] * jax.nn.sigmoid(zz[..., D_MODEL:])
    return out, xr[:, -1].astype(st_re.dtype), xi[:, -1].astype(st_im.dtype)


def conv_ffn(h, buf, w_up, conv_w, conv_b, w_down):
    up = h @ w_up
    l = up.shape[1]
    padded = jnp.concatenate([buf.astype(up.dtype), up], axis=1)
    c = conv_b
    for j in range(CONV_W):
        c = c + conv_w[j] * padded[:, j:j + l]
    g, v = jnp.split(c, 2, axis=-1)
    out = (jax.nn.silu(g) * v) @ w_down
    return out, padded[:, -(CONV_W - 1):]


def trunk(x, cache_k, cache_v, st_re, st_im, st_conv, norm_mix, norm_ffn, w_qkv, w_o, q_norm,
          k_norm, sinks, ssm_a_re, ssm_a_im, ssm_log_dt, ssm_b_re, ssm_b_im, ssm_c_re, ssm_c_im,
          ssm_d, w_glu, w_up, conv_w, conv_b, w_down):
    ks, vs, srs, sis, cbs = [], [], [], [], []
    for i in range(DEPTH):
        j = i // N_MIXERS
        h = rmsnorm(x, norm_mix[i])
        if i % N_MIXERS == 0:
            if cache_k is None:
                o, nk, nv = swa_prompt(h, w_qkv[j], w_o[j], q_norm[j], k_norm[j], sinks[j])
            else:
                o, nk, nv = swa_sample(h, cache_k[j], cache_v[j], w_qkv[j], w_o[j], q_norm[j],
                                       k_norm[j], sinks[j])
            ks.append(nk)
            vs.append(nv)
        else:
            o, sr, si = s5_mixer(h, st_re[j], st_im[j], ssm_a_re[j], ssm_a_im[j], ssm_log_dt[j],
                                 ssm_b_re[j], ssm_b_im[j], ssm_c_re[j], ssm_c_im[j], ssm_d[j], w_glu[j])
            srs.append(sr)
            sis.append(si)
        x = x + o
        h = rmsnorm(x, norm_ffn[i])
        o, cb = conv_ffn(h, st_conv[i], w_up[i], conv_w[i], conv_b[i], w_down[i])
        cbs.append(cb)
        x = x + o
    return x, jnp.stack(ks), jnp.stack(vs), jnp.stack(srs), jnp.stack(sis), jnp.stack(cbs)


def setup_inputs(seed: int = 0) -> dict:
    key = jax.random.key(seed)
    k = jax.random.split(key, 32)
    nrm = jax.random.normal
    F2 = 2 * D_FF
    n_idx = jnp.arange(SSM_STATE, dtype=F32)
    return {
        "x_prompt": nrm(k[0], (BATCH, SEQ, D_MODEL), F32),
        "x_sample": nrm(k[1], (DEC_BATCH, DEC_SEQ, D_MODEL), F32),
        "cache_k": nrm(k[2], (N_SWA_LAYERS, DEC_BATCH, CACHE_WIN, N_KV_HEADS, HEAD_DIM), F32),
        "cache_v": nrm(k[3], (N_SWA_LAYERS, DEC_BATCH, CACHE_WIN, N_KV_HEADS, HEAD_DIM), F32),
        "state_ssm_re": 0.5 * nrm(k[4], (N_SSM_LAYERS, DEC_BATCH, N_SSM_GROUPS, SSM_STATE), F32),
        "state_ssm_im": 0.5 * nrm(k[5], (N_SSM_LAYERS, DEC_BATCH, N_SSM_GROUPS, SSM_STATE), F32),
        "state_conv": nrm(k[6], (DEPTH, DEC_BATCH, CONV_W - 1, F2), F32),
        "norm_mix": 1.0 + 0.05 * nrm(k[7], (DEPTH, D_MODEL), F32),
        "norm_ffn": 1.0 + 0.05 * nrm(k[8], (DEPTH, D_MODEL), F32),
        "w_qkv": nrm(k[9], (N_SWA_LAYERS, D_MODEL, (N_HEADS + 2 * N_KV_HEADS) * HEAD_DIM), F32) * D_MODEL ** -0.5,
        "w_o": nrm(k[10], (N_SWA_LAYERS, N_HEADS * HEAD_DIM, D_MODEL), F32) * (N_HEADS * HEAD_DIM) ** -0.5,
        "q_norm": 1.0 + 0.05 * nrm(k[11], (N_SWA_LAYERS, HEAD_DIM), F32),
        "k_norm": 1.0 + 0.05 * nrm(k[12], (N_SWA_LAYERS, HEAD_DIM), F32),
        "sinks": 0.5 * nrm(k[13], (N_SWA_LAYERS, N_HEADS), F32),
        "ssm_a_re": -0.5 + 0.01 * nrm(k[14], (N_SSM_LAYERS, N_SSM_GROUPS, SSM_STATE), F32),
        "ssm_a_im": math.pi * n_idx + 0.01 * nrm(k[15], (N_SSM_LAYERS, N_SSM_GROUPS, SSM_STATE), F32),
        "ssm_log_dt": jax.random.uniform(k[16], (N_SSM_LAYERS, N_SSM_GROUPS), F32,
                                         math.log(DT_MIN), math.log(DT_MAX)),
        "ssm_b_re": nrm(k[17], (N_SSM_LAYERS, N_SSM_GROUPS, SSM_STATE, SSM_GROUP), F32) * (2 * SSM_GROUP) ** -0.5,
        "ssm_b_im": nrm(k[18], (N_SSM_LAYERS, N_SSM_GROUPS, SSM_STATE, SSM_GROUP), F32) * (2 * SSM_GROUP) ** -0.5,
        "ssm_c_re": nrm(k[19], (N_SSM_LAYERS, N_SSM_GROUPS, SSM_GROUP, SSM_STATE), F32) * SSM_STATE ** -0.5,
        "ssm_c_im": nrm(k[20], (N_SSM_LAYERS, N_SSM_GROUPS, SSM_GROUP, SSM_STATE), F32) * SSM_STATE ** -0.5,
        "ssm_d": nrm(k[21], (N_SSM_LAYERS, D_MODEL), F32),
        "w_glu": nrm(k[22], (N_SSM_LAYERS, D_MODEL, 2 * D_MODEL), F32) * D_MODEL ** -0.5,
        "w_up": nrm(k[23], (DEPTH, D_MODEL, F2), F32) * D_MODEL ** -0.5,
        "conv_w": nrm(k[24], (DEPTH, CONV_W, F2), F32) * CONV_W ** -0.5,
        "conv_b": 0.01 * nrm(k[25], (DEPTH, F2), F32),
        "w_down": nrm(k[26], (DEPTH, D_FF, D_MODEL), F32) * D_FF ** -0.5,
    }


def reference(x_prompt, x_sample, cache_k, cache_v, state_ssm_re, state_ssm_im, state_conv,
              norm_mix, norm_ffn, w_qkv, w_o, q_norm, k_norm, sinks, ssm_a_re, ssm_a_im,
              ssm_log_dt, ssm_b_re, ssm_b_im, ssm_c_re, ssm_c_im, ssm_d, w_glu, w_up, conv_w,
              conv_b, w_down):
    dt = x_prompt.dtype
    zero_re = jnp.zeros((N_SSM_LAYERS, BATCH, N_SSM_GROUPS, SSM_STATE), dt)
    zero_im = jnp.zeros((N_SSM_LAYERS, BATCH, N_SSM_GROUPS, SSM_STATE), dt)
    zero_conv = jnp.zeros((DEPTH, BATCH, CONV_W - 1, 2 * D_FF), dt)
    y_prompt, k_prompt, v_prompt, ssm_re_prompt, ssm_im_prompt, conv_prompt = trunk(
        x_prompt, None, None, zero_re, zero_im, zero_conv, norm_mix, norm_ffn, w_qkv, w_o,
        q_norm, k_norm, sinks, ssm_a_re, ssm_a_im, ssm_log_dt, ssm_b_re, ssm_b_im, ssm_c_re,
        ssm_c_im, ssm_d, w_glu, w_up, conv_w, conv_b, w_down)
    y_sample, k_sample, v_sample, ssm_re_sample, ssm_im_sample, conv_sample = trunk(
        x_sample, cache_k, cache_v, state_ssm_re, state_ssm_im, state_conv, norm_mix, norm_ffn,
        w_qkv, w_o, q_norm, k_norm, sinks, ssm_a_re, ssm_a_im, ssm_log_dt, ssm_b_re, ssm_b_im,
        ssm_c_re, ssm_c_im, ssm_d, w_glu, w_up, conv_w, conv_b, w_down)
    return (y_prompt, y_sample, k_prompt, v_prompt, k_sample, v_sample, ssm_re_prompt,
            ssm_im_prompt, ssm_re_sample, ssm_im_sample, conv_prompt, conv_sample)
```

```python
import functools

import jax
import jax.numpy as jnp
from jax import lax
from jax.experimental import pallas as pl
from jax.experimental.pallas import tpu as pltpu

F32 = jnp.float32
BF16 = jnp.bfloat16

D_MODEL = 1024
HEAD_DIM = 64
N_HEADS = 16
N_KV = 4
GQA = 4
WINDOW = 128
KV_DIM = N_KV * HEAD_DIM
QKV_DIM = D_MODEL + 2 * KV_DIM
D_FF = 2816
F2 = 2 * D_FF
FF_BLK = 256
N_FF_BLK = D_FF // FF_BLK
SSM_GROUP = 16
N_GROUPS = 64
SSM_STATE = 64
RMS_EPS = 1e-6

LANES = 128
CHUNK = 8
GB_GROUPS = LANES // SSM_GROUP
N_GB = N_GROUPS // GB_GROUPS
GB_IN = CHUNK * LANES
GB_ST = GB_GROUPS * SSM_STATE
GB_ST2 = 2 * GB_ST

VMEM_LIMIT = 56 * 1024 * 1024


def _cparams(sem):
    return pltpu.CompilerParams(dimension_semantics=sem, vmem_limit_bytes=VMEM_LIMIT)


def _rms(x, g):
    ms = jnp.mean(x * x, axis=-1, keepdims=True)
    return x * lax.rsqrt(ms + RMS_EPS) * g


def _head_rms(t, n, g):
    return jnp.concatenate(
        [_rms(t[:, j * HEAD_DIM:(j + 1) * HEAD_DIM], g) for j in range(n)], axis=1)


def _const_spec(shape):
    nd = len(shape)
    return pl.BlockSpec(shape, lambda *_: (0,) * nd)


ATT_TQ = 512
ATT_NBLK = ATT_TQ // WINDOW


def _attn_prompt_kernel(sink_ref, x_ref, g_ref, wqkv_ref, wo_ref, qg_ref, kg_ref,
                        xo_ref, kl_ref, vl_ref, kbuf, vbuf, lo_scr, o_scr):
    i = pl.program_id(1)

    @pl.when(i == 0)
    def _():
        kbuf[0:WINDOW, :] = jnp.zeros((WINDOW, KV_DIM), BF16)
        vbuf[0:WINDOW, :] = jnp.zeros((WINDOW, KV_DIM), BF16)
        lo_scr[...] = jnp.full(lo_scr.shape, WINDOW, jnp.int32)

    @pl.when(i == 1)
    def _():
        lo_scr[...] = jnp.zeros(lo_scr.shape, jnp.int32)

    x = x_ref[...]
    h = _rms(x, g_ref[...]).astype(BF16)
    qkv = jnp.dot(h, wqkv_ref[...], preferred_element_type=F32)
    q = qkv[:, :D_MODEL]
    kn = _head_rms(qkv[:, D_MODEL:D_MODEL + KV_DIM], N_KV, kg_ref[...])
    v = qkv[:, D_MODEL + KV_DIM:]
    kl_ref[...] = kn[ATT_TQ - WINDOW:]
    vl_ref[...] = v[ATT_TQ - WINDOW:]
    kbuf[WINDOW:, :] = kn.astype(BF16)
    vbuf[WINDOW:, :] = v.astype(BF16)

    rows = GQA * WINDOW
    row = lax.broadcasted_iota(jnp.int32, (rows, 2 * WINDOW), 0) % WINDOW
    col = lax.broadcasted_iota(jnp.int32, (rows, 2 * WINDOW), 1)
    band = (col > row) & (col <= row + WINDOW)
    qg = qg_ref[...]

    for blk in range(ATT_NBLK):
        r0 = blk * WINDOW
        valid = band & (col >= lo_scr[...]) if blk == 0 else band
        for kvh in range(N_KV):
            kk_h = kbuf[r0:r0 + 2 * WINDOW, kvh * HEAD_DIM:(kvh + 1) * HEAD_DIM]
            vv_h = vbuf[r0:r0 + 2 * WINDOW, kvh * HEAD_DIM:(kvh + 1) * HEAD_DIM]
            qs = []
            sk = []
            for g in range(GQA):
                hd = kvh * GQA + g
                qh = q[r0:r0 + WINDOW, hd * HEAD_DIM:(hd + 1) * HEAD_DIM]
                qs.append(_rms(qh, qg) * (HEAD_DIM ** -0.5))
                sk.append(jnp.full((WINDOW, 1), sink_ref[hd], F32))
            q4 = jnp.concatenate(qs, axis=0).astype(BF16)
            sink_col = jnp.concatenate(sk, axis=0)
            s = lax.dot_general(q4, kk_h, (((1,), (1,)), ((), ())),
                                preferred_element_type=F32)
            s = jnp.where(valid, s, -jnp.inf)
            m = jnp.maximum(jnp.max(s, axis=-1, keepdims=True), sink_col)
            e = jnp.exp(s - m)
            den = jnp.sum(e, axis=-1, keepdims=True) + jnp.exp(sink_col - m)
            p = (e / den).astype(BF16)
            o = jnp.dot(p, vv_h, preferred_element_type=F32)
            for g in range(GQA):
                hd = kvh * GQA + g
                o_scr[r0:r0 + WINDOW, hd * HEAD_DIM:(hd + 1) * HEAD_DIM] = (
                    o[g * WINDOW:(g + 1) * WINDOW].astype(BF16))

    kbuf[0:WINDOW, :] = kbuf[ATT_TQ:, :]
    vbuf[0:WINDOW, :] = vbuf[ATT_TQ:, :]
    out = jnp.dot(o_scr[...], wo_ref[...], preferred_element_type=F32)
    xo_ref[...] = x + out


def _attn_prompt(x, gain, wqkv, wo, qg, kg, sinks):
    b, l, d = x.shape
    tok = pl.BlockSpec((None, ATT_TQ, d), lambda bi, i: (bi, i, 0))
    last = pl.BlockSpec((None, WINDOW, KV_DIM), lambda bi, i: (bi, 0, 0))
    return pl.pallas_call(
        _attn_prompt_kernel,
        grid=(b, l // ATT_TQ),
        in_specs=[
            pl.BlockSpec(memory_space=pltpu.SMEM),
            tok,
            _const_spec((1, d)),
            _const_spec((d, QKV_DIM)),
            _const_spec((d, d)),
            _const_spec((1, HEAD_DIM)),
            _const_spec((1, HEAD_DIM)),
        ],
        out_specs=[tok, last, last],
        out_shape=[
            jax.ShapeDtypeStruct((b, l, d), F32),
            jax.ShapeDtypeStruct((b, WINDOW, KV_DIM), F32),
            jax.ShapeDtypeStruct((b, WINDOW, KV_DIM), F32),
        ],
        scratch_shapes=[
            pltpu.VMEM((WINDOW + ATT_TQ, KV_DIM), BF16),
            pltpu.VMEM((WINDOW + ATT_TQ, KV_DIM), BF16),
            pltpu.VMEM((1, 2 * WINDOW), jnp.int32),
            pltpu.VMEM((ATT_TQ, d), BF16),
        ],
        compiler_params=_cparams(("arbitrary", "arbitrary")),
        name="attn_prompt",
    )(sinks, x, gain, wqkv, wo, qg, kg)


ATS_TB = 8


def _attn_sample_kernel(sink_ref, x_ref, g_ref, wq_ref, wkv_ref, wo_ref, qg_ref, kg_ref,
                        ck_ref, cv_ref, xo_ref, ko_ref, vo_ref, r_scr):
    x = x_ref[...]
    h = _rms(x, g_ref[...]).astype(BF16)
    kv = jnp.dot(h, wkv_ref[...], preferred_element_type=F32)
    kn = _head_rms(kv[:, :KV_DIM], N_KV, kg_ref[...])
    v = kv[:, KV_DIM:]
    qg = qg_ref[...]
    q_g = [_head_rms(jnp.dot(h, wq_ref[g], preferred_element_type=F32), N_KV, qg)
           * (HEAD_DIM ** -0.5) for g in range(GQA)]

    rows = GQA * N_KV
    rkv = lax.broadcasted_iota(jnp.int32, (rows, KV_DIM), 0) % N_KV
    lkv = lax.broadcasted_iota(jnp.int32, (rows, KV_DIM), 1) // HEAD_DIM
    own = rkv == lkv
    colj = lax.broadcasted_iota(jnp.int32, (rows, WINDOW), 1)
    sink_col = jnp.concatenate(
        [jnp.full((1, 1), sink_ref[(r % N_KV) * GQA + r // N_KV], F32) for r in range(rows)],
        axis=0)

    for n in range(ATS_TB):
        kc = ck_ref[n]
        vc = cv_ref[n]
        k_new = kn[n:n + 1]
        v_new = v[n:n + 1]
        ko_ref[n, 0:WINDOW - 1, :] = ck_ref[n, 1:WINDOW, :]
        ko_ref[n, WINDOW - 1:WINDOW, :] = k_new
        vo_ref[n, 0:WINDOW - 1, :] = cv_ref[n, 1:WINDOW, :]
        vo_ref[n, WINDOW - 1:WINDOW, :] = v_new
        qbd = jnp.concatenate(
            [jnp.broadcast_to(q_g[g][n:n + 1], (N_KV, KV_DIM)) for g in range(GQA)], axis=0)
        qbd = jnp.where(own, qbd, 0.0)
        s = lax.dot_general(qbd.astype(BF16), kc.astype(BF16), (((1,), (1,)), ((), ())),
                            preferred_element_type=F32)
        s = jnp.where(colj >= 1, s, -jnp.inf)
        s_new = jnp.sum(qbd * k_new, axis=-1, keepdims=True)
        m = jnp.maximum(jnp.maximum(jnp.max(s, axis=-1, keepdims=True), s_new), sink_col)
        e = jnp.exp(s - m)
        e_new = jnp.exp(s_new - m)
        den = jnp.sum(e, axis=-1, keepdims=True) + e_new + jnp.exp(sink_col - m)
        p = (e / den).astype(BF16)
        o = jnp.dot(p, vc.astype(BF16), preferred_element_type=F32) + (e_new / den) * v_new
        o = jnp.where(own, o, 0.0)
        for g in range(GQA):
            r_scr[g, n:n + 1, :] = jnp.sum(o[g * N_KV:(g + 1) * N_KV], axis=0, keepdims=True)

    out = x
    for g in range(GQA):
        out = out + jnp.dot(r_scr[g].astype(BF16), wo_ref[g], preferred_element_type=F32)
    xo_ref[...] = out


def _attn_sample(x, gain, wq_g, wkv, wo_g, qg, kg, sinks, cache_k, cache_v):
    n, d = x.shape
    tok = pl.BlockSpec((ATS_TB, d), lambda i: (i, 0))
    cache = pl.BlockSpec((ATS_TB, WINDOW, KV_DIM), lambda i: (i, 0, 0))
    return pl.pallas_call(
        _attn_sample_kernel,
        grid=(n // ATS_TB,),
        in_specs=[
            pl.BlockSpec(memory_space=pltpu.SMEM),
            tok,
            _const_spec((1, d)),
            _const_spec((GQA, d, KV_DIM)),
            _const_spec((d, 2 * KV_DIM)),
            _const_spec((GQA, KV_DIM, d)),
            _const_spec((1, HEAD_DIM)),
            _const_spec((1, HEAD_DIM)),
            cache,
            cache,
        ],
        out_specs=[tok, cache, cache],
        out_shape=[
            jax.ShapeDtypeStruct((n, d), F32),
            jax.ShapeDtypeStruct((n, WINDOW, KV_DIM), F32),
            jax.ShapeDtypeStruct((n, WINDOW, KV_DIM), F32),
        ],
        scratch_shapes=[pltpu.VMEM((GQA, ATS_TB, KV_DIM), F32)],
        compiler_params=_cparams(("arbitrary",)),
        name="attn_sample",
    )(sinks, x, gain, wq_g, wkv, wo_g, qg, kg, cache_k, cache_v)


FFN_TM = 512
CARRY = 8


def _ffn_prompt_kernel(x_ref, g_ref, wup_ref, cw_ref, cb_ref, wdn_ref,
                       xo_ref, cs_ref, carry, upbuf, act):
    i = pl.program_id(1)

    @pl.when(i == 0)
    def _():
        carry[...] = jnp.zeros_like(carry)

    x = x_ref[...]
    h = _rms(x, g_ref[...]).astype(BF16)

    def conv_cols(c0):
        cols = pl.ds(c0, FF_BLK)
        up = jnp.dot(h, wup_ref[:, cols], preferred_element_type=F32)
        upbuf[0:CARRY, :] = carry[:, cols]
        upbuf[CARRY:, :] = up
        carry[:, cols] = up[FFN_TM - CARRY:]
        u1 = upbuf[CARRY - 1:CARRY - 1 + FFN_TM, :]
        u2 = upbuf[CARRY - 2:CARRY - 2 + FFN_TM, :]
        c = cb_ref[:, cols]
        c = c + cw_ref[0:1, cols] * u2
        c = c + cw_ref[1:2, cols] * u1
        c = c + cw_ref[2:3, cols] * up
        return c

    for j in range(N_FF_BLK):
        cg = conv_cols(j * FF_BLK)
        cv = conv_cols(D_FF + j * FF_BLK)
        act[:, j * FF_BLK:(j + 1) * FF_BLK] = (jax.nn.silu(cg) * cv).astype(BF16)

    out = jnp.dot(act[...], wdn_ref[...], preferred_element_type=F32)
    xo_ref[...] = x + out
    cs_ref[...] = carry[CARRY - 2:CARRY, :]


def _ffn_prompt(x, gain, wup, cw, cb, wdn):
    b, l, d = x.shape
    tok = pl.BlockSpec((None, FFN_TM, d), lambda bi, i: (bi, i, 0))
    return pl.pallas_call(
        _ffn_prompt_kernel,
        grid=(b, l // FFN_TM),
        in_specs=[
            tok,
            _const_spec((1, d)),
            pl.BlockSpec((d, F2), lambda bi, i: (0, 0), pipeline_mode=pl.Buffered(1)),
            _const_spec((3, F2)),
            _const_spec((1, F2)),
            pl.BlockSpec((D_FF, d), lambda bi, i: (0, 0), pipeline_mode=pl.Buffered(1)),
        ],
        out_specs=[tok, pl.BlockSpec((None, 2, F2), lambda bi, i: (bi, 0, 0))],
        out_shape=[
            jax.ShapeDtypeStruct((b, l, d), F32),
            jax.ShapeDtypeStruct((b, 2, F2), F32),
        ],
        scratch_shapes=[
            pltpu.VMEM((CARRY, F2), F32),
            pltpu.VMEM((CARRY + FFN_TM, FF_BLK), F32),
            pltpu.VMEM((FFN_TM, D_FF), BF16),
        ],
        compiler_params=_cparams(("arbitrary", "arbitrary")),
        name="ffn_prompt",
    )(x, gain, wup, cw, cb, wdn)


def _ffn_sample_kernel(x_ref, g_ref, wg_ref, wv_ref, cwg_ref, cwv_ref, cbg_ref, cbv_ref,
                       wdn_ref, b0g_ref, b0v_ref, b1g_ref, b1v_ref,
                       xo_ref, ug_ref, uv_ref):
    j = pl.program_id(0)

    @pl.when(j == 0)
    def _():
        xo_ref[...] = x_ref[...]

    h = _rms(x_ref[...], g_ref[...]).astype(BF16)
    upg = jnp.dot(h, wg_ref[...], preferred_element_type=F32)
    upv = jnp.dot(h, wv_ref[...], preferred_element_type=F32)
    ug_ref[...] = upg
    uv_ref[...] = upv
    cg = cbg_ref[...]
    cg = cg + cwg_ref[0:1, :] * b0g_ref[...]
    cg = cg + cwg_ref[1:2, :] * b1g_ref[...]
    cg = cg + cwg_ref[2:3, :] * upg
    cv = cbv_ref[...]
    cv = cv + cwv_ref[0:1, :] * b0v_ref[...]
    cv = cv + cwv_ref[1:2, :] * b1v_ref[...]
    cv = cv + cwv_ref[2:3, :] * upv
    a = (jax.nn.silu(cg) * cv).astype(BF16)
    xo_ref[...] += jnp.dot(a, wdn_ref[...], preferred_element_type=F32)


def _ffn_sample(x, gain, wup, cw, cb, wdn, buf):
    n, d = x.shape
    nb = N_FF_BLK
    gcol = lambda j: (0, j)
    vcol = lambda j: (0, nb + j)
    b1g = lambda j: (0, 2 * nb + j)
    b1v = lambda j: (0, 3 * nb + j)
    blk = lambda r: (r, FF_BLK)
    xo, ug, uv = pl.pallas_call(
        _ffn_sample_kernel,
        grid=(nb,),
        in_specs=[
            _const_spec((n, d)),
            _const_spec((1, d)),
            pl.BlockSpec(blk(d), gcol),
            pl.BlockSpec(blk(d), vcol),
            pl.BlockSpec(blk(3), gcol),
            pl.BlockSpec(blk(3), vcol),
            pl.BlockSpec(blk(1), gcol),
            pl.BlockSpec(blk(1), vcol),
            pl.BlockSpec((FF_BLK, d), lambda j: (j, 0)),
            pl.BlockSpec(blk(n), gcol),
            pl.BlockSpec(blk(n), vcol),
            pl.BlockSpec(blk(n), b1g),
            pl.BlockSpec(blk(n), b1v),
        ],
        out_specs=[
            _const_spec((n, d)),
            pl.BlockSpec(blk(n), gcol),
            pl.BlockSpec(blk(n), gcol),
        ],
        out_shape=[
            jax.ShapeDtypeStruct((n, d), F32),
            jax.ShapeDtypeStruct((n, D_FF), F32),
            jax.ShapeDtypeStruct((n, D_FF), F32),
        ],
        compiler_params=_cparams(("arbitrary",)),
        name="ffn_sample",
    )(x, gain, wup, wup, cw, cw, cb, cb, wdn, buf, buf, buf, buf)
    return xo, jnp.concatenate([ug, uv], axis=1)


def _tile_states(t):
    t2 = jnp.concatenate([t, t], axis=1)
    return jnp.concatenate([t2] * (GB_GROUPS // 2), axis=1)


def _ssm_prep_kernel(are_ref, aim_ref, dt_ref, bre_ref, bim_ref, cre_ref, cim_ref,
                     aflat_ref, iflat_ref, dtflat_ref,
                     w_ref, win_ref, cpt_ref, c0t_ref, a8_ref, a1_ref):
    a_re = are_ref[...]
    a_im = aim_ref[...]
    dt = jnp.exp(dt_ref[...])

    def power(d):
        mag = jnp.exp(a_re * dt * d)
        return mag * jnp.cos(a_im * dt * d), mag * jnp.sin(a_im * dt * d)

    ab_re, ab_im = power(1.0)
    den = a_re * a_re + a_im * a_im
    z_re = ((ab_re - 1.0) * a_re + ab_im * a_im) / den
    z_im = (ab_im * a_re - (ab_re - 1.0) * a_im) / den
    b_re = bre_ref[...]
    b_im = bim_ref[...]
    bb_re = z_re * b_re - z_im * b_im
    bb_im = z_re * b_im + z_im * b_re
    c_re = cre_ref[...]
    c_im = cim_ref[...]

    rg = lax.broadcasted_iota(jnp.int32, (LANES, GB_ST2), 0) // SSM_GROUP
    lg = (lax.broadcasted_iota(jnp.int32, (LANES, GB_ST2), 1) % GB_ST) // SSM_STATE
    own = rg == lg

    def blockdiag(t_re, t_im):
        full = jnp.concatenate([_tile_states(t_re), _tile_states(t_im)], axis=1)
        return jnp.where(own, full, 0.0)

    c0t = blockdiag(c_re, -c_im)
    c0t_ref[...] = c0t.astype(BF16)

    lfull = []
    for d in range(CHUNK):
        if d == 0:
            l_re, l_im = bb_re, bb_im
        else:
            e_re, e_im = power(float(d))
            l_re = e_re * bb_re - e_im * bb_im
            l_im = e_re * bb_im + e_im * bb_re
        lfull.append(blockdiag(l_re, l_im))
    kd = [lax.dot_general(l, c0t, (((1,), (1,)), ((), ())),
                          precision=lax.Precision.HIGHEST,
                          preferred_element_type=F32) for l in lfull]
    zero = jnp.zeros((LANES, LANES), F32)
    for s in range(CHUNK):
        row = jnp.concatenate([kd[t - s] if t >= s else zero for t in range(CHUNK)], axis=1)
        w_ref[s * LANES:(s + 1) * LANES, :] = row.astype(BF16)
        win_ref[s * LANES:(s + 1) * LANES, :] = lfull[CHUNK - 1 - s].astype(BF16)
    for t in range(CHUNK):
        e_re, e_im = power(float(t + 1))
        cpt_ref[t * LANES:(t + 1) * LANES, :] = blockdiag(
            c_re * e_re - c_im * e_im, -c_re * e_im - c_im * e_re).astype(BF16)

    af = aflat_ref[...]
    ai = iflat_ref[...]
    dtf = jnp.exp(dtflat_ref[...])
    for ref, d in ((a8_ref, float(CHUNK)), (a1_ref, 1.0)):
        mag = jnp.exp(af * dtf * d)
        ref[...] = jnp.concatenate([mag * jnp.cos(ai * dtf * d), mag * jnp.sin(ai * dtf * d)],
                                   axis=1)


def _ssm_prep(a_re, a_im, log_dt, b_re, b_im, c_re, c_im):
    rep = lambda t: jnp.repeat(t, SSM_GROUP, axis=0)
    are_x = rep(a_re)
    aim_x = rep(a_im)
    dt_x = rep(log_dt[:, None])
    bt_re = jnp.transpose(b_re, (0, 2, 1)).reshape(N_GROUPS * SSM_GROUP, SSM_STATE)
    bt_im = jnp.transpose(b_im, (0, 2, 1)).reshape(N_GROUPS * SSM_GROUP, SSM_STATE)
    cr = c_re.reshape(N_GROUPS * SSM_GROUP, SSM_STATE)
    ci = c_im.reshape(N_GROUPS * SSM_GROUP, SSM_STATE)
    aflat = a_re.reshape(N_GB, 1, GB_ST)
    iflat = a_im.reshape(N_GB, 1, GB_ST)
    dtflat = jnp.repeat(log_dt, SSM_STATE).reshape(N_GB, 1, GB_ST)
    rows = pl.BlockSpec((LANES, SSM_STATE), lambda g: (g, 0))
    flat = pl.BlockSpec((None, 1, GB_ST), lambda g: (g, 0, 0))
    big = pl.BlockSpec((None, GB_IN, GB_ST2), lambda g: (g, 0, 0))
    vec = pl.BlockSpec((None, 1, GB_ST2), lambda g: (g, 0, 0))
    return pl.pallas_call(
        _ssm_prep_kernel,
        grid=(N_GB,),
        in_specs=[rows, rows, pl.BlockSpec((LANES, 1), lambda g: (g, 0)),
                  rows, rows, rows, rows, flat, flat, flat],
        out_specs=[big, big, big,
                   pl.BlockSpec((None, LANES, GB_ST2), lambda g: (g, 0, 0)), vec, vec],
        out_shape=[
            jax.ShapeDtypeStruct((N_GB, GB_IN, GB_IN), BF16),
            jax.ShapeDtypeStruct((N_GB, GB_IN, GB_ST2), BF16),
            jax.ShapeDtypeStruct((N_GB, GB_IN, GB_ST2), BF16),
            jax.ShapeDtypeStruct((N_GB, LANES, GB_ST2), BF16),
            jax.ShapeDtypeStruct((N_GB, 1, GB_ST2), F32),
            jax.ShapeDtypeStruct((N_GB, 1, GB_ST2), F32),
        ],
        compiler_params=_cparams(("arbitrary",)),
        name="ssm_prep",
    )(are_x, aim_x, dt_x, bt_re, bt_im, cr, ci, aflat, iflat, dtflat)


SSM_TM = 512
SSM_TC = SSM_TM // CHUNK


def _ssm_pre_kernel(x_ref, g_ref, uc_ref, hs):
    h = _rms(x_ref[...], g_ref[...])
    for gb in range(N_GB):
        hs[gb] = h[:, gb * LANES:(gb + 1) * LANES]
    for gb in range(N_GB):
        for t in range(CHUNK):
            uc_ref[:, gb * GB_IN + t * LANES:gb * GB_IN + (t + 1) * LANES] = (
                hs[gb, pl.ds(t, SSM_TC, stride=CHUNK), :].astype(BF16))


def _ssm_pre(x, gain):
    b, l, d = x.shape
    nt = l // SSM_TM
    return pl.pallas_call(
        _ssm_pre_kernel,
        grid=(b, nt),
        in_specs=[pl.BlockSpec((None, SSM_TM, d), lambda bi, i: (bi, i, 0)), _const_spec((1, d))],
        out_specs=pl.BlockSpec((SSM_TC, CHUNK * d), lambda bi, i: (bi * nt + i, 0)),
        out_shape=jax.ShapeDtypeStruct((b * l // CHUNK, CHUNK * d), BF16),
        scratch_shapes=[pltpu.VMEM((N_GB, SSM_TM, LANES), F32)],
        compiler_params=_cparams(("arbitrary", "arbitrary")),
        name="ssm_pre",
    )(x, gain)


SSM_RT = 256


def _ssm_core_kernel(uc_ref, w_ref, win_ref, cpt_ref, a8_ref, yc_ref, sre_ref, sim_ref, zs,
                     *, n_seq, n_chunk):
    n_rows = n_seq * n_chunk
    nlb = GB_ST2 // LANES
    half = nlb // 2
    for r in range(n_rows // SSM_RT):
        rs = slice(r * SSM_RT, (r + 1) * SSM_RT)
        z = jnp.dot(uc_ref[rs, :], win_ref[...], preferred_element_type=F32)
        for c in range(nlb):
            zs[c, rs, :] = z[:, c * LANES:(c + 1) * LANES]

    a8 = a8_ref[...]
    a_bl = [jnp.broadcast_to(a8[:, c * LANES:(c + 1) * LANES], (n_seq, LANES))
            for c in range(nlb)]

    def step(k, carry):
        rows = pl.ds(k, n_seq, stride=n_chunk)
        new = []
        for c in range(half):
            s_re, s_im = carry[c], carry[half + c]
            z_re = zs[c, rows, :]
            z_im = zs[half + c, rows, :]
            zs[c, rows, :] = s_re
            zs[half + c, rows, :] = s_im
            new.append((a_bl[c] * s_re - a_bl[half + c] * s_im + z_re,
                        a_bl[c] * s_im + a_bl[half + c] * s_re + z_im))
        return tuple(n[0] for n in new) + tuple(n[1] for n in new)

    zero = jnp.zeros((n_seq, LANES), F32)
    fin = lax.fori_loop(0, n_chunk, step, (zero,) * nlb)
    sre_ref[...] = jnp.concatenate(fin[:half], axis=1)
    sim_ref[...] = jnp.concatenate(fin[half:], axis=1)

    for r in range(n_rows // SSM_RT):
        rs = slice(r * SSM_RT, (r + 1) * SSM_RT)
        sp = jnp.concatenate([zs[c, rs, :] for c in range(nlb)], axis=1).astype(BF16)
        y = jnp.dot(uc_ref[rs, :], w_ref[...], preferred_element_type=F32)
        y = y + lax.dot_general(sp, cpt_ref[...], (((1,), (1,)), ((), ())),
                                preferred_element_type=F32)
        yc_ref[rs, :] = y


def _ssm_core(uc, w, win, cpt, a8, n_seq):
    n_rows = uc.shape[0]
    n_chunk = n_rows // n_seq
    mat = lambda r: pl.BlockSpec((None, r, GB_ST2), lambda g: (g, 0, 0))
    blk = pl.BlockSpec((n_rows, GB_IN), lambda g: (0, g))
    st = pl.BlockSpec((n_seq, GB_ST), lambda g: (0, g))
    return pl.pallas_call(
        functools.partial(_ssm_core_kernel, n_seq=n_seq, n_chunk=n_chunk),
        grid=(N_GB,),
        in_specs=[blk, mat(GB_IN), mat(GB_IN), mat(GB_IN), mat(1)],
        out_specs=[blk, st, st],
        out_shape=[
            jax.ShapeDtypeStruct((n_rows, N_GB * GB_IN), F32),
            jax.ShapeDtypeStruct((n_seq, N_GROUPS * SSM_STATE), F32),
            jax.ShapeDtypeStruct((n_seq, N_GROUPS * SSM_STATE), F32),
        ],
        scratch_shapes=[pltpu.VMEM((GB_ST2 // LANES, n_rows, LANES), F32)],
        compiler_params=_cparams(("arbitrary",)),
        name="ssm_core",
    )(uc, w, win, cpt, a8)


def _glu_tail(x, y, g_ref, dsk_ref, wglu_ref):
    h = _rms(x, g_ref[...])
    z = jax.nn.gelu(y + dsk_ref[...] * h).astype(BF16)
    zz = jnp.dot(z, wglu_ref[...], preferred_element_type=F32)
    return x + zz[:, :D_MODEL] * jax.nn.sigmoid(zz[:, D_MODEL:])


def _ssm_post_kernel(x_ref, yc_ref, g_ref, dsk_ref, wglu_ref, xo_ref, ys):
    for gb in range(N_GB):
        for t in range(CHUNK):
            ys[gb, pl.ds(t, SSM_TC, stride=CHUNK), :] = (
                yc_ref[:, gb * GB_IN + t * LANES:gb * GB_IN + (t + 1) * LANES])
    y = jnp.concatenate([ys[gb] for gb in range(N_GB)], axis=1)
    xo_ref[...] = _glu_tail(x_ref[...], y, g_ref, dsk_ref, wglu_ref)


def _ssm_post(x, yc, gain, dsk, wglu):
    b, l, d = x.shape
    nt = l // SSM_TM
    tok = pl.BlockSpec((None, SSM_TM, d), lambda bi, i: (bi, i, 0))
    return pl.pallas_call(
        _ssm_post_kernel,
        grid=(b, nt),
        in_specs=[tok, pl.BlockSpec((SSM_TC, CHUNK * d), lambda bi, i: (bi * nt + i, 0)),
                  _const_spec((1, d)), _const_spec((1, d)), _const_spec((d, 2 * d))],
        out_specs=tok,
        out_shape=jax.ShapeDtypeStruct((b, l, d), F32),
        scratch_shapes=[pltpu.VMEM((N_GB, SSM_TM, LANES), F32)],
        compiler_params=_cparams(("arbitrary", "arbitrary")),
        name="ssm_post",
    )(x, yc, gain, dsk, wglu)


def _ssm_sample_kernel(x_ref, g_ref, bb_ref, c0t_ref, a1_ref, hre_ref, him_ref,
                       y_ref, sre_ref, sim_ref):
    gb = pl.program_id(0)
    x = x_ref[...]
    ms = jnp.mean(x * x, axis=-1, keepdims=True)
    cols = pl.ds(pl.multiple_of(gb * LANES, LANES), LANES)
    u = (x_ref[:, cols] * lax.rsqrt(ms + RMS_EPS) * g_ref[:, cols]).astype(BF16)
    bu = jnp.dot(u, bb_ref[...], preferred_element_type=F32)
    a1 = a1_ref[...]
    a_re = a1[:, :GB_ST]
    a_im = a1[:, GB_ST:]
    h_re = hre_ref[...]
    h_im = him_ref[...]
    s_re = a_re * h_re - a_im * h_im + bu[:, :GB_ST]
    s_im = a_re * h_im + a_im * h_re + bu[:, GB_ST:]
    sre_ref[...] = s_re
    sim_ref[...] = s_im
    s = jnp.concatenate([s_re, s_im], axis=1).astype(BF16)
    y_ref[...] = lax.dot_general(s, c0t_ref[...], (((1,), (1,)), ((), ())),
                                 preferred_element_type=F32)


def _ssm_sample(x, gain, win, c0t, a1, h_re, h_im):
    n, d = x.shape
    st = pl.BlockSpec((n, GB_ST), lambda g: (0, g))
    return pl.pallas_call(
        _ssm_sample_kernel,
        grid=(N_GB,),
        in_specs=[
            _const_spec((n, d)),
            _const_spec((1, d)),
            pl.BlockSpec((None, LANES, GB_ST2), lambda g: (g, CHUNK - 1, 0)),
            pl.BlockSpec((None, LANES, GB_ST2), lambda g: (g, 0, 0)),
            pl.BlockSpec((None, 1, GB_ST2), lambda g: (g, 0, 0)),
            st, st,
        ],
        out_specs=[pl.BlockSpec((n, LANES), lambda g: (0, g)), st, st],
        out_shape=[
            jax.ShapeDtypeStruct((n, d), F32),
            jax.ShapeDtypeStruct((n, N_GROUPS * SSM_STATE), F32),
            jax.ShapeDtypeStruct((n, N_GROUPS * SSM_STATE), F32),
        ],
        compiler_params=_cparams(("arbitrary",)),
        name="ssm_sample",
    )(x, gain, win, c0t, a1, h_re, h_im)


def _glu_sample_kernel(x_ref, y_ref, g_ref, dsk_ref, wglu_ref, xo_ref):
    xo_ref[...] = _glu_tail(x_ref[...], y_ref[...], g_ref, dsk_ref, wglu_ref)


def _glu_sample(x, y, gain, dsk, wglu):
    n, d = x.shape
    return pl.pallas_call(
        _glu_sample_kernel,
        grid=(1,),
        in_specs=[_const_spec((n, d)), _const_spec((n, d)), _const_spec((1, d)),
                  _const_spec((1, d)), _const_spec((d, 2 * d))],
        out_specs=_const_spec((n, d)),
        out_shape=jax.ShapeDtypeStruct((n, d), F32),
        compiler_params=_cparams(("arbitrary",)),
        name="glu_sample",
    )(x, y, gain, dsk, wglu)


def kernel(x_prompt, x_sample, cache_k, cache_v, state_ssm_re, state_ssm_im, state_conv,
           norm_mix, norm_ffn, w_qkv, w_o, q_norm, k_norm, sinks, ssm_a_re, ssm_a_im,
           ssm_log_dt, ssm_b_re, ssm_b_im, ssm_c_re, ssm_c_im, ssm_d, w_glu, w_up, conv_w,
           conv_b, w_down):
    depth = norm_mix.shape[0]
    nb, _, _ = x_prompt.shape
    ns = x_sample.shape[0]
    xp = x_prompt
    xs = x_sample.reshape(ns, D_MODEL)
    row = lambda t: t.reshape(1, -1)

    kps, vps, kss, vss = [], [], [], []
    srp, sip, srs, sis = [], [], [], []
    cps, css = [], []
    for i in range(depth):
        j = i // 2
        gm = row(norm_mix[i])
        if i % 2 == 0:
            wqkv = w_qkv[j].astype(BF16)
            wo = w_o[j].astype(BF16)
            qg, kg = row(q_norm[j]), row(k_norm[j])
            xp, kp, vp = _attn_prompt(xp, gm, wqkv, wo, qg, kg, sinks[j])
            wq_g = wqkv[:, :D_MODEL].reshape(D_MODEL, N_KV, GQA, HEAD_DIM)
            wq_g = jnp.transpose(wq_g, (2, 0, 1, 3)).reshape(GQA, D_MODEL, KV_DIM)
            wo_g = wo.reshape(N_KV, GQA, HEAD_DIM, D_MODEL)
            wo_g = jnp.transpose(wo_g, (1, 0, 2, 3)).reshape(GQA, KV_DIM, D_MODEL)
            ck = cache_k[j].reshape(ns, WINDOW, KV_DIM)
            cv = cache_v[j].reshape(ns, WINDOW, KV_DIM)
            xs, ks, vs = _attn_sample(xs, gm, wq_g, wqkv[:, D_MODEL:], wo_g, qg, kg, sinks[j],
                                      ck, cv)
            kps.append(kp.reshape(nb, WINDOW, N_KV, HEAD_DIM))
            vps.append(vp.reshape(nb, WINDOW, N_KV, HEAD_DIM))
            kss.append(ks.reshape(ns, WINDOW, N_KV, HEAD_DIM))
            vss.append(vs.reshape(ns, WINDOW, N_KV, HEAD_DIM))
        else:
            w, win, cpt, c0t, a8, a1 = _ssm_prep(
                ssm_a_re[j], ssm_a_im[j], ssm_log_dt[j], ssm_b_re[j], ssm_b_im[j],
                ssm_c_re[j], ssm_c_im[j])
            wglu = w_glu[j].astype(BF16)
            dsk = row(ssm_d[j])
            uc = _ssm_pre(xp, gm)
            yc, s_re, s_im = _ssm_core(uc, w, win, cpt, a8, nb)
            xp = _ssm_post(xp, yc, gm, dsk, wglu)
            srp.append(s_re.reshape(nb, N_GROUPS, SSM_STATE))
            sip.append(s_im.reshape(nb, N_GROUPS, SSM_STATE))
            ys, t_re, t_im = _ssm_sample(
                xs, gm, win, c0t, a1,
                state_ssm_re[j].reshape(ns, N_GROUPS * SSM_STATE),
                state_ssm_im[j].reshape(ns, N_GROUPS * SSM_STATE))
            xs = _glu_sample(xs, ys, gm, dsk, wglu)
            srs.append(t_re.reshape(ns, N_GROUPS, SSM_STATE))
            sis.append(t_im.reshape(ns, N_GROUPS, SSM_STATE))

        gf = row(norm_ffn[i])
        wup = w_up[i].astype(BF16)
        wdn = w_down[i].astype(BF16)
        cb = row(conv_b[i])
        xp, cp = _ffn_prompt(xp, gf, wup, conv_w[i], cb, wdn)
        buf = state_conv[i].reshape(ns, 2 * F2)
        xs, up_new = _ffn_sample(xs, gf, wup, conv_w[i], cb, wdn, buf)
        cps.append(cp)
        css.append(jnp.stack([state_conv[i][:, 1, :], up_new], axis=1))

    return (xp, xs.reshape(ns, 1, D_MODEL),
            jnp.stack(kps), jnp.stack(vps), jnp.stack(kss), jnp.stack(vss),
            jnp.stack(srp), jnp.stack(sip), jnp.stack(srs), jnp.stack(sis),
            jnp.stack(cps), jnp.stack(css))
```

```python
import functools

import jax
import jax.numpy as jnp
from jax import lax
from jax.experimental import pallas as pl
from jax.experimental.pallas import tpu as pltpu

F32 = jnp.float32
BF16 = jnp.bfloat16

D_MODEL = 1024
HEAD_DIM = 64
N_HEADS = 16
N_KV = 4
GQA = 4
WINDOW = 128
KV_DIM = N_KV * HEAD_DIM
QKV_DIM = D_MODEL + 2 * KV_DIM
D_FF = 2816
F2 = 2 * D_FF
FF_BLK = 256
N_FF_BLK = D_FF // FF_BLK
SSM_GROUP = 16
N_GROUPS = 64
SSM_STATE = 64
RMS_EPS = 1e-6

LANES = 128
CHUNK = 8
GB_GROUPS = LANES // SSM_GROUP
N_GB = N_GROUPS // GB_GROUPS
GB_IN = CHUNK * LANES
GB_ST = GB_GROUPS * SSM_STATE
GB_ST2 = 2 * GB_ST

VMEM_LIMIT = 56 * 1024 * 1024


def _cparams(sem):
    return pltpu.CompilerParams(dimension_semantics=sem, vmem_limit_bytes=VMEM_LIMIT)


def _rms(x, g):
    ms = jnp.mean(x * x, axis=-1, keepdims=True)
    return x * lax.rsqrt(ms + RMS_EPS) * g


def _head_rms(t, n, g):
    return jnp.concatenate(
        [_rms(t[:, j * HEAD_DIM:(j + 1) * HEAD_DIM], g) for j in range(n)], axis=1)


def _const_spec(shape):
    nd = len(shape)
    return pl.BlockSpec(shape, lambda *_: (0,) * nd)


ATT_TQ = 512
ATT_NBLK = ATT_TQ // WINDOW
SEG_W = 256


def _seg_mean_sq(t, seg):
    sq = t * t
    hi = sq.astype(BF16)
    lo = (sq - hi.astype(F32)).astype(BF16)
    out = []
    for c in range(t.shape[1] // SEG_W):
        sl = slice(c * SEG_W, (c + 1) * SEG_W)
        out.append(jnp.dot(hi[:, sl], seg, preferred_element_type=F32)
                   + jnp.dot(lo[:, sl], seg, preferred_element_type=F32))
    return jnp.concatenate(out, axis=1) * (1.0 / HEAD_DIM)


VT_ROWS = HEAD_DIM + 16


def _attn_prompt_kernel(sink_ref, x_ref, g_ref, wqkv_ref, wot_ref, qg_ref, kg_ref,
                        xo_ref, kl_ref, vl_ref, kbuf, krol, vt, lo_scr, ot):
    i = pl.program_id(1)
    keys = 2 * WINDOW

    @pl.when(i == 0)
    def _():
        kbuf[0:WINDOW, :] = jnp.zeros((WINDOW, KV_DIM), BF16)
        krol[0:WINDOW, :] = jnp.zeros((WINDOW, KV_DIM), BF16)
        vt[:, 0:HEAD_DIM, 0:WINDOW] = jnp.zeros((N_KV, HEAD_DIM, WINDOW), BF16)
        vt[:, HEAD_DIM:, :] = jnp.ones((N_KV, VT_ROWS - HEAD_DIM, WINDOW + ATT_TQ), BF16)
        lo_scr[...] = jnp.full(lo_scr.shape, WINDOW, jnp.int32)

    @pl.when(i == 1)
    def _():
        lo_scr[...] = jnp.zeros(lo_scr.shape, jnp.int32)

    si = lax.broadcasted_iota(jnp.int32, (SEG_W, SEG_W), 0) // HEAD_DIM
    sj = lax.broadcasted_iota(jnp.int32, (SEG_W, SEG_W), 1) // HEAD_DIM
    seg = jnp.where(si == sj, 1.0, 0.0).astype(BF16)

    x = x_ref[...]
    h = _rms(x, g_ref[...]).astype(BF16)
    qkv = jnp.dot(h, wqkv_ref[...], preferred_element_type=F32)
    q = qkv[:, :D_MODEL]
    k = qkv[:, D_MODEL:D_MODEL + KV_DIM]
    v = qkv[:, D_MODEL + KV_DIM:]
    qn = q * lax.rsqrt(_seg_mean_sq(q, seg) + RMS_EPS) * qg_ref[...] * (HEAD_DIM ** -0.5)
    kn = k * lax.rsqrt(_seg_mean_sq(k, seg) + RMS_EPS) * kg_ref[...]
    kl_ref[...] = kn[ATT_TQ - WINDOW:]
    vl_ref[...] = v[ATT_TQ - WINDOW:]
    kbuf[WINDOW:, :] = kn.astype(BF16)
    for c in range(KV_DIM // LANES):
        sl = slice(c * LANES, (c + 1) * LANES)
        krol[WINDOW:, sl] = pltpu.roll(kn[:, sl], HEAD_DIM, 1).astype(BF16)
    v_t = v.T
    for kvh in range(N_KV):
        vt[kvh, 0:HEAD_DIM, WINDOW:] = v_t[kvh * HEAD_DIM:(kvh + 1) * HEAD_DIM].astype(BF16)

    kc = lax.broadcasted_iota(jnp.int32, (keys, 2 * WINDOW), 0)
    qi = lax.broadcasted_iota(jnp.int32, (keys, 2 * WINDOW), 1) % WINDOW
    band = (kc > qi) & (kc <= qi + WINDOW)
    low = lax.broadcasted_iota(jnp.int32, (WINDOW, LANES), 1) < HEAD_DIM
    nt = (((1,), (1,)), ((), ()))

    for blk in range(ATT_NBLK):
        r0 = blk * WINDOW
        valid = band & (kc >= lo_scr[...]) if blk == 0 else band
        combos = []
        for kvh in range(N_KV):
            khalf = kvh % 2
            combos.append((kvh, (khalf, khalf + 2), kbuf))
            combos.append((kvh, (1 - khalf, 3 - khalf), krol))
        scores = []
        for kvh, pair, kref in combos:
            kcol = slice((kvh // 2) * LANES, (kvh // 2 + 1) * LANES)
            qm = []
            for g in pair:
                qcol = slice((kvh * 2 + g // 2) * LANES, (kvh * 2 + g // 2 + 1) * LANES)
                qm.append(jnp.where(low if g % 2 == 0 else ~low, qn[r0:r0 + WINDOW, qcol], 0.0))
            scores.append(lax.dot_general(kref[r0:r0 + keys, kcol],
                                          jnp.concatenate(qm, axis=0).astype(BF16), nt,
                                          preferred_element_type=F32))
        probs = []
        for (kvh, pair, _), s in zip(combos, scores):
            sink_row = jnp.concatenate(
                [jnp.full((1, WINDOW), sink_ref[kvh * GQA + g], F32) for g in pair], axis=1)
            s = jnp.where(valid, s, -jnp.inf)
            m = jnp.maximum(jnp.max(s, axis=0, keepdims=True), sink_row)
            probs.append((jnp.exp(s - m).astype(BF16), jnp.exp(sink_row - m)))
        outs = [jnp.dot(vt[kvh, :, r0:r0 + keys], e, preferred_element_type=F32)
                for (kvh, _, _), (e, _) in zip(combos, probs)]
        for (kvh, pair, _), (_, e_sink), o in zip(combos, probs, outs):
            rden = 1.0 / (o[HEAD_DIM:HEAD_DIM + 1] + e_sink)
            on = (o[:HEAD_DIM] * rden).astype(BF16)
            for n, g in enumerate(pair):
                hd = kvh * GQA + g
                ot[hd * HEAD_DIM:(hd + 1) * HEAD_DIM, r0:r0 + WINDOW] = (
                    on[:, n * WINDOW:(n + 1) * WINDOW])

    kbuf[0:WINDOW, :] = kbuf[ATT_TQ:, :]
    krol[0:WINDOW, :] = krol[ATT_TQ:, :]
    vt[:, 0:HEAD_DIM, 0:WINDOW] = vt[:, 0:HEAD_DIM, ATT_TQ:]
    out_t = jnp.dot(wot_ref[...], ot[...], preferred_element_type=F32)
    xo_ref[...] = x + out_t.T


def _attn_prompt(x, gain, wqkv, wo_t, qg, kg, sinks):
    b, l, d = x.shape
    tok = pl.BlockSpec((None, ATT_TQ, d), lambda bi, i: (bi, i, 0))
    last = pl.BlockSpec((None, WINDOW, KV_DIM), lambda bi, i: (bi, 0, 0))
    kvbuf = pltpu.VMEM((WINDOW + ATT_TQ, KV_DIM), BF16)
    return pl.pallas_call(
        _attn_prompt_kernel,
        grid=(b, l // ATT_TQ),
        in_specs=[
            pl.BlockSpec(memory_space=pltpu.SMEM),
            tok,
            _const_spec((1, d)),
            _const_spec((d, QKV_DIM)),
            _const_spec((d, d)),
            _const_spec((1, d)),
            _const_spec((1, KV_DIM)),
        ],
        out_specs=[tok, last, last],
        out_shape=[
            jax.ShapeDtypeStruct((b, l, d), F32),
            jax.ShapeDtypeStruct((b, WINDOW, KV_DIM), F32),
            jax.ShapeDtypeStruct((b, WINDOW, KV_DIM), F32),
        ],
        scratch_shapes=[
            kvbuf, kvbuf,
            pltpu.VMEM((N_KV, VT_ROWS, WINDOW + ATT_TQ), BF16),
            pltpu.VMEM((2 * WINDOW, 2 * WINDOW), jnp.int32),
            pltpu.VMEM((d, ATT_TQ), BF16),
        ],
        compiler_params=_cparams(("arbitrary", "arbitrary")),
        name="attn_prompt",
    )(sinks, x, gain, wqkv, wo_t, qg, kg)


ATS_TB = 8


def _attn_sample_kernel(sink_ref, x_ref, g_ref, wq_ref, wkv_ref, wo_ref, qg_ref, kg_ref,
                        ck_ref, cv_ref, xo_ref, ko_ref, vo_ref, r_scr):
    x = x_ref[...]
    h = _rms(x, g_ref[...]).astype(BF16)
    kv = jnp.dot(h, wkv_ref[...], preferred_element_type=F32)
    kn = _head_rms(kv[:, :KV_DIM], N_KV, kg_ref[...])
    v = kv[:, KV_DIM:]
    qg = qg_ref[...]
    q_g = [_head_rms(jnp.dot(h, wq_ref[g], preferred_element_type=F32), N_KV, qg)
           * (HEAD_DIM ** -0.5) for g in range(GQA)]

    rows = GQA * N_KV
    rkv = lax.broadcasted_iota(jnp.int32, (rows, KV_DIM), 0) % N_KV
    lkv = lax.broadcasted_iota(jnp.int32, (rows, KV_DIM), 1) // HEAD_DIM
    own = rkv == lkv
    colj = lax.broadcasted_iota(jnp.int32, (rows, WINDOW), 1)
    sink_col = jnp.concatenate(
        [jnp.full((1, 1), sink_ref[(r % N_KV) * GQA + r // N_KV], F32) for r in range(rows)],
        axis=0)

    for n in range(ATS_TB):
        kc = ck_ref[n]
        vc = cv_ref[n]
        k_new = kn[n:n + 1]
        v_new = v[n:n + 1]
        ko_ref[n, 0:WINDOW - 1, :] = ck_ref[n, 1:WINDOW, :]
        ko_ref[n, WINDOW - 1:WINDOW, :] = k_new
        vo_ref[n, 0:WINDOW - 1, :] = cv_ref[n, 1:WINDOW, :]
        vo_ref[n, WINDOW - 1:WINDOW, :] = v_new
        qbd = jnp.concatenate(
            [jnp.broadcast_to(q_g[g][n:n + 1], (N_KV, KV_DIM)) for g in range(GQA)], axis=0)
        qbd = jnp.where(own, qbd, 0.0)
        s = lax.dot_general(qbd.astype(BF16), kc.astype(BF16), (((1,), (1,)), ((), ())),
                            preferred_element_type=F32)
        s = jnp.where(colj >= 1, s, -jnp.inf)
        s_new = jnp.sum(qbd * k_new, axis=-1, keepdims=True)
        m = jnp.maximum(jnp.maximum(jnp.max(s, axis=-1, keepdims=True), s_new), sink_col)
        e = jnp.exp(s - m)
        e_new = jnp.exp(s_new - m)
        den = jnp.sum(e, axis=-1, keepdims=True) + e_new + jnp.exp(sink_col - m)
        p = (e / den).astype(BF16)
        o = jnp.dot(p, vc.astype(BF16), preferred_element_type=F32) + (e_new / den) * v_new
        o = jnp.where(own, o, 0.0)
        for g in range(GQA):
            r_scr[g, n:n + 1, :] = jnp.sum(o[g * N_KV:(g + 1) * N_KV], axis=0, keepdims=True)

    out = x
    for g in range(GQA):
        out = out + jnp.dot(r_scr[g].astype(BF16), wo_ref[g], preferred_element_type=F32)
    xo_ref[...] = out


def _attn_sample(x, gain, wq_g, wkv, wo_g, qg, kg, sinks, cache_k, cache_v):
    n, d = x.shape
    tok = pl.BlockSpec((ATS_TB, d), lambda i: (i, 0))
    cache = pl.BlockSpec((ATS_TB, WINDOW, KV_DIM), lambda i: (i, 0, 0))
    return pl.pallas_call(
        _attn_sample_kernel,
        grid=(n // ATS_TB,),
        in_specs=[
            pl.BlockSpec(memory_space=pltpu.SMEM),
            tok,
            _const_spec((1, d)),
            _const_spec((GQA, d, KV_DIM)),
            _const_spec((d, 2 * KV_DIM)),
            _const_spec((GQA, KV_DIM, d)),
            _const_spec((1, HEAD_DIM)),
            _const_spec((1, HEAD_DIM)),
            cache,
            cache,
        ],
        out_specs=[tok, cache, cache],
        out_shape=[
            jax.ShapeDtypeStruct((n, d), F32),
            jax.ShapeDtypeStruct((n, WINDOW, KV_DIM), F32),
            jax.ShapeDtypeStruct((n, WINDOW, KV_DIM), F32),
        ],
        scratch_shapes=[pltpu.VMEM((GQA, ATS_TB, KV_DIM), F32)],
        compiler_params=_cparams(("arbitrary",)),
        name="attn_sample",
    )(sinks, x, gain, wq_g, wkv, wo_g, qg, kg, cache_k, cache_v)


FFN_TM = 512
CARRY = 8


def _ffn_prompt_kernel(x_ref, g_ref, wup_ref, cw_ref, cb_ref, wdn_ref,
                       xo_ref, cs_ref, carry, upbuf, act):
    i = pl.program_id(1)

    @pl.when(i == 0)
    def _():
        carry[...] = jnp.zeros_like(carry)

    x = x_ref[...]
    h = _rms(x, g_ref[...]).astype(BF16)

    def conv_cols(c0):
        cols = pl.ds(c0, FF_BLK)
        up = jnp.dot(h, wup_ref[:, cols], preferred_element_type=F32)
        upbuf[0:CARRY, :] = carry[:, cols]
        upbuf[CARRY:, :] = up
        carry[:, cols] = up[FFN_TM - CARRY:]
        u1 = upbuf[CARRY - 1:CARRY - 1 + FFN_TM, :]
        u2 = upbuf[CARRY - 2:CARRY - 2 + FFN_TM, :]
        c = cb_ref[:, cols]
        c = c + cw_ref[0:1, cols] * u2
        c = c + cw_ref[1:2, cols] * u1
        c = c + cw_ref[2:3, cols] * up
        return c

    for j in range(N_FF_BLK):
        cg = conv_cols(j * FF_BLK)
        cv = conv_cols(D_FF + j * FF_BLK)
        act[:, j * FF_BLK:(j + 1) * FF_BLK] = (jax.nn.silu(cg) * cv).astype(BF16)

    out = jnp.dot(act[...], wdn_ref[...], preferred_element_type=F32)
    xo_ref[...] = x + out
    cs_ref[...] = carry[CARRY - 2:CARRY, :]


def _ffn_prompt(x, gain, wup, cw, cb, wdn):
    b, l, d = x.shape
    tok = pl.BlockSpec((None, FFN_TM, d), lambda bi, i: (bi, i, 0))
    return pl.pallas_call(
        _ffn_prompt_kernel,
        grid=(b, l // FFN_TM),
        in_specs=[
            tok,
            _const_spec((1, d)),
            pl.BlockSpec((d, F2), lambda bi, i: (0, 0), pipeline_mode=pl.Buffered(1)),
            _const_spec((3, F2)),
            _const_spec((1, F2)),
            pl.BlockSpec((D_FF, d), lambda bi, i: (0, 0), pipeline_mode=pl.Buffered(1)),
        ],
        out_specs=[tok, pl.BlockSpec((None, 2, F2), lambda bi, i: (bi, 0, 0))],
        out_shape=[
            jax.ShapeDtypeStruct((b, l, d), F32),
            jax.ShapeDtypeStruct((b, 2, F2), F32),
        ],
        scratch_shapes=[
            pltpu.VMEM((CARRY, F2), F32),
            pltpu.VMEM((CARRY + FFN_TM, FF_BLK), F32),
            pltpu.VMEM((FFN_TM, D_FF), BF16),
        ],
        compiler_params=_cparams(("arbitrary", "arbitrary")),
        name="ffn_prompt",
    )(x, gain, wup, cw, cb, wdn)


def _ffn_sample_kernel(x_ref, g_ref, wg_ref, wv_ref, cwg_ref, cwv_ref, cbg_ref, cbv_ref,
                       wdn_ref, b0g_ref, b0v_ref, b1g_ref, b1v_ref,
                       xo_ref, ug_ref, uv_ref):
    j = pl.program_id(0)

    @pl.when(j == 0)
    def _():
        xo_ref[...] = x_ref[...]

    h = _rms(x_ref[...], g_ref[...]).astype(BF16)
    upg = jnp.dot(h, wg_ref[...], preferred_element_type=F32)
    upv = jnp.dot(h, wv_ref[...], preferred_element_type=F32)
    ug_ref[...] = upg
    uv_ref[...] = upv
    cg = cbg_ref[...]
    cg = cg + cwg_ref[0:1, :] * b0g_ref[...]
    cg = cg + cwg_ref[1:2, :] * b1g_ref[...]
    cg = cg + cwg_ref[2:3, :] * upg
    cv = cbv_ref[...]
    cv = cv + cwv_ref[0:1, :] * b0v_ref[...]
    cv = cv + cwv_ref[1:2, :] * b1v_ref[...]
    cv = cv + cwv_ref[2:3, :] * upv
    a = (jax.nn.silu(cg) * cv).astype(BF16)
    xo_ref[...] += jnp.dot(a, wdn_ref[...], preferred_element_type=F32)


def _ffn_sample(x, gain, wup, cw, cb, wdn, buf):
    n, d = x.shape
    nb = N_FF_BLK
    gcol = lambda j: (0, j)
    vcol = lambda j: (0, nb + j)
    b1g = lambda j: (0, 2 * nb + j)
    b1v = lambda j: (0, 3 * nb + j)
    blk = lambda r: (r, FF_BLK)
    xo, ug, uv = pl.pallas_call(
        _ffn_sample_kernel,
        grid=(nb,),
        in_specs=[
            _const_spec((n, d)),
            _const_spec((1, d)),
            pl.BlockSpec(blk(d), gcol),
            pl.BlockSpec(blk(d), vcol),
            pl.BlockSpec(blk(3), gcol),
            pl.BlockSpec(blk(3), vcol),
            pl.BlockSpec(blk(1), gcol),
            pl.BlockSpec(blk(1), vcol),
            pl.BlockSpec((FF_BLK, d), lambda j: (j, 0)),
            pl.BlockSpec(blk(n), gcol),
            pl.BlockSpec(blk(n), vcol),
            pl.BlockSpec(blk(n), b1g),
            pl.BlockSpec(blk(n), b1v),
        ],
        out_specs=[
            _const_spec((n, d)),
            pl.BlockSpec(blk(n), gcol),
            pl.BlockSpec(blk(n), gcol),
        ],
        out_shape=[
            jax.ShapeDtypeStruct((n, d), F32),
            jax.ShapeDtypeStruct((n, D_FF), F32),
            jax.ShapeDtypeStruct((n, D_FF), F32),
        ],
        compiler_params=_cparams(("arbitrary",)),
        name="ffn_sample",
    )(x, gain, wup, wup, cw, cw, cb, cb, wdn, buf, buf, buf, buf)
    return xo, jnp.concatenate([ug, uv], axis=1)


def _tile_states(t):
    t2 = jnp.concatenate([t, t], axis=1)
    return jnp.concatenate([t2] * (GB_GROUPS // 2), axis=1)


def _ssm_prep_kernel(are_ref, aim_ref, dt_ref, bre_ref, bim_ref, cre_ref, cim_ref,
                     aflat_ref, iflat_ref, dtflat_ref,
                     w_ref, win_ref, cpt_ref, c0t_ref, a8_ref, a1_ref):
    a_re = are_ref[...]
    a_im = aim_ref[...]
    dt = jnp.exp(dt_ref[...])

    def power(d):
        mag = jnp.exp(a_re * dt * d)
        return mag * jnp.cos(a_im * dt * d), mag * jnp.sin(a_im * dt * d)

    ab_re, ab_im = power(1.0)
    den = a_re * a_re + a_im * a_im
    z_re = ((ab_re - 1.0) * a_re + ab_im * a_im) / den
    z_im = (ab_im * a_re - (ab_re - 1.0) * a_im) / den
    b_re = bre_ref[...]
    b_im = bim_ref[...]
    bb_re = z_re * b_re - z_im * b_im
    bb_im = z_re * b_im + z_im * b_re
    c_re = cre_ref[...]
    c_im = cim_ref[...]

    rg = lax.broadcasted_iota(jnp.int32, (LANES, GB_ST2), 0) // SSM_GROUP
    lg = (lax.broadcasted_iota(jnp.int32, (LANES, GB_ST2), 1) % GB_ST) // SSM_STATE
    own = rg == lg

    def blockdiag(t_re, t_im):
        full = jnp.concatenate([_tile_states(t_re), _tile_states(t_im)], axis=1)
        return jnp.where(own, full, 0.0)

    c0t = blockdiag(c_re, -c_im)
    c0t_ref[...] = c0t.astype(BF16)

    lfull = []
    for d in range(CHUNK):
        if d == 0:
            l_re, l_im = bb_re, bb_im
        else:
            e_re, e_im = power(float(d))
            l_re = e_re * bb_re - e_im * bb_im
            l_im = e_re * bb_im + e_im * bb_re
        lfull.append(blockdiag(l_re, l_im))
    kd = [lax.dot_general(l, c0t, (((1,), (1,)), ((), ())),
                          precision=lax.Precision.HIGHEST,
                          preferred_element_type=F32) for l in lfull]
    zero = jnp.zeros((LANES, LANES), F32)
    for s in range(CHUNK):
        row = jnp.concatenate([kd[t - s] if t >= s else zero for t in range(CHUNK)], axis=1)
        w_ref[s * LANES:(s + 1) * LANES, :] = row.astype(BF16)
        win_ref[s * LANES:(s + 1) * LANES, :] = lfull[CHUNK - 1 - s].astype(BF16)
    for t in range(CHUNK):
        e_re, e_im = power(float(t + 1))
        cpt_ref[t * LANES:(t + 1) * LANES, :] = blockdiag(
            c_re * e_re - c_im * e_im, -c_re * e_im - c_im * e_re).astype(BF16)

    af = aflat_ref[...]
    ai = iflat_ref[...]
    dtf = jnp.exp(dtflat_ref[...])
    for ref, d in ((a8_ref, float(CHUNK)), (a1_ref, 1.0)):
        mag = jnp.exp(af * dtf * d)
        ref[...] = jnp.concatenate([mag * jnp.cos(ai * dtf * d), mag * jnp.sin(ai * dtf * d)],
                                   axis=1)


def _ssm_prep(a_re, a_im, log_dt, b_re, b_im, c_re, c_im):
    rep = lambda t: jnp.repeat(t, SSM_GROUP, axis=0)
    are_x = rep(a_re)
    aim_x = rep(a_im)
    dt_x = rep(log_dt[:, None])
    bt_re = jnp.transpose(b_re, (0, 2, 1)).reshape(N_GROUPS * SSM_GROUP, SSM_STATE)
    bt_im = jnp.transpose(b_im, (0, 2, 1)).reshape(N_GROUPS * SSM_GROUP, SSM_STATE)
    cr = c_re.reshape(N_GROUPS * SSM_GROUP, SSM_STATE)
    ci = c_im.reshape(N_GROUPS * SSM_GROUP, SSM_STATE)
    aflat = a_re.reshape(N_GB, 1, GB_ST)
    iflat = a_im.reshape(N_GB, 1, GB_ST)
    dtflat = jnp.repeat(log_dt, SSM_STATE).reshape(N_GB, 1, GB_ST)
    rows = pl.BlockSpec((LANES, SSM_STATE), lambda g: (g, 0))
    flat = pl.BlockSpec((None, 1, GB_ST), lambda g: (g, 0, 0))
    big = pl.BlockSpec((None, GB_IN, GB_ST2), lambda g: (g, 0, 0))
    vec = pl.BlockSpec((None, 1, GB_ST2), lambda g: (g, 0, 0))
    return pl.pallas_call(
        _ssm_prep_kernel,
        grid=(N_GB,),
        in_specs=[rows, rows, pl.BlockSpec((LANES, 1), lambda g: (g, 0)),
                  rows, rows, rows, rows, flat, flat, flat],
        out_specs=[big, big, big,
                   pl.BlockSpec((None, LANES, GB_ST2), lambda g: (g, 0, 0)), vec, vec],
        out_shape=[
            jax.ShapeDtypeStruct((N_GB, GB_IN, GB_IN), BF16),
            jax.ShapeDtypeStruct((N_GB, GB_IN, GB_ST2), BF16),
            jax.ShapeDtypeStruct((N_GB, GB_IN, GB_ST2), BF16),
            jax.ShapeDtypeStruct((N_GB, LANES, GB_ST2), BF16),
            jax.ShapeDtypeStruct((N_GB, 1, GB_ST2), F32),
            jax.ShapeDtypeStruct((N_GB, 1, GB_ST2), F32),
        ],
        compiler_params=_cparams(("arbitrary",)),
        name="ssm_prep",
    )(are_x, aim_x, dt_x, bt_re, bt_im, cr, ci, aflat, iflat, dtflat)


SSM_TM = 512
SSM_TC = SSM_TM // CHUNK


def _ssm_pre_kernel(x_ref, g_ref, uc_ref, hs):
    h = _rms(x_ref[...], g_ref[...])
    for gb in range(N_GB):
        hs[gb] = h[:, gb * LANES:(gb + 1) * LANES]
    for gb in range(N_GB):
        for t in range(CHUNK):
            uc_ref[:, gb * GB_IN + t * LANES:gb * GB_IN + (t + 1) * LANES] = (
                hs[gb, pl.ds(t, SSM_TC, stride=CHUNK), :].astype(BF16))


def _ssm_pre(x, gain):
    b, l, d = x.shape
    nt = l // SSM_TM
    return pl.pallas_call(
        _ssm_pre_kernel,
        grid=(b, nt),
        in_specs=[pl.BlockSpec((None, SSM_TM, d), lambda bi, i: (bi, i, 0)), _const_spec((1, d))],
        out_specs=pl.BlockSpec((SSM_TC, CHUNK * d), lambda bi, i: (bi * nt + i, 0)),
        out_shape=jax.ShapeDtypeStruct((b * l // CHUNK, CHUNK * d), BF16),
        scratch_shapes=[pltpu.VMEM((N_GB, SSM_TM, LANES), F32)],
        compiler_params=_cparams(("arbitrary", "arbitrary")),
        name="ssm_pre",
    )(x, gain)


SSM_RT = 256


def _ssm_core_kernel(uc_ref, w_ref, win_ref, cpt_ref, a8_ref, yc_ref, sre_ref, sim_ref, zs,
                     *, n_seq, n_chunk):
    n_rows = n_seq * n_chunk
    nlb = GB_ST2 // LANES
    half = nlb // 2
    for r in range(n_rows // SSM_RT):
        rs = slice(r * SSM_RT, (r + 1) * SSM_RT)
        z = jnp.dot(uc_ref[rs, :], win_ref[...], preferred_element_type=F32)
        for c in range(nlb):
            zs[c, rs, :] = z[:, c * LANES:(c + 1) * LANES]

    a8 = a8_ref[...]
    a_bl = [jnp.broadcast_to(a8[:, c * LANES:(c + 1) * LANES], (n_seq, LANES))
            for c in range(nlb)]

    def step(k, carry):
        rows = pl.ds(k, n_seq, stride=n_chunk)
        new = []
        for c in range(half):
            s_re, s_im = carry[c], carry[half + c]
            z_re = zs[c, rows, :]
            z_im = zs[half + c, rows, :]
            zs[c, rows, :] = s_re
            zs[half + c, rows, :] = s_im
            new.append((a_bl[c] * s_re - a_bl[half + c] * s_im + z_re,
                        a_bl[c] * s_im + a_bl[half + c] * s_re + z_im))
        return tuple(n[0] for n in new) + tuple(n[1] for n in new)

    zero = jnp.zeros((n_seq, LANES), F32)
    fin = lax.fori_loop(0, n_chunk, step, (zero,) * nlb)
    sre_ref[...] = jnp.concatenate(fin[:half], axis=1)
    sim_ref[...] = jnp.concatenate(fin[half:], axis=1)

    for r in range(n_rows // SSM_RT):
        rs = slice(r * SSM_RT, (r + 1) * SSM_RT)
        sp = jnp.concatenate([zs[c, rs, :] for c in range(nlb)], axis=1).astype(BF16)
        y = jnp.dot(uc_ref[rs, :], w_ref[...], preferred_element_type=F32)
        y = y + lax.dot_general(sp, cpt_ref[...], (((1,), (1,)), ((), ())),
                                preferred_element_type=F32)
        yc_ref[rs, :] = y


def _ssm_core(uc, w, win, cpt, a8, n_seq):
    n_rows = uc.shape[0]
    n_chunk = n_rows // n_seq
    mat = lambda r: pl.BlockSpec((None, r, GB_ST2), lambda g: (g, 0, 0))
    blk = pl.BlockSpec((n_rows, GB_IN), lambda g: (0, g))
    st = pl.BlockSpec((n_seq, GB_ST), lambda g: (0, g))
    return pl.pallas_call(
        functools.partial(_ssm_core_kernel, n_seq=n_seq, n_chunk=n_chunk),
        grid=(N_GB,),
        in_specs=[blk, mat(GB_IN), mat(GB_IN), mat(GB_IN), mat(1)],
        out_specs=[blk, st, st],
        out_shape=[
            jax.ShapeDtypeStruct((n_rows, N_GB * GB_IN), F32),
            jax.ShapeDtypeStruct((n_seq, N_GROUPS * SSM_STATE), F32),
            jax.ShapeDtypeStruct((n_seq, N_GROUPS * SSM_STATE), F32),
        ],
        scratch_shapes=[pltpu.VMEM((GB_ST2 // LANES, n_rows, LANES), F32)],
        compiler_params=_cparams(("arbitrary",)),
        name="ssm_core",
    )(uc, w, win, cpt, a8)


def _glu_tail(x, y, g_ref, dsk_ref, wglu_ref):
    h = _rms(x, g_ref[...])
    z = jax.nn.gelu(y + dsk_ref[...] * h).astype(BF16)
    zz = jnp.dot(z, wglu_ref[...], preferred_element_type=F32)
    return x + zz[:, :D_MODEL] * jax.nn.sigmoid(zz[:, D_MODEL:])


def _ssm_post_kernel(x_ref, yc_ref, g_ref, dsk_ref, wglu_ref, xo_ref, ys):
    for gb in range(N_GB):
        for t in range(CHUNK):
            ys[gb, pl.ds(t, SSM_TC, stride=CHUNK), :] = (
                yc_ref[:, gb * GB_IN + t * LANES:gb * GB_IN + (t + 1) * LANES])
    y = jnp.concatenate([ys[gb] for gb in range(N_GB)], axis=1)
    xo_ref[...] = _glu_tail(x_ref[...], y, g_ref, dsk_ref, wglu_ref)


def _ssm_post(x, yc, gain, dsk, wglu):
    b, l, d = x.shape
    nt = l // SSM_TM
    tok = pl.BlockSpec((None, SSM_TM, d), lambda bi, i: (bi, i, 0))
    return pl.pallas_call(
        _ssm_post_kernel,
        grid=(b, nt),
        in_specs=[tok, pl.BlockSpec((SSM_TC, CHUNK * d), lambda bi, i: (bi * nt + i, 0)),
                  _const_spec((1, d)), _const_spec((1, d)), _const_spec((d, 2 * d))],
        out_specs=tok,
        out_shape=jax.ShapeDtypeStruct((b, l, d), F32),
        scratch_shapes=[pltpu.VMEM((N_GB, SSM_TM, LANES), F32)],
        compiler_params=_cparams(("arbitrary", "arbitrary")),
        name="ssm_post",
    )(x, yc, gain, dsk, wglu)


def _ssm_sample_kernel(x_ref, g_ref, bb_ref, c0t_ref, a1_ref, hre_ref, him_ref,
                       y_ref, sre_ref, sim_ref):
    gb = pl.program_id(0)
    x = x_ref[...]
    ms = jnp.mean(x * x, axis=-1, keepdims=True)
    cols = pl.ds(pl.multiple_of(gb * LANES, LANES), LANES)
    u = (x_ref[:, cols] * lax.rsqrt(ms + RMS_EPS) * g_ref[:, cols]).astype(BF16)
    bu = jnp.dot(u, bb_ref[...], preferred_element_type=F32)
    a1 = a1_ref[...]
    a_re = a1[:, :GB_ST]
    a_im = a1[:, GB_ST:]
    h_re = hre_ref[...]
    h_im = him_ref[...]
    s_re = a_re * h_re - a_im * h_im + bu[:, :GB_ST]
    s_im = a_re * h_im + a_im * h_re + bu[:, GB_ST:]
    sre_ref[...] = s_re
    sim_ref[...] = s_im
    s = jnp.concatenate([s_re, s_im], axis=1).astype(BF16)
    y_ref[...] = lax.dot_general(s, c0t_ref[...], (((1,), (1,)), ((), ())),
                                 preferred_element_type=F32)


def _ssm_sample(x, gain, win, c0t, a1, h_re, h_im):
    n, d = x.shape
    st = pl.BlockSpec((n, GB_ST), lambda g: (0, g))
    return pl.pallas_call(
        _ssm_sample_kernel,
        grid=(N_GB,),
        in_specs=[
            _const_spec((n, d)),
            _const_spec((1, d)),
            pl.BlockSpec((None, LANES, GB_ST2), lambda g: (g, CHUNK - 1, 0)),
            pl.BlockSpec((None, LANES, GB_ST2), lambda g: (g, 0, 0)),
            pl.BlockSpec((None, 1, GB_ST2), lambda g: (g, 0, 0)),
            st, st,
        ],
        out_specs=[pl.BlockSpec((n, LANES), lambda g: (0, g)), st, st],
        out_shape=[
            jax.ShapeDtypeStruct((n, d), F32),
            jax.ShapeDtypeStruct((n, N_GROUPS * SSM_STATE), F32),
            jax.ShapeDtypeStruct((n, N_GROUPS * SSM_STATE), F32),
        ],
        compiler_params=_cparams(("arbitrary",)),
        name="ssm_sample",
    )(x, gain, win, c0t, a1, h_re, h_im)


def _glu_sample_kernel(x_ref, y_ref, g_ref, dsk_ref, wglu_ref, xo_ref):
    xo_ref[...] = _glu_tail(x_ref[...], y_ref[...], g_ref, dsk_ref, wglu_ref)


def _glu_sample(x, y, gain, dsk, wglu):
    n, d = x.shape
    return pl.pallas_call(
        _glu_sample_kernel,
        grid=(1,),
        in_specs=[_const_spec((n, d)), _const_spec((n, d)), _const_spec((1, d)),
                  _const_spec((1, d)), _const_spec((d, 2 * d))],
        out_specs=_const_spec((n, d)),
        out_shape=jax.ShapeDtypeStruct((n, d), F32),
        compiler_params=_cparams(("arbitrary",)),
        name="glu_sample",
    )(x, y, gain, dsk, wglu)


def kernel(x_prompt, x_sample, cache_k, cache_v, state_ssm_re, state_ssm_im, state_conv,
           norm_mix, norm_ffn, w_qkv, w_o, q_norm, k_norm, sinks, ssm_a_re, ssm_a_im,
           ssm_log_dt, ssm_b_re, ssm_b_im, ssm_c_re, ssm_c_im, ssm_d, w_glu, w_up, conv_w,
           conv_b, w_down):
    depth = norm_mix.shape[0]
    nb, _, _ = x_prompt.shape
    ns = x_sample.shape[0]
    xp = x_prompt
    xs = x_sample.reshape(ns, D_MODEL)
    row = lambda t: t.reshape(1, -1)

    kps, vps, kss, vss = [], [], [], []
    srp, sip, srs, sis = [], [], [], []
    cps, css = [], []
    for i in range(depth):
        j = i // 2
        gm = row(norm_mix[i])
        if i % 2 == 0:
            wqkv = w_qkv[j].astype(BF16)
            wo = w_o[j].astype(BF16)
            qg, kg = row(q_norm[j]), row(k_norm[j])
            xp, kp, vp = _attn_prompt(xp, gm, wqkv, wo.T, jnp.tile(qg, (1, N_HEADS)),
                                      jnp.tile(kg, (1, N_KV)), sinks[j])
            wq_g = wqkv[:, :D_MODEL].reshape(D_MODEL, N_KV, GQA, HEAD_DIM)
            wq_g = jnp.transpose(wq_g, (2, 0, 1, 3)).reshape(GQA, D_MODEL, KV_DIM)
            wo_g = wo.reshape(N_KV, GQA, HEAD_DIM, D_MODEL)
            wo_g = jnp.transpose(wo_g, (1, 0, 2, 3)).reshape(GQA, KV_DIM, D_MODEL)
            ck = cache_k[j].reshape(ns, WINDOW, KV_DIM)
            cv = cache_v[j].reshape(ns, WINDOW, KV_DIM)
            xs, ks, vs = _attn_sample(xs, gm, wq_g, wqkv[:, D_MODEL:], wo_g, qg, kg, sinks[j],
                                      ck, cv)
            kps.append(kp.reshape(nb, WINDOW, N_KV, HEAD_DIM))
            vps.append(vp.reshape(nb, WINDOW, N_KV, HEAD_DIM))
            kss.append(ks.reshape(ns, WINDOW, N_KV, HEAD_DIM))
            vss.append(vs.reshape(ns, WINDOW, N_KV, HEAD_DIM))
        else:
            w, win, cpt, c0t, a8, a1 = _ssm_prep(
                ssm_a_re[j], ssm_a_im[j], ssm_log_dt[j], ssm_b_re[j], ssm_b_im[j],
                ssm_c_re[j], ssm_c_im[j])
            wglu = w_glu[j].astype(BF16)
            dsk = row(ssm_d[j])
            uc = _ssm_pre(xp, gm)
            yc, s_re, s_im = _ssm_core(uc, w, win, cpt, a8, nb)
            xp = _ssm_post(xp, yc, gm, dsk, wglu)
            srp.append(s_re.reshape(nb, N_GROUPS, SSM_STATE))
            sip.append(s_im.reshape(nb, N_GROUPS, SSM_STATE))
            ys, t_re, t_im = _ssm_sample(
                xs, gm, win, c0t, a1,
                state_ssm_re[j].reshape(ns, N_GROUPS * SSM_STATE),
                state_ssm_im[j].reshape(ns, N_GROUPS * SSM_STATE))
            xs = _glu_sample(xs, ys, gm, dsk, wglu)
            srs.append(t_re.reshape(ns, N_GROUPS, SSM_STATE))
            sis.append(t_im.reshape(ns, N_GROUPS, SSM_STATE))

        gf = row(norm_ffn[i])
        wup = w_up[i].astype(BF16)
        wdn = w_down[i].astype(BF16)
        cb = row(conv_b[i])
        xp, cp = _ffn_prompt(xp, gf, wup, conv_w[i], cb, wdn)
        buf = state_conv[i].reshape(ns, 2 * F2)
        xs, up_new = _ffn_sample(xs, gf, wup, conv_w[i], cb, wdn, buf)
        cps.append(cp)
        css.append(jnp.stack([state_conv[i][:, 1, :], up_new], axis=1))

    return (xp, xs.reshape(ns, 1, D_MODEL),
            jnp.stack(kps), jnp.stack(vps), jnp.stack(kss), jnp.stack(vss),
            jnp.stack(srp), jnp.stack(sip), jnp.stack(srs), jnp.stack(sis),
            jnp.stack(cps), jnp.stack(css))
```

```python
import functools

import jax
import jax.numpy as jnp
from jax import lax
from jax.experimental import pallas as pl
from jax.experimental.pallas import tpu as pltpu

F32 = jnp.float32
BF16 = jnp.bfloat16

D_MODEL = 1024
HEAD_DIM = 64
N_HEADS = 16
N_KV = 4
GQA = 4
WINDOW = 128
KV_DIM = N_KV * HEAD_DIM
QKV_DIM = D_MODEL + 2 * KV_DIM
D_FF = 2816
F2 = 2 * D_FF
FF_BLK = 256
N_FF_BLK = D_FF // FF_BLK
SSM_GROUP = 16
N_GROUPS = 64
SSM_STATE = 64
RMS_EPS = 1e-6

LANES = 128
CHUNK = 8
GB_GROUPS = LANES // SSM_GROUP
N_GB = N_GROUPS // GB_GROUPS
GB_IN = CHUNK * LANES
GB_ST = GB_GROUPS * SSM_STATE
GB_ST2 = 2 * GB_ST

VMEM_LIMIT = 56 * 1024 * 1024


def _cparams(sem):
    return pltpu.CompilerParams(dimension_semantics=sem, vmem_limit_bytes=VMEM_LIMIT)


def _rms(x, g):
    ms = jnp.mean(x * x, axis=-1, keepdims=True)
    return x * lax.rsqrt(ms + RMS_EPS) * g


def _head_rms(t, n, g):
    return jnp.concatenate(
        [_rms(t[:, j * HEAD_DIM:(j + 1) * HEAD_DIM], g) for j in range(n)], axis=1)


def _const_spec(shape):
    nd = len(shape)
    return pl.BlockSpec(shape, lambda *_: (0,) * nd)


ATT_TQ = 512
ATT_NBLK = ATT_TQ // WINDOW
SEG_W = 256


def _seg_mean_sq(t, seg):
    sq = t * t
    hi = sq.astype(BF16)
    lo = (sq - hi.astype(F32)).astype(BF16)
    out = []
    for c in range(t.shape[1] // SEG_W):
        sl = slice(c * SEG_W, (c + 1) * SEG_W)
        out.append(jnp.dot(hi[:, sl], seg, preferred_element_type=F32)
                   + jnp.dot(lo[:, sl], seg, preferred_element_type=F32))
    return jnp.concatenate(out, axis=1) * (1.0 / HEAD_DIM)


VT_ROWS = HEAD_DIM + 16


def _attn_prompt_kernel(sink_ref, x_ref, g_ref, wqkv_ref, wot_ref, qg_ref, kg_ref,
                        xo_ref, kl_ref, vl_ref, kbuf, krol, vt, lo_scr, ot):
    i = pl.program_id(1)
    keys = 2 * WINDOW

    @pl.when(i == 0)
    def _():
        kbuf[0:WINDOW, :] = jnp.zeros((WINDOW, KV_DIM), BF16)
        krol[0:WINDOW, :] = jnp.zeros((WINDOW, KV_DIM), BF16)
        vt[:, 0:HEAD_DIM, 0:WINDOW] = jnp.zeros((N_KV, HEAD_DIM, WINDOW), BF16)
        vt[:, HEAD_DIM:, :] = jnp.ones((N_KV, VT_ROWS - HEAD_DIM, WINDOW + ATT_TQ), BF16)
        lo_scr[...] = jnp.full(lo_scr.shape, WINDOW, jnp.int32)

    @pl.when(i == 1)
    def _():
        lo_scr[...] = jnp.zeros(lo_scr.shape, jnp.int32)

    si = lax.broadcasted_iota(jnp.int32, (SEG_W, SEG_W), 0) // HEAD_DIM
    sj = lax.broadcasted_iota(jnp.int32, (SEG_W, SEG_W), 1) // HEAD_DIM
    seg = jnp.where(si == sj, 1.0, 0.0).astype(BF16)

    x = x_ref[...]
    h = _rms(x, g_ref[...]).astype(BF16)
    qkv = jnp.dot(h, wqkv_ref[...], preferred_element_type=F32)
    q = qkv[:, :D_MODEL]
    k = qkv[:, D_MODEL:D_MODEL + KV_DIM]
    v = qkv[:, D_MODEL + KV_DIM:]
    qn = q * lax.rsqrt(_seg_mean_sq(q, seg) + RMS_EPS) * qg_ref[...] * (HEAD_DIM ** -0.5)
    kn = k * lax.rsqrt(_seg_mean_sq(k, seg) + RMS_EPS) * kg_ref[...]
    kl_ref[...] = kn[ATT_TQ - WINDOW:]
    vl_ref[...] = v[ATT_TQ - WINDOW:]
    kbuf[WINDOW:, :] = kn.astype(BF16)
    for c in range(KV_DIM // LANES):
        sl = slice(c * LANES, (c + 1) * LANES)
        krol[WINDOW:, sl] = pltpu.roll(kn[:, sl], HEAD_DIM, 1).astype(BF16)
    v_t = v.T
    for kvh in range(N_KV):
        vt[kvh, 0:HEAD_DIM, WINDOW:] = v_t[kvh * HEAD_DIM:(kvh + 1) * HEAD_DIM].astype(BF16)

    kc = lax.broadcasted_iota(jnp.int32, (keys, 2 * WINDOW), 0)
    qi = lax.broadcasted_iota(jnp.int32, (keys, 2 * WINDOW), 1) % WINDOW
    band = (kc > qi) & (kc <= qi + WINDOW)
    low = lax.broadcasted_iota(jnp.int32, (WINDOW, LANES), 1) < HEAD_DIM
    nt = (((1,), (1,)), ((), ()))

    for blk in range(ATT_NBLK):
        r0 = blk * WINDOW
        valid = band & (kc >= lo_scr[...]) if blk == 0 else band
        combos = []
        for kvh in range(N_KV):
            khalf = kvh % 2
            combos.append((kvh, (khalf, khalf + 2), kbuf))
            combos.append((kvh, (1 - khalf, 3 - khalf), krol))
        scores = []
        for kvh, pair, kref in combos:
            kcol = slice((kvh // 2) * LANES, (kvh // 2 + 1) * LANES)
            qm = []
            for g in pair:
                qcol = slice((kvh * 2 + g // 2) * LANES, (kvh * 2 + g // 2 + 1) * LANES)
                qm.append(jnp.where(low if g % 2 == 0 else ~low, qn[r0:r0 + WINDOW, qcol], 0.0))
            scores.append(lax.dot_general(kref[r0:r0 + keys, kcol],
                                          jnp.concatenate(qm, axis=0).astype(BF16), nt,
                                          preferred_element_type=F32))
        probs = []
        for (kvh, pair, _), s in zip(combos, scores):
            sink_row = jnp.concatenate(
                [jnp.full((1, WINDOW), sink_ref[kvh * GQA + g], F32) for g in pair], axis=1)
            s = jnp.where(valid, s, -jnp.inf)
            m = jnp.maximum(jnp.max(s, axis=0, keepdims=True), sink_row)
            probs.append((jnp.exp(s - m).astype(BF16), jnp.exp(sink_row - m)))
        outs = [jnp.dot(vt[kvh, :, r0:r0 + keys], e, preferred_element_type=F32)
                for (kvh, _, _), (e, _) in zip(combos, probs)]
        for (kvh, pair, _), (_, e_sink), o in zip(combos, probs, outs):
            rden = 1.0 / (o[HEAD_DIM:HEAD_DIM + 1] + e_sink)
            on = (o[:HEAD_DIM] * rden).astype(BF16)
            for n, g in enumerate(pair):
                hd = kvh * GQA + g
                ot[hd * HEAD_DIM:(hd + 1) * HEAD_DIM, r0:r0 + WINDOW] = (
                    on[:, n * WINDOW:(n + 1) * WINDOW])

    kbuf[0:WINDOW, :] = kbuf[ATT_TQ:, :]
    krol[0:WINDOW, :] = krol[ATT_TQ:, :]
    vt[:, 0:HEAD_DIM, 0:WINDOW] = vt[:, 0:HEAD_DIM, ATT_TQ:]
    out_t = jnp.dot(wot_ref[...], ot[...], preferred_element_type=F32)
    xo_ref[...] = x + out_t.T


def _attn_prompt(x, gain, wqkv, wo_t, qg, kg, sinks, layer):
    b, l, d = x.shape
    wsel = lambda bi, i: (layer, 0, 0)
    tok = pl.BlockSpec((None, ATT_TQ, d), lambda bi, i: (bi, i, 0))
    last = pl.BlockSpec((None, WINDOW, KV_DIM), lambda bi, i: (bi, 0, 0))
    kvbuf = pltpu.VMEM((WINDOW + ATT_TQ, KV_DIM), BF16)
    return pl.pallas_call(
        _attn_prompt_kernel,
        grid=(b, l // ATT_TQ),
        in_specs=[
            pl.BlockSpec(memory_space=pltpu.SMEM),
            tok,
            _const_spec((1, d)),
            pl.BlockSpec((None, d, QKV_DIM), wsel),
            pl.BlockSpec((None, d, d), wsel),
            _const_spec((1, d)),
            _const_spec((1, KV_DIM)),
        ],
        out_specs=[tok, last, last],
        out_shape=[
            jax.ShapeDtypeStruct((b, l, d), F32),
            jax.ShapeDtypeStruct((b, WINDOW, KV_DIM), F32),
            jax.ShapeDtypeStruct((b, WINDOW, KV_DIM), F32),
        ],
        scratch_shapes=[
            kvbuf, kvbuf,
            pltpu.VMEM((N_KV, VT_ROWS, WINDOW + ATT_TQ), BF16),
            pltpu.VMEM((2 * WINDOW, 2 * WINDOW), jnp.int32),
            pltpu.VMEM((d, ATT_TQ), BF16),
        ],
        compiler_params=_cparams(("arbitrary", "arbitrary")),
        name="attn_prompt",
    )(sinks, x, gain, wqkv, wo_t, qg, kg)


ATS_TB = 8


def _attn_sample_kernel(sink_ref, x_ref, g_ref, wq_ref, wkv_ref, wo_ref, qg_ref, kg_ref,
                        ck_ref, cv_ref, xo_ref, ko_ref, vo_ref, r_scr):
    x = x_ref[...]
    h = _rms(x, g_ref[...]).astype(BF16)
    kv = jnp.dot(h, wkv_ref[...], preferred_element_type=F32)
    kn = _head_rms(kv[:, :KV_DIM], N_KV, kg_ref[...])
    v = kv[:, KV_DIM:]
    qg = qg_ref[...]
    q_g = [_head_rms(jnp.dot(h, wq_ref[g], preferred_element_type=F32), N_KV, qg)
           * (HEAD_DIM ** -0.5) for g in range(GQA)]

    rows = GQA * N_KV
    rkv = lax.broadcasted_iota(jnp.int32, (rows, KV_DIM), 0) % N_KV
    lkv = lax.broadcasted_iota(jnp.int32, (rows, KV_DIM), 1) // HEAD_DIM
    own = rkv == lkv
    colj = lax.broadcasted_iota(jnp.int32, (rows, WINDOW), 1)
    sink_col = jnp.concatenate(
        [jnp.full((1, 1), sink_ref[(r % N_KV) * GQA + r // N_KV], F32) for r in range(rows)],
        axis=0)

    for n in range(ATS_TB):
        ko_ref[n, 0:WINDOW - 1, :] = ck_ref[n, 1:WINDOW, :]
        ko_ref[n, WINDOW - 1:WINDOW, :] = kn[n:n + 1]
        vo_ref[n, 0:WINDOW - 1, :] = cv_ref[n, 1:WINDOW, :]
        vo_ref[n, WINDOW - 1:WINDOW, :] = v[n:n + 1]
    qbds = []
    for n in range(ATS_TB):
        qbd = jnp.concatenate(
            [jnp.broadcast_to(q_g[g][n:n + 1], (N_KV, KV_DIM)) for g in range(GQA)], axis=0)
        qbds.append(jnp.where(own, qbd, 0.0))
    scores = [lax.dot_general(qbds[n].astype(BF16), ck_ref[n].astype(BF16),
                              (((1,), (1,)), ((), ())), preferred_element_type=F32)
              for n in range(ATS_TB)]
    probs = []
    for n in range(ATS_TB):
        s = jnp.where(colj >= 1, scores[n], -jnp.inf)
        s_new = jnp.sum(qbds[n] * kn[n:n + 1], axis=-1, keepdims=True)
        m = jnp.maximum(jnp.maximum(jnp.max(s, axis=-1, keepdims=True), s_new), sink_col)
        e = jnp.exp(s - m)
        e_new = jnp.exp(s_new - m)
        rden = 1.0 / (jnp.sum(e, axis=-1, keepdims=True) + e_new + jnp.exp(sink_col - m))
        probs.append(((e * rden).astype(BF16), e_new * rden))
    outs = [jnp.dot(probs[n][0], cv_ref[n].astype(BF16), preferred_element_type=F32)
            for n in range(ATS_TB)]
    for n in range(ATS_TB):
        o = jnp.where(own, outs[n] + probs[n][1] * v[n:n + 1], 0.0)
        for g in range(GQA):
            r_scr[g, n:n + 1, :] = jnp.sum(o[g * N_KV:(g + 1) * N_KV], axis=0, keepdims=True)

    out = x
    for g in range(GQA):
        out = out + jnp.dot(r_scr[g].astype(BF16), wo_ref[g], preferred_element_type=F32)
    xo_ref[...] = out


def _attn_sample(x, gain, wq_g, wqkv, wo_g, qg, kg, sinks, cache_k, cache_v, layer):
    n, d = x.shape
    tok = pl.BlockSpec((ATS_TB, d), lambda i: (i, 0))
    cache = pl.BlockSpec((None, ATS_TB, WINDOW, KV_DIM), lambda i: (layer, i, 0, 0))
    in_specs = [
        pl.BlockSpec(memory_space=pltpu.SMEM),
        tok,
        _const_spec((1, d)),
        pl.BlockSpec((None, GQA, d, KV_DIM), lambda i: (layer, 0, 0, 0)),
        pl.BlockSpec((None, d, 2 * KV_DIM), lambda i: (layer, 0, D_MODEL // (2 * KV_DIM))),
        pl.BlockSpec((None, GQA, KV_DIM, d), lambda i: (layer, 0, 0, 0)),
        _const_spec((1, HEAD_DIM)),
        _const_spec((1, HEAD_DIM)),
        cache,
        cache,
    ]
    args = [sinks, x, gain, wq_g, wqkv, wo_g, qg, kg, cache_k, cache_v]
    return pl.pallas_call(
        _attn_sample_kernel,
        grid=(n // ATS_TB,),
        in_specs=in_specs,
        out_specs=[tok, cache, cache],
        out_shape=[
            jax.ShapeDtypeStruct((n, d), F32),
            jax.ShapeDtypeStruct(cache_k.shape, F32),
            jax.ShapeDtypeStruct(cache_v.shape, F32),
        ],
        scratch_shapes=[pltpu.VMEM((GQA, ATS_TB, KV_DIM), F32)],
        input_output_aliases={len(args) - 2: 1, len(args) - 1: 2},
        compiler_params=_cparams(("arbitrary",)),
        name="attn_sample",
    )(*args)


FFN_TM = 512
CARRY = 8


def _ffn_prompt_kernel(x_ref, g_ref, wup_ref, cw_ref, cb_ref, wdn_ref,
                       xo_ref, cs_ref, carry, upbuf, act, hs):
    i = pl.program_id(1)

    @pl.when(i == 0)
    def _():
        carry[...] = jnp.zeros_like(carry)

    hs[...] = _rms(x_ref[...], g_ref[...]).astype(BF16)

    def up_into(j, slot):
        for half, c0 in enumerate((j * FF_BLK, D_FF + j * FF_BLK)):
            cols = pl.ds(c0, FF_BLK)
            buf = upbuf.at[slot, half]
            buf[0:CARRY, :] = carry[:, cols]
            buf[CARRY:, :] = jnp.dot(hs[...], wup_ref[:, cols], preferred_element_type=F32)
            carry[:, cols] = buf[FFN_TM:, :]

    def conv(slot, half, c0):
        cols = pl.ds(c0, FF_BLK)
        buf = upbuf.at[slot, half]
        c = cb_ref[:, cols]
        c = c + cw_ref[0:1, cols] * buf[CARRY - 2:CARRY - 2 + FFN_TM, :]
        c = c + cw_ref[1:2, cols] * buf[CARRY - 1:CARRY - 1 + FFN_TM, :]
        c = c + cw_ref[2:3, cols] * buf[CARRY:, :]
        return c

    for j in range(N_FF_BLK):
        up_into(j, j % 2)
        cg = conv(j % 2, 0, j * FF_BLK)
        cv = conv(j % 2, 1, D_FF + j * FF_BLK)
        act[:, j * FF_BLK:(j + 1) * FF_BLK] = (jax.nn.silu(cg) * cv).astype(BF16)

    out = jnp.dot(act[...], wdn_ref[...], preferred_element_type=F32)
    xo_ref[...] = x_ref[...] + out
    cs_ref[...] = carry[CARRY - 2:CARRY, :]


def _ffn_prompt(x, gain, wup, cw, cb, wdn, layer):
    b, l, d = x.shape
    tok = pl.BlockSpec((None, FFN_TM, d), lambda bi, i: (bi, i, 0))
    wsel = lambda bi, i: (layer, 0, 0)
    return pl.pallas_call(
        _ffn_prompt_kernel,
        grid=(b, l // FFN_TM),
        in_specs=[
            tok,
            _const_spec((1, d)),
            pl.BlockSpec((None, d, F2), wsel, pipeline_mode=pl.Buffered(1)),
            pl.BlockSpec((None, 3, F2), wsel),
            pl.BlockSpec((None, 1, F2), wsel),
            pl.BlockSpec((None, D_FF, d), wsel, pipeline_mode=pl.Buffered(1)),
        ],
        out_specs=[tok, pl.BlockSpec((None, 2, F2), lambda bi, i: (bi, 0, 0))],
        out_shape=[
            jax.ShapeDtypeStruct((b, l, d), F32),
            jax.ShapeDtypeStruct((b, 2, F2), F32),
        ],
        scratch_shapes=[
            pltpu.VMEM((CARRY, F2), F32),
            pltpu.VMEM((2, 2, CARRY + FFN_TM, FF_BLK), F32),
            pltpu.VMEM((FFN_TM, D_FF), BF16),
            pltpu.VMEM((FFN_TM, d), BF16),
        ],
        compiler_params=_cparams(("arbitrary", "arbitrary")),
        name="ffn_prompt",
    )(x, gain, wup, cw, cb, wdn)


def _ffn_sample_kernel(x_ref, g_ref, w_ref, cw_ref, cb_ref, wdn_ref, sc_ref,
                       xo_ref, cs_ref, gate):
    j = pl.program_id(0)

    @pl.when(j == 0)
    def _():
        xo_ref[...] = x_ref[...]

    h = _rms(x_ref[...], g_ref[...]).astype(BF16)
    up = jnp.dot(h, w_ref[...], preferred_element_type=F32)
    b0 = sc_ref[:, 0, :]
    b1 = sc_ref[:, 1, :]
    c = cb_ref[...]
    c = c + cw_ref[0:1, :] * b0
    c = c + cw_ref[1:2, :] * b1
    c = c + cw_ref[2:3, :] * up
    cs_ref[...] = jnp.stack([b1, up], axis=1)

    @pl.when(j < N_FF_BLK)
    def _():
        gate[j] = jax.nn.silu(c)

    @pl.when(j >= N_FF_BLK)
    def _():
        a = (gate[j - N_FF_BLK] * c).astype(BF16)
        xo_ref[...] += jnp.dot(a, wdn_ref[...], preferred_element_type=F32)


def _ffn_sample(x, gain, wup, cw, cb, wdn, state, layer):
    n, d = x.shape
    nb = N_FF_BLK
    sblk = pl.BlockSpec((None, n, 2, FF_BLK), lambda j: (layer, 0, 0, j))
    in_specs = [
        _const_spec((n, d)),
        _const_spec((1, d)),
        pl.BlockSpec((None, d, FF_BLK), lambda j: (layer, 0, j)),
        pl.BlockSpec((None, 3, FF_BLK), lambda j: (layer, 0, j)),
        pl.BlockSpec((None, 1, FF_BLK), lambda j: (layer, 0, j)),
        pl.BlockSpec((None, FF_BLK, d), lambda j: (layer, jnp.maximum(j - nb, 0), 0)),
        sblk,
    ]
    args = [x, gain, wup, cw, cb, wdn, state]
    return pl.pallas_call(
        _ffn_sample_kernel,
        grid=(2 * nb,),
        in_specs=in_specs,
        out_specs=[_const_spec((n, d)), sblk],
        out_shape=[
            jax.ShapeDtypeStruct((n, d), F32),
            jax.ShapeDtypeStruct(state.shape, F32),
        ],
        scratch_shapes=[pltpu.VMEM((nb, n, FF_BLK), F32)],
        input_output_aliases={len(args) - 1: 1},
        compiler_params=_cparams(("arbitrary",)),
        name="ffn_sample",
    )(*args)


def _tile_states(t):
    t2 = jnp.concatenate([t, t], axis=1)
    return jnp.concatenate([t2] * (GB_GROUPS // 2), axis=1)


def _ssm_prep_kernel(are_ref, aim_ref, dt_ref, bre_ref, bim_ref, cre_ref, cim_ref,
                     aflat_ref, iflat_ref, dtflat_ref,
                     w_ref, win_ref, cpt_ref, c0t_ref, a8_ref, a1_ref):
    a_re = are_ref[...]
    a_im = aim_ref[...]
    dt = jnp.exp(dt_ref[...])

    def power(d):
        mag = jnp.exp(a_re * dt * d)
        return mag * jnp.cos(a_im * dt * d), mag * jnp.sin(a_im * dt * d)

    ab_re, ab_im = power(1.0)
    den = a_re * a_re + a_im * a_im
    z_re = ((ab_re - 1.0) * a_re + ab_im * a_im) / den
    z_im = (ab_im * a_re - (ab_re - 1.0) * a_im) / den
    b_re = bre_ref[...]
    b_im = bim_ref[...]
    bb_re = z_re * b_re - z_im * b_im
    bb_im = z_re * b_im + z_im * b_re
    c_re = cre_ref[...]
    c_im = cim_ref[...]

    rg = lax.broadcasted_iota(jnp.int32, (LANES, GB_ST2), 0) // SSM_GROUP
    lg = (lax.broadcasted_iota(jnp.int32, (LANES, GB_ST2), 1) % GB_ST) // SSM_STATE
    own = rg == lg

    def blockdiag(t_re, t_im):
        full = jnp.concatenate([_tile_states(t_re), _tile_states(t_im)], axis=1)
        return jnp.where(own, full, 0.0)

    c0t = blockdiag(c_re, -c_im)
    c0t_ref[...] = c0t.astype(BF16)

    lfull = []
    for d in range(CHUNK):
        if d == 0:
            l_re, l_im = bb_re, bb_im
        else:
            e_re, e_im = power(float(d))
            l_re = e_re * bb_re - e_im * bb_im
            l_im = e_re * bb_im + e_im * bb_re
        lfull.append(blockdiag(l_re, l_im))
    kd = [lax.dot_general(l, c0t, (((1,), (1,)), ((), ())),
                          precision=lax.Precision.HIGHEST,
                          preferred_element_type=F32) for l in lfull]
    zero = jnp.zeros((LANES, LANES), F32)
    for s in range(CHUNK):
        row = jnp.concatenate([kd[t - s] if t >= s else zero for t in range(CHUNK)], axis=1)
        w_ref[s * LANES:(s + 1) * LANES, :] = row.astype(BF16)
        win_ref[s * LANES:(s + 1) * LANES, :] = lfull[CHUNK - 1 - s].astype(BF16)
    for t in range(CHUNK):
        e_re, e_im = power(float(t + 1))
        cpt_ref[t * LANES:(t + 1) * LANES, :] = blockdiag(
            c_re * e_re - c_im * e_im, -c_re * e_im - c_im * e_re).astype(BF16)

    af = aflat_ref[...]
    ai = iflat_ref[...]
    dtf = jnp.exp(dtflat_ref[...])
    for ref, d in ((a8_ref, float(CHUNK)), (a1_ref, 1.0)):
        mag = jnp.exp(af * dtf * d)
        ref[...] = jnp.concatenate([mag * jnp.cos(ai * dtf * d), mag * jnp.sin(ai * dtf * d)],
                                   axis=1)


def _ssm_prep(a_re, a_im, log_dt, b_re, b_im, c_re, c_im):
    rep = lambda t: jnp.repeat(t, SSM_GROUP, axis=0)
    are_x = rep(a_re)
    aim_x = rep(a_im)
    dt_x = rep(log_dt[:, None])
    bt_re = jnp.transpose(b_re, (0, 2, 1)).reshape(N_GROUPS * SSM_GROUP, SSM_STATE)
    bt_im = jnp.transpose(b_im, (0, 2, 1)).reshape(N_GROUPS * SSM_GROUP, SSM_STATE)
    cr = c_re.reshape(N_GROUPS * SSM_GROUP, SSM_STATE)
    ci = c_im.reshape(N_GROUPS * SSM_GROUP, SSM_STATE)
    aflat = a_re.reshape(N_GB, 1, GB_ST)
    iflat = a_im.reshape(N_GB, 1, GB_ST)
    dtflat = jnp.repeat(log_dt, SSM_STATE).reshape(N_GB, 1, GB_ST)
    rows = pl.BlockSpec((LANES, SSM_STATE), lambda g: (g, 0))
    flat = pl.BlockSpec((None, 1, GB_ST), lambda g: (g, 0, 0))
    big = pl.BlockSpec((None, GB_IN, GB_ST2), lambda g: (g, 0, 0))
    vec = pl.BlockSpec((None, 1, GB_ST2), lambda g: (g, 0, 0))
    return pl.pallas_call(
        _ssm_prep_kernel,
        grid=(N_GB,),
        in_specs=[rows, rows, pl.BlockSpec((LANES, 1), lambda g: (g, 0)),
                  rows, rows, rows, rows, flat, flat, flat],
        out_specs=[big, big, big,
                   pl.BlockSpec((None, LANES, GB_ST2), lambda g: (g, 0, 0)), vec, vec],
        out_shape=[
            jax.ShapeDtypeStruct((N_GB, GB_IN, GB_IN), BF16),
            jax.ShapeDtypeStruct((N_GB, GB_IN, GB_ST2), BF16),
            jax.ShapeDtypeStruct((N_GB, GB_IN, GB_ST2), BF16),
            jax.ShapeDtypeStruct((N_GB, LANES, GB_ST2), BF16),
            jax.ShapeDtypeStruct((N_GB, 1, GB_ST2), F32),
            jax.ShapeDtypeStruct((N_GB, 1, GB_ST2), F32),
        ],
        compiler_params=_cparams(("arbitrary",)),
        name="ssm_prep",
    )(are_x, aim_x, dt_x, bt_re, bt_im, cr, ci, aflat, iflat, dtflat)


SSM_TM = 512
SSM_TC = SSM_TM // CHUNK


def _ssm_pre_kernel(x_ref, g_ref, uc_ref, hs):
    h = _rms(x_ref[...], g_ref[...])
    for gb in range(N_GB):
        hs[gb] = h[:, gb * LANES:(gb + 1) * LANES]
    for gb in range(N_GB):
        for t in range(CHUNK):
            uc_ref[:, gb * GB_IN + t * LANES:gb * GB_IN + (t + 1) * LANES] = (
                hs[gb, pl.ds(t, SSM_TC, stride=CHUNK), :].astype(BF16))


def _ssm_pre(x, gain):
    b, l, d = x.shape
    nt = l // SSM_TM
    return pl.pallas_call(
        _ssm_pre_kernel,
        grid=(b, nt),
        in_specs=[pl.BlockSpec((None, SSM_TM, d), lambda bi, i: (bi, i, 0)), _const_spec((1, d))],
        out_specs=pl.BlockSpec((SSM_TC, CHUNK * d), lambda bi, i: (bi * nt + i, 0)),
        out_shape=jax.ShapeDtypeStruct((b * l // CHUNK, CHUNK * d), BF16),
        scratch_shapes=[pltpu.VMEM((N_GB, SSM_TM, LANES), F32)],
        compiler_params=_cparams(("arbitrary", "arbitrary")),
        name="ssm_pre",
    )(x, gain)


SSM_RT = 256
SCAN_UNROLL = 8
SCAN_PAD = 8


def _ssm_core_kernel(uc_ref, w_ref, win_ref, cpt_ref, a8_ref, yc_ref, sre_ref, sim_ref, zs,
                     *, n_seq, n_chunk):
    n_rows = n_seq * n_chunk
    nlb = GB_ST2 // LANES
    half = nlb // 2
    pitch = n_chunk + SCAN_PAD

    def zrows(r):
        seq, off = divmod(r * SSM_RT, n_chunk)
        return slice(seq * pitch + off, seq * pitch + off + SSM_RT)

    for r in range(n_rows // SSM_RT):
        rs = slice(r * SSM_RT, (r + 1) * SSM_RT)
        z = jnp.dot(uc_ref[rs, :], win_ref[...], preferred_element_type=F32)
        for c in range(nlb):
            zs[c, zrows(r), :] = z[:, c * LANES:(c + 1) * LANES]

    a8 = a8_ref[...]
    a_bl = [jnp.broadcast_to(a8[:, c * LANES:(c + 1) * LANES], (n_seq, LANES))
            for c in range(nlb)]

    def step(k, carry):
        rows = pl.ds(k, n_seq, stride=pitch)
        new = []
        for c in range(half):
            s_re, s_im = carry[c], carry[half + c]
            z_re = zs[c, rows, :]
            z_im = zs[half + c, rows, :]
            zs[c, rows, :] = s_re
            zs[half + c, rows, :] = s_im
            new.append((a_bl[c] * s_re - a_bl[half + c] * s_im + z_re,
                        a_bl[c] * s_im + a_bl[half + c] * s_re + z_im))
        return tuple(n[0] for n in new) + tuple(n[1] for n in new)

    zero = jnp.zeros((n_seq, LANES), F32)
    fin = lax.fori_loop(0, n_chunk, step, (zero,) * nlb, unroll=SCAN_UNROLL)
    sre_ref[...] = jnp.concatenate(fin[:half], axis=1)
    sim_ref[...] = jnp.concatenate(fin[half:], axis=1)

    for r in range(n_rows // SSM_RT):
        rs = slice(r * SSM_RT, (r + 1) * SSM_RT)
        sp = jnp.concatenate([zs[c, zrows(r), :] for c in range(nlb)], axis=1).astype(BF16)
        y = lax.dot_general(sp, cpt_ref[...], (((1,), (1,)), ((), ())),
                            preferred_element_type=F32)
        for m in range(CHUNK // 2):
            kin = (2 * m + 2) * LANES
            oc = slice(2 * m * LANES, (2 * m + 2) * LANES)
            yc_ref[rs, oc] = y[:, oc] + jnp.dot(uc_ref[rs, :kin], w_ref[:kin, oc],
                                               preferred_element_type=F32)


def _ssm_core(uc, w, win, cpt, a8, n_seq):
    n_rows = uc.shape[0]
    n_chunk = n_rows // n_seq
    mat = lambda r: pl.BlockSpec((None, r, GB_ST2), lambda g: (g, 0, 0))
    blk = pl.BlockSpec((n_rows, GB_IN), lambda g: (0, g))
    st = pl.BlockSpec((n_seq, GB_ST), lambda g: (0, g))
    return pl.pallas_call(
        functools.partial(_ssm_core_kernel, n_seq=n_seq, n_chunk=n_chunk),
        grid=(N_GB,),
        in_specs=[blk, mat(GB_IN), mat(GB_IN), mat(GB_IN), mat(1)],
        out_specs=[blk, st, st],
        out_shape=[
            jax.ShapeDtypeStruct((n_rows, N_GB * GB_IN), F32),
            jax.ShapeDtypeStruct((n_seq, N_GROUPS * SSM_STATE), F32),
            jax.ShapeDtypeStruct((n_seq, N_GROUPS * SSM_STATE), F32),
        ],
        scratch_shapes=[pltpu.VMEM((GB_ST2 // LANES, n_seq * (n_chunk + SCAN_PAD), LANES), F32)],
        compiler_params=_cparams(("arbitrary",)),
        name="ssm_core",
    )(uc, w, win, cpt, a8)


def _glu_tail(x, y, g_ref, dsk_ref, wglu_ref):
    h = _rms(x, g_ref[...])
    z = jax.nn.gelu(y + dsk_ref[...] * h).astype(BF16)
    zz = jnp.dot(z, wglu_ref[...], preferred_element_type=F32)
    return x + zz[:, :D_MODEL] * jax.nn.sigmoid(zz[:, D_MODEL:])


def _ssm_post_kernel(x_ref, yc_ref, g_ref, dsk_ref, wglu_ref, xo_ref, ys):
    for gb in range(N_GB):
        for t in range(CHUNK):
            ys[gb, pl.ds(t, SSM_TC, stride=CHUNK), :] = (
                yc_ref[:, gb * GB_IN + t * LANES:gb * GB_IN + (t + 1) * LANES])
    y = jnp.concatenate([ys[gb] for gb in range(N_GB)], axis=1)
    xo_ref[...] = _glu_tail(x_ref[...], y, g_ref, dsk_ref, wglu_ref)


def _ssm_post(x, yc, gain, dsk, wglu, layer):
    b, l, d = x.shape
    nt = l // SSM_TM
    tok = pl.BlockSpec((None, SSM_TM, d), lambda bi, i: (bi, i, 0))
    return pl.pallas_call(
        _ssm_post_kernel,
        grid=(b, nt),
        in_specs=[tok, pl.BlockSpec((SSM_TC, CHUNK * d), lambda bi, i: (bi * nt + i, 0)),
                  _const_spec((1, d)), _const_spec((1, d)),
                  pl.BlockSpec((None, d, 2 * d), lambda bi, i: (layer, 0, 0))],
        out_specs=tok,
        out_shape=jax.ShapeDtypeStruct((b, l, d), F32),
        scratch_shapes=[pltpu.VMEM((N_GB, SSM_TM, LANES), F32)],
        compiler_params=_cparams(("arbitrary", "arbitrary")),
        name="ssm_post",
    )(x, yc, gain, dsk, wglu)


def _ssm_sample_kernel(x_ref, g_ref, bb_ref, c0t_ref, a1_ref, hre_ref, him_ref,
                       y_ref, sre_ref, sim_ref):
    gb = pl.program_id(0)
    x = x_ref[...]
    ms = jnp.mean(x * x, axis=-1, keepdims=True)
    cols = pl.ds(pl.multiple_of(gb * LANES, LANES), LANES)
    u = (x_ref[:, cols] * lax.rsqrt(ms + RMS_EPS) * g_ref[:, cols]).astype(BF16)
    bu = jnp.dot(u, bb_ref[...], preferred_element_type=F32)
    a1 = a1_ref[...]
    a_re = a1[:, :GB_ST]
    a_im = a1[:, GB_ST:]
    h_re = hre_ref[...]
    h_im = him_ref[...]
    s_re = a_re * h_re - a_im * h_im + bu[:, :GB_ST]
    s_im = a_re * h_im + a_im * h_re + bu[:, GB_ST:]
    sre_ref[...] = s_re
    sim_ref[...] = s_im
    s = jnp.concatenate([s_re, s_im], axis=1).astype(BF16)
    y_ref[...] = lax.dot_general(s, c0t_ref[...], (((1,), (1,)), ((), ())),
                                 preferred_element_type=F32)


def _ssm_sample(x, gain, win, c0t, a1, h_re, h_im):
    n, d = x.shape
    st = pl.BlockSpec((n, GB_ST), lambda g: (0, g))
    return pl.pallas_call(
        _ssm_sample_kernel,
        grid=(N_GB,),
        in_specs=[
            _const_spec((n, d)),
            _const_spec((1, d)),
            pl.BlockSpec((None, LANES, GB_ST2), lambda g: (g, CHUNK - 1, 0)),
            pl.BlockSpec((None, LANES, GB_ST2), lambda g: (g, 0, 0)),
            pl.BlockSpec((None, 1, GB_ST2), lambda g: (g, 0, 0)),
            st, st,
        ],
        out_specs=[pl.BlockSpec((n, LANES), lambda g: (0, g)), st, st],
        out_shape=[
            jax.ShapeDtypeStruct((n, d), F32),
            jax.ShapeDtypeStruct((n, N_GROUPS * SSM_STATE), F32),
            jax.ShapeDtypeStruct((n, N_GROUPS * SSM_STATE), F32),
        ],
        compiler_params=_cparams(("arbitrary",)),
        name="ssm_sample",
    )(x, gain, win, c0t, a1, h_re, h_im)


def _glu_sample_kernel(x_ref, y_ref, g_ref, dsk_ref, wglu_ref, xo_ref):
    xo_ref[...] = _glu_tail(x_ref[...], y_ref[...], g_ref, dsk_ref, wglu_ref)


def _glu_sample(x, y, gain, dsk, wglu, layer):
    n, d = x.shape
    return pl.pallas_call(
        _glu_sample_kernel,
        grid=(1,),
        in_specs=[_const_spec((n, d)), _const_spec((n, d)), _const_spec((1, d)),
                  _const_spec((1, d)), pl.BlockSpec((None, d, 2 * d), lambda i: (layer, 0, 0))],
        out_specs=_const_spec((n, d)),
        out_shape=jax.ShapeDtypeStruct((n, d), F32),
        compiler_params=_cparams(("arbitrary",)),
        name="glu_sample",
    )(x, y, gain, dsk, wglu)


def kernel(x_prompt, x_sample, cache_k, cache_v, state_ssm_re, state_ssm_im, state_conv,
           norm_mix, norm_ffn, w_qkv, w_o, q_norm, k_norm, sinks, ssm_a_re, ssm_a_im,
           ssm_log_dt, ssm_b_re, ssm_b_im, ssm_c_re, ssm_c_im, ssm_d, w_glu, w_up, conv_w,
           conv_b, w_down):
    depth = norm_mix.shape[0]
    nb, _, _ = x_prompt.shape
    ns = x_sample.shape[0]
    xp = x_prompt
    xs = x_sample.reshape(ns, D_MODEL)
    row = lambda t: t.reshape(1, -1)

    wqkv = w_qkv.astype(BF16)
    wo = w_o.astype(BF16)
    wo_t = jnp.swapaxes(wo, 1, 2)
    wglu = w_glu.astype(BF16)
    wup = w_up.astype(BF16)
    wdn = w_down.astype(BF16)
    cb = conv_b.reshape(depth, 1, F2)
    n_att = wqkv.shape[0]
    wq_g = wqkv[:, :, :D_MODEL].reshape(n_att, D_MODEL, N_KV, GQA, HEAD_DIM)
    wq_g = jnp.transpose(wq_g, (0, 3, 1, 2, 4)).reshape(n_att, GQA, D_MODEL, KV_DIM)
    wo_g = wo.reshape(n_att, N_KV, GQA, HEAD_DIM, D_MODEL)
    wo_g = jnp.transpose(wo_g, (0, 2, 1, 3, 4)).reshape(n_att, GQA, KV_DIM, D_MODEL)
    ck = cache_k.reshape(n_att, ns, WINDOW, KV_DIM)
    cv = cache_v.reshape(n_att, ns, WINDOW, KV_DIM)

    kps, vps = [], []
    srp, sip, srs, sis = [], [], [], []
    cps = []
    conv_s = state_conv
    for i in range(depth):
        j = i // 2
        gm = row(norm_mix[i])
        if i % 2 == 0:
            qg, kg = row(q_norm[j]), row(k_norm[j])
            xp, kp, vp = _attn_prompt(xp, gm, wqkv, wo_t, jnp.tile(qg, (1, N_HEADS)),
                                      jnp.tile(kg, (1, N_KV)), sinks[j], j)
            xs, ck, cv = _attn_sample(xs, gm, wq_g, wqkv, wo_g, qg, kg, sinks[j], ck, cv, j)
            kps.append(kp.reshape(nb, WINDOW, N_KV, HEAD_DIM))
            vps.append(vp.reshape(nb, WINDOW, N_KV, HEAD_DIM))
        else:
            w, win, cpt, c0t, a8, a1 = _ssm_prep(
                ssm_a_re[j], ssm_a_im[j], ssm_log_dt[j], ssm_b_re[j], ssm_b_im[j],
                ssm_c_re[j], ssm_c_im[j])
            dsk = row(ssm_d[j])
            uc = _ssm_pre(xp, gm)
            yc, s_re, s_im = _ssm_core(uc, w, win, cpt, a8, nb)
            xp = _ssm_post(xp, yc, gm, dsk, wglu, j)
            srp.append(s_re.reshape(nb, N_GROUPS, SSM_STATE))
            sip.append(s_im.reshape(nb, N_GROUPS, SSM_STATE))
            ys, t_re, t_im = _ssm_sample(
                xs, gm, win, c0t, a1,
                state_ssm_re[j].reshape(ns, N_GROUPS * SSM_STATE),
                state_ssm_im[j].reshape(ns, N_GROUPS * SSM_STATE))
            xs = _glu_sample(xs, ys, gm, dsk, wglu, j)
            srs.append(t_re.reshape(ns, N_GROUPS, SSM_STATE))
            sis.append(t_im.reshape(ns, N_GROUPS, SSM_STATE))

        gf = row(norm_ffn[i])
        xp, cp = _ffn_prompt(xp, gf, wup, conv_w, cb, wdn, i)
        xs, conv_s = _ffn_sample(xs, gf, wup, conv_w, cb, wdn, conv_s, i)
        cps.append(cp)

    k_s, v_s = (t.reshape(n_att, ns, WINDOW, N_KV, HEAD_DIM) for t in (ck, cv))
    return (xp, xs.reshape(ns, 1, D_MODEL),
            jnp.stack(kps), jnp.stack(vps), k_s, v_s,
            jnp.stack(srp), jnp.stack(sip), jnp.stack(srs), jnp.stack(sis),
            jnp.stack(cps), conv_s)
```

```python
import functools

import jax
import jax.numpy as jnp
from jax import lax
from jax.experimental import pallas as pl
from jax.experimental.pallas import tpu as pltpu

F32 = jnp.float32
BF16 = jnp.bfloat16

D_MODEL = 1024
HEAD_DIM = 64
N_HEADS = 16
N_KV = 4
GQA = 4
WINDOW = 128
KV_DIM = N_KV * HEAD_DIM
QKV_DIM = D_MODEL + 2 * KV_DIM
D_FF = 2816
F2 = 2 * D_FF
FF_BLK = 256
N_FF_BLK = D_FF // FF_BLK
SSM_GROUP = 16
N_GROUPS = 64
SSM_STATE = 64
RMS_EPS = 1e-6

LANES = 128
CHUNK = 8
GB_GROUPS = LANES // SSM_GROUP
N_GB = N_GROUPS // GB_GROUPS
GB_IN = CHUNK * LANES
GB_ST = GB_GROUPS * SSM_STATE
GB_ST2 = 2 * GB_ST

VMEM_LIMIT = 56 * 1024 * 1024


def _cparams(sem):
    return pltpu.CompilerParams(dimension_semantics=sem, vmem_limit_bytes=VMEM_LIMIT)


def _rms(x, g):
    ms = jnp.mean(x * x, axis=-1, keepdims=True)
    return x * lax.rsqrt(ms + RMS_EPS) * g


def _head_rms(t, n, g):
    return jnp.concatenate(
        [_rms(t[:, j * HEAD_DIM:(j + 1) * HEAD_DIM], g) for j in range(n)], axis=1)


def _const_spec(shape):
    nd = len(shape)
    return pl.BlockSpec(shape, lambda *_: (0,) * nd)


ATT_TQ = 512
ATT_NBLK = ATT_TQ // WINDOW
SEG_W = 256


def _seg_mean_sq(t, seg):
    sq = t * t
    hi = sq.astype(BF16)
    lo = (sq - hi.astype(F32)).astype(BF16)
    out = []
    for c in range(t.shape[1] // SEG_W):
        sl = slice(c * SEG_W, (c + 1) * SEG_W)
        out.append(jnp.dot(hi[:, sl], seg, preferred_element_type=F32)
                   + jnp.dot(lo[:, sl], seg, preferred_element_type=F32))
    return jnp.concatenate(out, axis=1) * (1.0 / HEAD_DIM)


VT_ROWS = HEAD_DIM + 16


def _attn_prompt_kernel(sink_ref, x_ref, g_ref, wqkv_ref, wot_ref, qg_ref, kg_ref,
                        xo_ref, kl_ref, vl_ref, kbuf, krol, vt, lo_scr, ot):
    i = pl.program_id(1)
    keys = 2 * WINDOW

    @pl.when(i == 0)
    def _():
        kbuf[0:WINDOW, :] = jnp.zeros((WINDOW, KV_DIM), BF16)
        krol[0:WINDOW, :] = jnp.zeros((WINDOW, KV_DIM), BF16)
        vt[:, 0:HEAD_DIM, 0:WINDOW] = jnp.zeros((N_KV, HEAD_DIM, WINDOW), BF16)
        vt[:, HEAD_DIM:, :] = jnp.ones((N_KV, VT_ROWS - HEAD_DIM, WINDOW + ATT_TQ), BF16)
        lo_scr[...] = jnp.full(lo_scr.shape, WINDOW, jnp.int32)

    @pl.when(i == 1)
    def _():
        lo_scr[...] = jnp.zeros(lo_scr.shape, jnp.int32)

    si = lax.broadcasted_iota(jnp.int32, (SEG_W, SEG_W), 0) // HEAD_DIM
    sj = lax.broadcasted_iota(jnp.int32, (SEG_W, SEG_W), 1) // HEAD_DIM
    seg = jnp.where(si == sj, 1.0, 0.0).astype(BF16)

    x = x_ref[...]
    h = _rms(x, g_ref[...]).astype(BF16)
    qkv = jnp.dot(h, wqkv_ref[...], preferred_element_type=F32)
    q = qkv[:, :D_MODEL]
    k = qkv[:, D_MODEL:D_MODEL + KV_DIM]
    v = qkv[:, D_MODEL + KV_DIM:]
    qn = q * lax.rsqrt(_seg_mean_sq(q, seg) + RMS_EPS) * qg_ref[...] * (HEAD_DIM ** -0.5)
    kn = k * lax.rsqrt(_seg_mean_sq(k, seg) + RMS_EPS) * kg_ref[...]
    kl_ref[...] = kn[ATT_TQ - WINDOW:]
    vl_ref[...] = v[ATT_TQ - WINDOW:]
    kbuf[WINDOW:, :] = kn.astype(BF16)
    for c in range(KV_DIM // LANES):
        sl = slice(c * LANES, (c + 1) * LANES)
        krol[WINDOW:, sl] = pltpu.roll(kn[:, sl], HEAD_DIM, 1).astype(BF16)
    v_t = v.T
    for kvh in range(N_KV):
        vt[kvh, 0:HEAD_DIM, WINDOW:] = v_t[kvh * HEAD_DIM:(kvh + 1) * HEAD_DIM].astype(BF16)

    kc = lax.broadcasted_iota(jnp.int32, (keys, 2 * WINDOW), 0)
    qi = lax.broadcasted_iota(jnp.int32, (keys, 2 * WINDOW), 1) % WINDOW
    band = (kc > qi) & (kc <= qi + WINDOW)
    low = lax.broadcasted_iota(jnp.int32, (WINDOW, LANES), 1) < HEAD_DIM
    nt = (((1,), (1,)), ((), ()))

    for blk in range(ATT_NBLK):
        r0 = blk * WINDOW
        valid = band & (kc >= lo_scr[...]) if blk == 0 else band
        combos = []
        for kvh in range(N_KV):
            khalf = kvh % 2
            combos.append((kvh, (khalf, khalf + 2), kbuf))
            combos.append((kvh, (1 - khalf, 3 - khalf), krol))
        scores = []
        for kvh, pair, kref in combos:
            kcol = slice((kvh // 2) * LANES, (kvh // 2 + 1) * LANES)
            qm = []
            for g in pair:
                qcol = slice((kvh * 2 + g // 2) * LANES, (kvh * 2 + g // 2 + 1) * LANES)
                qm.append(jnp.where(low if g % 2 == 0 else ~low, qn[r0:r0 + WINDOW, qcol], 0.0))
            scores.append(lax.dot_general(kref[r0:r0 + keys, kcol],
                                          jnp.concatenate(qm, axis=0).astype(BF16), nt,
                                          preferred_element_type=F32))
        probs = []
        for (kvh, pair, _), s in zip(combos, scores):
            sink_row = jnp.concatenate(
                [jnp.full((1, WINDOW), sink_ref[kvh * GQA + g], F32) for g in pair], axis=1)
            s = jnp.where(valid, s, -jnp.inf)
            m = jnp.maximum(jnp.max(s, axis=0, keepdims=True), sink_row)
            probs.append((jnp.exp(s - m).astype(BF16), jnp.exp(sink_row - m)))
        outs = [jnp.dot(vt[kvh, :, r0:r0 + keys], e, preferred_element_type=F32)
                for (kvh, _, _), (e, _) in zip(combos, probs)]
        for (kvh, pair, _), (_, e_sink), o in zip(combos, probs, outs):
            rden = 1.0 / (o[HEAD_DIM:HEAD_DIM + 1] + e_sink)
            on = (o[:HEAD_DIM] * rden).astype(BF16)
            for n, g in enumerate(pair):
                hd = kvh * GQA + g
                ot[hd * HEAD_DIM:(hd + 1) * HEAD_DIM, r0:r0 + WINDOW] = (
                    on[:, n * WINDOW:(n + 1) * WINDOW])

    kbuf[0:WINDOW, :] = kbuf[ATT_TQ:, :]
    krol[0:WINDOW, :] = krol[ATT_TQ:, :]
    vt[:, 0:HEAD_DIM, 0:WINDOW] = vt[:, 0:HEAD_DIM, ATT_TQ:]
    out_t = jnp.dot(wot_ref[...], ot[...], preferred_element_type=F32)
    xo_ref[...] = x + out_t.T


def _attn_prompt(x, gain, wqkv, wo_t, qg, kg, sinks, layer):
    b, l, d = x.shape
    wsel = lambda bi, i: (layer, 0, 0)
    tok = pl.BlockSpec((None, ATT_TQ, d), lambda bi, i: (bi, i, 0))
    last = pl.BlockSpec((None, WINDOW, KV_DIM), lambda bi, i: (bi, 0, 0))
    kvbuf = pltpu.VMEM((WINDOW + ATT_TQ, KV_DIM), BF16)
    return pl.pallas_call(
        _attn_prompt_kernel,
        grid=(b, l // ATT_TQ),
        in_specs=[
            pl.BlockSpec(memory_space=pltpu.SMEM),
            tok,
            _const_spec((1, d)),
            pl.BlockSpec((None, d, QKV_DIM), wsel),
            pl.BlockSpec((None, d, d), wsel),
            _const_spec((1, d)),
            _const_spec((1, KV_DIM)),
        ],
        out_specs=[tok, last, last],
        out_shape=[
            jax.ShapeDtypeStruct((b, l, d), F32),
            jax.ShapeDtypeStruct((b, WINDOW, KV_DIM), F32),
            jax.ShapeDtypeStruct((b, WINDOW, KV_DIM), F32),
        ],
        scratch_shapes=[
            kvbuf, kvbuf,
            pltpu.VMEM((N_KV, VT_ROWS, WINDOW + ATT_TQ), BF16),
            pltpu.VMEM((2 * WINDOW, 2 * WINDOW), jnp.int32),
            pltpu.VMEM((d, ATT_TQ), BF16),
        ],
        compiler_params=_cparams(("arbitrary", "arbitrary")),
        name="attn_prompt",
    )(sinks, x, gain, wqkv, wo_t, qg, kg)


ATS_TB = 8


def _attn_sample_kernel(sink_ref, x_ref, g_ref, wq_ref, wkv_ref, wo_ref, qg_ref, kg_ref,
                        ck_ref, cv_ref, xo_ref, ko_ref, vo_ref, r_scr):
    x = x_ref[...]
    h = _rms(x, g_ref[...]).astype(BF16)
    kv = jnp.dot(h, wkv_ref[...], preferred_element_type=F32)
    kn = _head_rms(kv[:, :KV_DIM], N_KV, kg_ref[...])
    v = kv[:, KV_DIM:]
    qg = qg_ref[...]
    q_g = [_head_rms(jnp.dot(h, wq_ref[g], preferred_element_type=F32), N_KV, qg)
           * (HEAD_DIM ** -0.5) for g in range(GQA)]

    rows = GQA * N_KV
    rkv = lax.broadcasted_iota(jnp.int32, (rows, KV_DIM), 0) % N_KV
    lkv = lax.broadcasted_iota(jnp.int32, (rows, KV_DIM), 1) // HEAD_DIM
    own = rkv == lkv
    colj = lax.broadcasted_iota(jnp.int32, (rows, WINDOW), 1)
    sink_col = jnp.concatenate(
        [jnp.full((1, 1), sink_ref[(r % N_KV) * GQA + r // N_KV], F32) for r in range(rows)],
        axis=0)

    for n in range(ATS_TB):
        ko_ref[n, 0:WINDOW - 1, :] = ck_ref[n, 1:WINDOW, :]
        ko_ref[n, WINDOW - 1:WINDOW, :] = kn[n:n + 1]
        vo_ref[n, 0:WINDOW - 1, :] = cv_ref[n, 1:WINDOW, :]
        vo_ref[n, WINDOW - 1:WINDOW, :] = v[n:n + 1]
    qbds = []
    for n in range(ATS_TB):
        qbd = jnp.concatenate(
            [jnp.broadcast_to(q_g[g][n:n + 1], (N_KV, KV_DIM)) for g in range(GQA)], axis=0)
        qbds.append(jnp.where(own, qbd, 0.0))
    scores = [lax.dot_general(qbds[n].astype(BF16), ck_ref[n].astype(BF16),
                              (((1,), (1,)), ((), ())), preferred_element_type=F32)
              for n in range(ATS_TB)]
    probs = []
    for n in range(ATS_TB):
        s = jnp.where(colj >= 1, scores[n], -jnp.inf)
        s_new = jnp.sum(qbds[n] * kn[n:n + 1], axis=-1, keepdims=True)
        m = jnp.maximum(jnp.maximum(jnp.max(s, axis=-1, keepdims=True), s_new), sink_col)
        e = jnp.exp(s - m)
        e_new = jnp.exp(s_new - m)
        rden = 1.0 / (jnp.sum(e, axis=-1, keepdims=True) + e_new + jnp.exp(sink_col - m))
        probs.append(((e * rden).astype(BF16), e_new * rden))
    outs = [jnp.dot(probs[n][0], cv_ref[n].astype(BF16), preferred_element_type=F32)
            for n in range(ATS_TB)]
    for n in range(ATS_TB):
        o = jnp.where(own, outs[n] + probs[n][1] * v[n:n + 1], 0.0)
        for g in range(GQA):
            r_scr[g, n:n + 1, :] = jnp.sum(o[g * N_KV:(g + 1) * N_KV], axis=0, keepdims=True)

    out = x
    for g in range(GQA):
        out = out + jnp.dot(r_scr[g].astype(BF16), wo_ref[g], preferred_element_type=F32)
    xo_ref[...] = out


def _attn_sample(x, gain, wq_g, wqkv, wo_g, qg, kg, sinks, cache_k, cache_v, layer):
    n, d = x.shape
    tok = pl.BlockSpec((ATS_TB, d), lambda i: (i, 0))
    cache = pl.BlockSpec((None, ATS_TB, WINDOW, KV_DIM), lambda i: (layer, i, 0, 0))
    in_specs = [
        pl.BlockSpec(memory_space=pltpu.SMEM),
        tok,
        _const_spec((1, d)),
        pl.BlockSpec((None, GQA, d, KV_DIM), lambda i: (layer, 0, 0, 0)),
        pl.BlockSpec((None, d, 2 * KV_DIM), lambda i: (layer, 0, D_MODEL // (2 * KV_DIM))),
        pl.BlockSpec((None, GQA, KV_DIM, d), lambda i: (layer, 0, 0, 0)),
        _const_spec((1, HEAD_DIM)),
        _const_spec((1, HEAD_DIM)),
        cache,
        cache,
    ]
    args = [sinks, x, gain, wq_g, wqkv, wo_g, qg, kg, cache_k, cache_v]
    return pl.pallas_call(
        _attn_sample_kernel,
        grid=(n // ATS_TB,),
        in_specs=in_specs,
        out_specs=[tok, cache, cache],
        out_shape=[
            jax.ShapeDtypeStruct((n, d), F32),
            jax.ShapeDtypeStruct(cache_k.shape, F32),
            jax.ShapeDtypeStruct(cache_v.shape, F32),
        ],
        scratch_shapes=[pltpu.VMEM((GQA, ATS_TB, KV_DIM), F32)],
        input_output_aliases={len(args) - 2: 1, len(args) - 1: 2},
        compiler_params=_cparams(("arbitrary",)),
        name="attn_sample",
    )(*args)


FFN_TM = 512
CARRY = 8


def _ffn_prompt_kernel(x_ref, g_ref, wup_ref, cw_ref, cb_ref, wdn_ref,
                       xo_ref, cs_ref, carry, upbuf, act, hs, oscr):
    i = pl.program_id(1)
    hm = FFN_TM // 2
    nlb = FF_BLK // LANES

    @pl.when(i == 0)
    def _():
        carry[...] = jnp.zeros_like(carry)

    hs[...] = _rms(x_ref[...], g_ref[...]).astype(BF16)

    def up_into(j, slot):
        for half, c0 in enumerate((j * FF_BLK, D_FF + j * FF_BLK)):
            cols = pl.ds(c0, FF_BLK)
            up = jnp.dot(hs[...], wup_ref[:, cols], preferred_element_type=F32)
            for lb in range(nlb):
                lcols = pl.ds(c0 + lb * LANES, LANES)
                buf = upbuf.at[slot, half, lb]
                buf[0:CARRY, :] = carry[:, lcols]
                buf[CARRY:, :] = up[:, lb * LANES:(lb + 1) * LANES]
                carry[:, lcols] = buf[FFN_TM:, :]

    def conv(slot, half, c0):
        ev, od = [], []
        for lb in range(nlb):
            lcols = pl.ds(c0 + lb * LANES, LANES)
            buf = upbuf.at[slot, half, lb]
            tap = lambda off: buf[pl.ds(CARRY + off, hm, stride=2), :]
            em2, om1, e0, o1 = tap(-2), tap(-1), tap(0), tap(1)
            w0, w1, w2 = cw_ref[0:1, lcols], cw_ref[1:2, lcols], cw_ref[2:3, lcols]
            cb = cb_ref[:, lcols]
            ev.append(cb + w0 * em2 + w1 * om1 + w2 * e0)
            od.append(cb + w0 * om1 + w1 * e0 + w2 * o1)
        return jnp.concatenate(ev, axis=1), jnp.concatenate(od, axis=1)

    for j in range(N_FF_BLK):
        up_into(j, j % 2)
        cg = conv(j % 2, 0, j * FF_BLK)
        cv = conv(j % 2, 1, D_FF + j * FF_BLK)
        for p in range(2):
            act[p * hm:(p + 1) * hm, j * FF_BLK:(j + 1) * FF_BLK] = (
                jax.nn.silu(cg[p]) * cv[p]).astype(BF16)

    out = jnp.dot(act[...], wdn_ref[...], preferred_element_type=F32)
    for lb in range(D_MODEL // LANES):
        for p in range(2):
            oscr[lb, pl.ds(p, hm, stride=2), :] = out[p * hm:(p + 1) * hm,
                                                      lb * LANES:(lb + 1) * LANES]
    xo_ref[...] = x_ref[...] + jnp.concatenate(
        [oscr[lb] for lb in range(D_MODEL // LANES)], axis=1)
    cs_ref[...] = carry[CARRY - 2:CARRY, :]


def _ffn_prompt(x, gain, wup, cw, cb, wdn, layer):
    b, l, d = x.shape
    tok = pl.BlockSpec((None, FFN_TM, d), lambda bi, i: (bi, i, 0))
    wsel = lambda bi, i: (layer, 0, 0)
    return pl.pallas_call(
        _ffn_prompt_kernel,
        grid=(b, l // FFN_TM),
        in_specs=[
            tok,
            _const_spec((1, d)),
            pl.BlockSpec((None, d, F2), wsel, pipeline_mode=pl.Buffered(1)),
            pl.BlockSpec((None, 3, F2), wsel),
            pl.BlockSpec((None, 1, F2), wsel),
            pl.BlockSpec((None, D_FF, d), wsel, pipeline_mode=pl.Buffered(1)),
        ],
        out_specs=[tok, pl.BlockSpec((None, 2, F2), lambda bi, i: (bi, 0, 0))],
        out_shape=[
            jax.ShapeDtypeStruct((b, l, d), F32),
            jax.ShapeDtypeStruct((b, 2, F2), F32),
        ],
        scratch_shapes=[
            pltpu.VMEM((CARRY, F2), F32),
            pltpu.VMEM((2, 2, FF_BLK // LANES, CARRY + FFN_TM, LANES), F32),
            pltpu.VMEM((FFN_TM, D_FF), BF16),
            pltpu.VMEM((FFN_TM, d), BF16),
            pltpu.VMEM((d // LANES, FFN_TM, LANES), F32),
        ],
        compiler_params=_cparams(("arbitrary", "arbitrary")),
        name="ffn_prompt",
    )(x, gain, wup, cw, cb, wdn)


def _ffn_sample_kernel(x_ref, g_ref, w_ref, cw_ref, cb_ref, wdn_ref, sc_ref,
                       xo_ref, cs_ref, gate):
    j = pl.program_id(0)

    @pl.when(j == 0)
    def _():
        xo_ref[...] = x_ref[...]

    h = _rms(x_ref[...], g_ref[...]).astype(BF16)
    up = jnp.dot(h, w_ref[...], preferred_element_type=F32)
    b0 = sc_ref[:, 0, :]
    b1 = sc_ref[:, 1, :]
    c = cb_ref[...]
    c = c + cw_ref[0:1, :] * b0
    c = c + cw_ref[1:2, :] * b1
    c = c + cw_ref[2:3, :] * up
    cs_ref[...] = jnp.stack([b1, up], axis=1)

    @pl.when(j < N_FF_BLK)
    def _():
        gate[j] = jax.nn.silu(c)

    @pl.when(j >= N_FF_BLK)
    def _():
        a = (gate[j - N_FF_BLK] * c).astype(BF16)
        xo_ref[...] += jnp.dot(a, wdn_ref[...], preferred_element_type=F32)


def _ffn_sample(x, gain, wup, cw, cb, wdn, state, layer):
    n, d = x.shape
    nb = N_FF_BLK
    sblk = pl.BlockSpec((None, n, 2, FF_BLK), lambda j: (layer, 0, 0, j))
    in_specs = [
        _const_spec((n, d)),
        _const_spec((1, d)),
        pl.BlockSpec((None, d, FF_BLK), lambda j: (layer, 0, j)),
        pl.BlockSpec((None, 3, FF_BLK), lambda j: (layer, 0, j)),
        pl.BlockSpec((None, 1, FF_BLK), lambda j: (layer, 0, j)),
        pl.BlockSpec((None, FF_BLK, d), lambda j: (layer, jnp.maximum(j - nb, 0), 0)),
        sblk,
    ]
    args = [x, gain, wup, cw, cb, wdn, state]
    return pl.pallas_call(
        _ffn_sample_kernel,
        grid=(2 * nb,),
        in_specs=in_specs,
        out_specs=[_const_spec((n, d)), sblk],
        out_shape=[
            jax.ShapeDtypeStruct((n, d), F32),
            jax.ShapeDtypeStruct(state.shape, F32),
        ],
        scratch_shapes=[pltpu.VMEM((nb, n, FF_BLK), F32)],
        input_output_aliases={len(args) - 1: 1},
        compiler_params=_cparams(("arbitrary",)),
        name="ffn_sample",
    )(*args)


def _tile_states(t):
    t2 = jnp.concatenate([t, t], axis=1)
    return jnp.concatenate([t2] * (GB_GROUPS // 2), axis=1)


def _ssm_prep_kernel(are_ref, aim_ref, dt_ref, bre_ref, bim_ref, cre_ref, cim_ref,
                     aflat_ref, iflat_ref, dtflat_ref,
                     w_ref, win_ref, cpt_ref, c0t_ref, a8_ref, a1_ref):
    a_re = are_ref[...]
    a_im = aim_ref[...]
    dt = jnp.exp(dt_ref[...])

    mag = jnp.exp(a_re * dt)
    ab_re, ab_im = mag * jnp.cos(a_im * dt), mag * jnp.sin(a_im * dt)
    pw = [(jnp.ones_like(ab_re), jnp.zeros_like(ab_re)), (ab_re, ab_im)]
    for _ in range(CHUNK - 1):
        p_re, p_im = pw[-1]
        pw.append((p_re * ab_re - p_im * ab_im, p_re * ab_im + p_im * ab_re))
    den = a_re * a_re + a_im * a_im
    z_re = ((ab_re - 1.0) * a_re + ab_im * a_im) / den
    z_im = (ab_im * a_re - (ab_re - 1.0) * a_im) / den
    b_re = bre_ref[...]
    b_im = bim_ref[...]
    bb_re = z_re * b_re - z_im * b_im
    bb_im = z_re * b_im + z_im * b_re
    c_re = cre_ref[...]
    c_im = cim_ref[...]

    rg = lax.broadcasted_iota(jnp.int32, (LANES, GB_ST2), 0) // SSM_GROUP
    lg = (lax.broadcasted_iota(jnp.int32, (LANES, GB_ST2), 1) % GB_ST) // SSM_STATE
    own = rg == lg

    def blockdiag(t_re, t_im):
        full = jnp.concatenate([_tile_states(t_re), _tile_states(t_im)], axis=1)
        return jnp.where(own, full, 0.0)

    c0t = blockdiag(c_re, -c_im)
    c0t_ref[...] = c0t.astype(BF16)

    lfull = []
    for d in range(CHUNK):
        if d == 0:
            l_re, l_im = bb_re, bb_im
        else:
            e_re, e_im = pw[d]
            l_re = e_re * bb_re - e_im * bb_im
            l_im = e_re * bb_im + e_im * bb_re
        lfull.append(blockdiag(l_re, l_im))

    def split(t):
        hi = t.astype(BF16)
        return hi, (t - hi.astype(F32)).astype(BF16)

    def dot_nt(a, b):
        return lax.dot_general(a, b, (((1,), (1,)), ((), ())), preferred_element_type=F32)

    c_hi, c_lo = split(c0t)
    kd = []
    for l in lfull:
        l_hi, l_lo = split(l)
        kd.append(dot_nt(l_hi, c_hi) + dot_nt(l_hi, c_lo) + dot_nt(l_lo, c_hi))
    zero = jnp.zeros((LANES, LANES), F32)
    for s in range(CHUNK):
        row = jnp.concatenate([kd[t - s] if t >= s else zero for t in range(CHUNK)], axis=1)
        w_ref[s * LANES:(s + 1) * LANES, :] = row.astype(BF16)
        win_ref[s * LANES:(s + 1) * LANES, :] = lfull[CHUNK - 1 - s].astype(BF16)
    for t in range(CHUNK):
        e_re, e_im = pw[t + 1]
        cpt_ref[t * LANES:(t + 1) * LANES, :] = blockdiag(
            c_re * e_re - c_im * e_im, -c_re * e_im - c_im * e_re).astype(BF16)

    af = aflat_ref[...]
    ai = iflat_ref[...]
    dtf = jnp.exp(dtflat_ref[...])
    magf = jnp.exp(af * dtf)
    f_re, f_im = magf * jnp.cos(ai * dtf), magf * jnp.sin(ai * dtf)
    a1_ref[...] = jnp.concatenate([f_re, f_im], axis=1)
    for _ in range(CHUNK.bit_length() - 1):
        f_re, f_im = f_re * f_re - f_im * f_im, 2.0 * f_re * f_im
    a8_ref[...] = jnp.concatenate([f_re, f_im], axis=1)


def _ssm_prep(a_re, a_im, log_dt, b_re, b_im, c_re, c_im):
    rep = lambda t: jnp.repeat(t, SSM_GROUP, axis=0)
    are_x = rep(a_re)
    aim_x = rep(a_im)
    dt_x = rep(log_dt[:, None])
    bt_re = jnp.transpose(b_re, (0, 2, 1)).reshape(N_GROUPS * SSM_GROUP, SSM_STATE)
    bt_im = jnp.transpose(b_im, (0, 2, 1)).reshape(N_GROUPS * SSM_GROUP, SSM_STATE)
    cr = c_re.reshape(N_GROUPS * SSM_GROUP, SSM_STATE)
    ci = c_im.reshape(N_GROUPS * SSM_GROUP, SSM_STATE)
    aflat = a_re.reshape(N_GB, 1, GB_ST)
    iflat = a_im.reshape(N_GB, 1, GB_ST)
    dtflat = jnp.repeat(log_dt, SSM_STATE).reshape(N_GB, 1, GB_ST)
    rows = pl.BlockSpec((LANES, SSM_STATE), lambda g: (g, 0))
    flat = pl.BlockSpec((None, 1, GB_ST), lambda g: (g, 0, 0))
    big = pl.BlockSpec((None, GB_IN, GB_ST2), lambda g: (g, 0, 0))
    vec = pl.BlockSpec((None, 1, GB_ST2), lambda g: (g, 0, 0))
    return pl.pallas_call(
        _ssm_prep_kernel,
        grid=(N_GB,),
        in_specs=[rows, rows, pl.BlockSpec((LANES, 1), lambda g: (g, 0)),
                  rows, rows, rows, rows, flat, flat, flat],
        out_specs=[big, big, big,
                   pl.BlockSpec((None, LANES, GB_ST2), lambda g: (g, 0, 0)), vec, vec],
        out_shape=[
            jax.ShapeDtypeStruct((N_GB, GB_IN, GB_IN), BF16),
            jax.ShapeDtypeStruct((N_GB, GB_IN, GB_ST2), BF16),
            jax.ShapeDtypeStruct((N_GB, GB_IN, GB_ST2), BF16),
            jax.ShapeDtypeStruct((N_GB, LANES, GB_ST2), BF16),
            jax.ShapeDtypeStruct((N_GB, 1, GB_ST2), F32),
            jax.ShapeDtypeStruct((N_GB, 1, GB_ST2), F32),
        ],
        compiler_params=_cparams(("arbitrary",)),
        name="ssm_prep",
    )(are_x, aim_x, dt_x, bt_re, bt_im, cr, ci, aflat, iflat, dtflat)


SSM_TM = 512
SSM_TC = SSM_TM // CHUNK


def _ssm_pre_kernel(x_ref, g_ref, uc_ref, hs):
    h = _rms(x_ref[...], g_ref[...])
    for gb in range(N_GB):
        hs[gb] = h[:, gb * LANES:(gb + 1) * LANES]
    for gb in range(N_GB):
        for t in range(CHUNK):
            uc_ref[:, gb * GB_IN + t * LANES:gb * GB_IN + (t + 1) * LANES] = (
                hs[gb, pl.ds(t, SSM_TC, stride=CHUNK), :].astype(BF16))


def _ssm_pre(x, gain):
    b, l, d = x.shape
    nt = l // SSM_TM
    return pl.pallas_call(
        _ssm_pre_kernel,
        grid=(b, nt),
        in_specs=[pl.BlockSpec((None, SSM_TM, d), lambda bi, i: (bi, i, 0)), _const_spec((1, d))],
        out_specs=pl.BlockSpec((SSM_TC, CHUNK * d), lambda bi, i: (bi * nt + i, 0)),
        out_shape=jax.ShapeDtypeStruct((b * l // CHUNK, CHUNK * d), BF16),
        scratch_shapes=[pltpu.VMEM((N_GB, SSM_TM, LANES), F32)],
        compiler_params=_cparams(("arbitrary", "arbitrary")),
        name="ssm_pre",
    )(x, gain)


SSM_RT = 256
SCAN_UNROLL = 8
SCAN_PAD = 8


def _ssm_core_kernel(uc_ref, w_ref, win_ref, cpt_ref, a8_ref, yc_ref, sre_ref, sim_ref, zs,
                     *, n_seq, n_chunk):
    n_rows = n_seq * n_chunk
    nlb = GB_ST2 // LANES
    half = nlb // 2
    pitch = n_chunk + SCAN_PAD

    def zrows(r):
        seq, off = divmod(r * SSM_RT, n_chunk)
        return slice(seq * pitch + off, seq * pitch + off + SSM_RT)

    for r in range(n_rows // SSM_RT):
        rs = slice(r * SSM_RT, (r + 1) * SSM_RT)
        z = jnp.dot(uc_ref[rs, :], win_ref[...], preferred_element_type=F32)
        for c in range(nlb):
            zs[c, zrows(r), :] = z[:, c * LANES:(c + 1) * LANES]

    a8 = a8_ref[...]
    a_bl = [jnp.broadcast_to(a8[:, c * LANES:(c + 1) * LANES], (n_seq, LANES))
            for c in range(nlb)]

    def step(k, carry):
        rows = pl.ds(k, n_seq, stride=pitch)
        new = []
        for c in range(half):
            s_re, s_im = carry[c], carry[half + c]
            z_re = zs[c, rows, :]
            z_im = zs[half + c, rows, :]
            zs[c, rows, :] = s_re
            zs[half + c, rows, :] = s_im
            new.append((a_bl[c] * s_re - a_bl[half + c] * s_im + z_re,
                        a_bl[c] * s_im + a_bl[half + c] * s_re + z_im))
        return tuple(n[0] for n in new) + tuple(n[1] for n in new)

    zero = jnp.zeros((n_seq, LANES), F32)
    fin = lax.fori_loop(0, n_chunk, step, (zero,) * nlb, unroll=SCAN_UNROLL)
    sre_ref[...] = jnp.concatenate(fin[:half], axis=1)
    sim_ref[...] = jnp.concatenate(fin[half:], axis=1)

    for r in range(n_rows // SSM_RT):
        rs = slice(r * SSM_RT, (r + 1) * SSM_RT)
        sp = jnp.concatenate([zs[c, zrows(r), :] for c in range(nlb)], axis=1).astype(BF16)
        y = lax.dot_general(sp, cpt_ref[...], (((1,), (1,)), ((), ())),
                            preferred_element_type=F32)
        for m in range(CHUNK // 2):
            kin = (2 * m + 2) * LANES
            oc = slice(2 * m * LANES, (2 * m + 2) * LANES)
            yc_ref[rs, oc] = y[:, oc] + jnp.dot(uc_ref[rs, :kin], w_ref[:kin, oc],
                                               preferred_element_type=F32)


def _ssm_core(uc, w, win, cpt, a8, n_seq):
    n_rows = uc.shape[0]
    n_chunk = n_rows // n_seq
    mat = lambda r: pl.BlockSpec((None, r, GB_ST2), lambda g: (g, 0, 0))
    blk = pl.BlockSpec((n_rows, GB_IN), lambda g: (0, g))
    st = pl.BlockSpec((n_seq, GB_ST), lambda g: (0, g))
    return pl.pallas_call(
        functools.partial(_ssm_core_kernel, n_seq=n_seq, n_chunk=n_chunk),
        grid=(N_GB,),
        in_specs=[blk, mat(GB_IN), mat(GB_IN), mat(GB_IN), mat(1)],
        out_specs=[blk, st, st],
        out_shape=[
            jax.ShapeDtypeStruct((n_rows, N_GB * GB_IN), F32),
            jax.ShapeDtypeStruct((n_seq, N_GROUPS * SSM_STATE), F32),
            jax.ShapeDtypeStruct((n_seq, N_GROUPS * SSM_STATE), F32),
        ],
        scratch_shapes=[pltpu.VMEM((GB_ST2 // LANES, n_seq * (n_chunk + SCAN_PAD), LANES), F32)],
        compiler_params=_cparams(("arbitrary",)),
        name="ssm_core",
    )(uc, w, win, cpt, a8)


def _glu_tail(x, y, g_ref, dsk_ref, wglu_ref):
    h = _rms(x, g_ref[...])
    z = jax.nn.gelu(y + dsk_ref[...] * h).astype(BF16)
    zz = jnp.dot(z, wglu_ref[...], preferred_element_type=F32)
    return x + zz[:, :D_MODEL] * jax.nn.sigmoid(zz[:, D_MODEL:])


def _ssm_post_kernel(x_ref, yc_ref, g_ref, dsk_ref, wglu_ref, xo_ref, ys):
    for gb in range(N_GB):
        for t in range(CHUNK):
            ys[gb, pl.ds(t, SSM_TC, stride=CHUNK), :] = (
                yc_ref[:, gb * GB_IN + t * LANES:gb * GB_IN + (t + 1) * LANES])
    y = jnp.concatenate([ys[gb] for gb in range(N_GB)], axis=1)
    xo_ref[...] = _glu_tail(x_ref[...], y, g_ref, dsk_ref, wglu_ref)


def _ssm_post(x, yc, gain, dsk, wglu, layer):
    b, l, d = x.shape
    nt = l // SSM_TM
    tok = pl.BlockSpec((None, SSM_TM, d), lambda bi, i: (bi, i, 0))
    return pl.pallas_call(
        _ssm_post_kernel,
        grid=(b, nt),
        in_specs=[tok, pl.BlockSpec((SSM_TC, CHUNK * d), lambda bi, i: (bi * nt + i, 0)),
                  _const_spec((1, d)), _const_spec((1, d)),
                  pl.BlockSpec((None, d, 2 * d), lambda bi, i: (layer, 0, 0))],
        out_specs=tok,
        out_shape=jax.ShapeDtypeStruct((b, l, d), F32),
        scratch_shapes=[pltpu.VMEM((N_GB, SSM_TM, LANES), F32)],
        compiler_params=_cparams(("arbitrary", "arbitrary")),
        name="ssm_post",
    )(x, yc, gain, dsk, wglu)


def _ssm_sample_kernel(x_ref, g_ref, bb_ref, c0t_ref, a1_ref, hre_ref, him_ref,
                       y_ref, sre_ref, sim_ref):
    gb = pl.program_id(0)
    x = x_ref[...]
    ms = jnp.mean(x * x, axis=-1, keepdims=True)
    cols = pl.ds(pl.multiple_of(gb * LANES, LANES), LANES)
    u = (x_ref[:, cols] * lax.rsqrt(ms + RMS_EPS) * g_ref[:, cols]).astype(BF16)
    bu = jnp.dot(u, bb_ref[...], preferred_element_type=F32)
    a1 = a1_ref[...]
    a_re = a1[:, :GB_ST]
    a_im = a1[:, GB_ST:]
    h_re = hre_ref[...]
    h_im = him_ref[...]
    s_re = a_re * h_re - a_im * h_im + bu[:, :GB_ST]
    s_im = a_re * h_im + a_im * h_re + bu[:, GB_ST:]
    sre_ref[...] = s_re
    sim_ref[...] = s_im
    s = jnp.concatenate([s_re, s_im], axis=1).astype(BF16)
    y_ref[...] = lax.dot_general(s, c0t_ref[...], (((1,), (1,)), ((), ())),
                                 preferred_element_type=F32)


def _ssm_sample(x, gain, win, c0t, a1, h_re, h_im):
    n, d = x.shape
    st = pl.BlockSpec((n, GB_ST), lambda g: (0, g))
    return pl.pallas_call(
        _ssm_sample_kernel,
        grid=(N_GB,),
        in_specs=[
            _const_spec((n, d)),
            _const_spec((1, d)),
            pl.BlockSpec((None, LANES, GB_ST2), lambda g: (g, CHUNK - 1, 0)),
            pl.BlockSpec((None, LANES, GB_ST2), lambda g: (g, 0, 0)),
            pl.BlockSpec((None, 1, GB_ST2), lambda g: (g, 0, 0)),
            st, st,
        ],
        out_specs=[pl.BlockSpec((n, LANES), lambda g: (0, g)), st, st],
        out_shape=[
            jax.ShapeDtypeStruct((n, d), F32),
            jax.ShapeDtypeStruct((n, N_GROUPS * SSM_STATE), F32),
            jax.ShapeDtypeStruct((n, N_GROUPS * SSM_STATE), F32),
        ],
        compiler_params=_cparams(("arbitrary",)),
        name="ssm_sample",
    )(x, gain, win, c0t, a1, h_re, h_im)


def _glu_sample_kernel(x_ref, y_ref, g_ref, dsk_ref, wglu_ref, xo_ref):
    xo_ref[...] = _glu_tail(x_ref[...], y_ref[...], g_ref, dsk_ref, wglu_ref)


def _glu_sample(x, y, gain, dsk, wglu, layer):
    n, d = x.shape
    return pl.pallas_call(
        _glu_sample_kernel,
        grid=(1,),
        in_specs=[_const_spec((n, d)), _const_spec((n, d)), _const_spec((1, d)),
                  _const_spec((1, d)), pl.BlockSpec((None, d, 2 * d), lambda i: (layer, 0, 0))],
        out_specs=_const_spec((n, d)),
        out_shape=jax.ShapeDtypeStruct((n, d), F32),
        compiler_params=_cparams(("arbitrary",)),
        name="glu_sample",
    )(x, y, gain, dsk, wglu)


def kernel(x_prompt, x_sample, cache_k, cache_v, state_ssm_re, state_ssm_im, state_conv,
           norm_mix, norm_ffn, w_qkv, w_o, q_norm, k_norm, sinks, ssm_a_re, ssm_a_im,
           ssm_log_dt, ssm_b_re, ssm_b_im, ssm_c_re, ssm_c_im, ssm_d, w_glu, w_up, conv_w,
           conv_b, w_down):
    depth = norm_mix.shape[0]
    nb, _, _ = x_prompt.shape
    ns = x_sample.shape[0]
    xp = x_prompt
    xs = x_sample.reshape(ns, D_MODEL)
    row = lambda t: t.reshape(1, -1)

    wqkv = w_qkv.astype(BF16)
    wo = w_o.astype(BF16)
    wo_t = jnp.swapaxes(wo, 1, 2)
    wglu = w_glu.astype(BF16)
    wup = w_up.astype(BF16)
    wdn = w_down.astype(BF16)
    cb = conv_b.reshape(depth, 1, F2)
    n_att = wqkv.shape[0]
    wq_g = wqkv[:, :, :D_MODEL].reshape(n_att, D_MODEL, N_KV, GQA, HEAD_DIM)
    wq_g = jnp.transpose(wq_g, (0, 3, 1, 2, 4)).reshape(n_att, GQA, D_MODEL, KV_DIM)
    wo_g = wo.reshape(n_att, N_KV, GQA, HEAD_DIM, D_MODEL)
    wo_g = jnp.transpose(wo_g, (0, 2, 1, 3, 4)).reshape(n_att, GQA, KV_DIM, D_MODEL)
    ck = cache_k.reshape(n_att, ns, WINDOW, KV_DIM)
    cv = cache_v.reshape(n_att, ns, WINDOW, KV_DIM)

    kps, vps = [], []
    srp, sip, srs, sis = [], [], [], []
    cps = []
    conv_s = state_conv
    for i in range(depth):
        j = i // 2
        gm = row(norm_mix[i])
        if i % 2 == 0:
            qg, kg = row(q_norm[j]), row(k_norm[j])
            xp, kp, vp = _attn_prompt(xp, gm, wqkv, wo_t, jnp.tile(qg, (1, N_HEADS)),
                                      jnp.tile(kg, (1, N_KV)), sinks[j], j)
            xs, ck, cv = _attn_sample(xs, gm, wq_g, wqkv, wo_g, qg, kg, sinks[j], ck, cv, j)
            kps.append(kp.reshape(nb, WINDOW, N_KV, HEAD_DIM))
            vps.append(vp.reshape(nb, WINDOW, N_KV, HEAD_DIM))
        else:
            w, win, cpt, c0t, a8, a1 = _ssm_prep(
                ssm_a_re[j], ssm_a_im[j], ssm_log_dt[j], ssm_b_re[j], ssm_b_im[j],
                ssm_c_re[j], ssm_c_im[j])
            dsk = row(ssm_d[j])
            uc = _ssm_pre(xp, gm)
            yc, s_re, s_im = _ssm_core(uc, w, win, cpt, a8, nb)
            xp = _ssm_post(xp, yc, gm, dsk, wglu, j)
            srp.append(s_re.reshape(nb, N_GROUPS, SSM_STATE))
            sip.append(s_im.reshape(nb, N_GROUPS, SSM_STATE))
            ys, t_re, t_im = _ssm_sample(
                xs, gm, win, c0t, a1,
                state_ssm_re[j].reshape(ns, N_GROUPS * SSM_STATE),
                state_ssm_im[j].reshape(ns, N_GROUPS * SSM_STATE))
            xs = _glu_sample(xs, ys, gm, dsk, wglu, j)
            srs.append(t_re.reshape(ns, N_GROUPS, SSM_STATE))
            sis.append(t_im.reshape(ns, N_GROUPS, SSM_STATE))

        gf = row(norm_ffn[i])
        xp, cp = _ffn_prompt(xp, gf, wup, conv_w, cb, wdn, i)
        xs, conv_s = _ffn_sample(xs, gf, wup, conv_w, cb, wdn, conv_s, i)
        cps.append(cp)

    k_s, v_s = (t.reshape(n_att, ns, WINDOW, N_KV, HEAD_DIM) for t in (ck, cv))
    return (xp, xs.reshape(ns, 1, D_MODEL),
            jnp.stack(kps), jnp.stack(vps), k_s, v_s,
            jnp.stack(srp), jnp.stack(sip), jnp.stack(srs), jnp.stack(sis),
            jnp.stack(cps), conv_s)
```

```python
import functools

import jax
import jax.numpy as jnp
from jax import lax
from jax.experimental import pallas as pl
from jax.experimental.pallas import tpu as pltpu

F32 = jnp.float32
BF16 = jnp.bfloat16

D_MODEL = 1024
HEAD_DIM = 64
N_HEADS = 16
N_KV = 4
GQA = 4
WINDOW = 128
KV_DIM = N_KV * HEAD_DIM
QKV_DIM = D_MODEL + 2 * KV_DIM
D_FF = 2816
F2 = 2 * D_FF
FF_BLK = 256
N_FF_BLK = D_FF // FF_BLK
SSM_GROUP = 16
N_GROUPS = 64
SSM_STATE = 64
RMS_EPS = 1e-6

LANES = 128
CHUNK = 8
GB_GROUPS = LANES // SSM_GROUP
N_GB = N_GROUPS // GB_GROUPS
GB_IN = CHUNK * LANES
GB_ST = GB_GROUPS * SSM_STATE
GB_ST2 = 2 * GB_ST

VMEM_LIMIT = 56 * 1024 * 1024


def _cparams(sem):
    return pltpu.CompilerParams(dimension_semantics=sem, vmem_limit_bytes=VMEM_LIMIT)


def _rms(x, g):
    ms = jnp.mean(x * x, axis=-1, keepdims=True)
    return x * lax.rsqrt(ms + RMS_EPS) * g


def _head_rms(t, n, g):
    return jnp.concatenate(
        [_rms(t[:, j * HEAD_DIM:(j + 1) * HEAD_DIM], g) for j in range(n)], axis=1)


def _const_spec(shape):
    nd = len(shape)
    return pl.BlockSpec(shape, lambda *_: (0,) * nd)


ATT_TQ = 512
ATT_NBLK = ATT_TQ // WINDOW
SEG_W = 256


def _seg_mean_sq(t, seg):
    sq = t * t
    hi = sq.astype(BF16)
    lo = (sq - hi.astype(F32)).astype(BF16)
    out = []
    for c in range(t.shape[1] // SEG_W):
        sl = slice(c * SEG_W, (c + 1) * SEG_W)
        out.append(jnp.dot(hi[:, sl], seg, preferred_element_type=F32)
                   + jnp.dot(lo[:, sl], seg, preferred_element_type=F32))
    return jnp.concatenate(out, axis=1) * (1.0 / HEAD_DIM)


VT_ROWS = HEAD_DIM + 16


def _attn_prompt_kernel(sink_ref, x_ref, g_ref, wqkv_ref, wot_ref, qg_ref, kg_ref,
                        xo_ref, kl_ref, vl_ref, kbuf, krol, vt, lo_scr, ot):
    i = pl.program_id(1)
    keys = 2 * WINDOW

    @pl.when(i == 0)
    def _():
        kbuf[0:WINDOW, :] = jnp.zeros((WINDOW, KV_DIM), BF16)
        krol[0:WINDOW, :] = jnp.zeros((WINDOW, KV_DIM), BF16)
        vt[:, 0:HEAD_DIM, 0:WINDOW] = jnp.zeros((N_KV, HEAD_DIM, WINDOW), BF16)
        vt[:, HEAD_DIM:, :] = jnp.ones((N_KV, VT_ROWS - HEAD_DIM, WINDOW + ATT_TQ), BF16)
        lo_scr[...] = jnp.full(lo_scr.shape, WINDOW, jnp.int32)

    @pl.when(i == 1)
    def _():
        lo_scr[...] = jnp.zeros(lo_scr.shape, jnp.int32)

    si = lax.broadcasted_iota(jnp.int32, (SEG_W, SEG_W), 0) // HEAD_DIM
    sj = lax.broadcasted_iota(jnp.int32, (SEG_W, SEG_W), 1) // HEAD_DIM
    seg = jnp.where(si == sj, 1.0, 0.0).astype(BF16)

    x = x_ref[...]
    h = _rms(x, g_ref[...]).astype(BF16)
    qkv = jnp.dot(h, wqkv_ref[...], preferred_element_type=F32)
    q = qkv[:, :D_MODEL]
    k = qkv[:, D_MODEL:D_MODEL + KV_DIM]
    v = qkv[:, D_MODEL + KV_DIM:]
    qn = q * lax.rsqrt(_seg_mean_sq(q, seg) + RMS_EPS) * qg_ref[...] * (HEAD_DIM ** -0.5)
    kn = k * lax.rsqrt(_seg_mean_sq(k, seg) + RMS_EPS) * kg_ref[...]
    kl_ref[...] = kn[ATT_TQ - WINDOW:]
    vl_ref[...] = v[ATT_TQ - WINDOW:]
    kbuf[WINDOW:, :] = kn.astype(BF16)
    for c in range(KV_DIM // LANES):
        sl = slice(c * LANES, (c + 1) * LANES)
        krol[WINDOW:, sl] = pltpu.roll(kn[:, sl], HEAD_DIM, 1).astype(BF16)
    v_t = v.T
    for kvh in range(N_KV):
        vt[kvh, 0:HEAD_DIM, WINDOW:] = v_t[kvh * HEAD_DIM:(kvh + 1) * HEAD_DIM].astype(BF16)

    kc = lax.broadcasted_iota(jnp.int32, (keys, 2 * WINDOW), 0)
    qi = lax.broadcasted_iota(jnp.int32, (keys, 2 * WINDOW), 1) % WINDOW
    band = (kc > qi) & (kc <= qi + WINDOW)
    low = lax.broadcasted_iota(jnp.int32, (WINDOW, LANES), 1) < HEAD_DIM
    nt = (((1,), (1,)), ((), ()))

    for blk in range(ATT_NBLK):
        r0 = blk * WINDOW
        valid = band & (kc >= lo_scr[...]) if blk == 0 else band
        combos = []
        for kvh in range(N_KV):
            khalf = kvh % 2
            combos.append((kvh, (khalf, khalf + 2), kbuf))
            combos.append((kvh, (1 - khalf, 3 - khalf), krol))
        scores = []
        for kvh, pair, kref in combos:
            kcol = slice((kvh // 2) * LANES, (kvh // 2 + 1) * LANES)
            qm = []
            for g in pair:
                qcol = slice((kvh * 2 + g // 2) * LANES, (kvh * 2 + g // 2 + 1) * LANES)
                qm.append(jnp.where(low if g % 2 == 0 else ~low, qn[r0:r0 + WINDOW, qcol], 0.0))
            scores.append(lax.dot_general(kref[r0:r0 + keys, kcol],
                                          jnp.concatenate(qm, axis=0).astype(BF16), nt,
                                          preferred_element_type=F32))
        probs = []
        for (kvh, pair, _), s in zip(combos, scores):
            sink_row = jnp.concatenate(
                [jnp.full((1, WINDOW), sink_ref[kvh * GQA + g], F32) for g in pair], axis=1)
            s = jnp.where(valid, s, -jnp.inf)
            m = jnp.maximum(jnp.max(s, axis=0, keepdims=True), sink_row)
            probs.append((jnp.exp(s - m).astype(BF16), jnp.exp(sink_row - m)))
        outs = [jnp.dot(vt[kvh, :, r0:r0 + keys], e, preferred_element_type=F32)
                for (kvh, _, _), (e, _) in zip(combos, probs)]
        for (kvh, pair, _), (_, e_sink), o in zip(combos, probs, outs):
            rden = 1.0 / (o[HEAD_DIM:HEAD_DIM + 1] + e_sink)
            on = (o[:HEAD_DIM] * rden).astype(BF16)
            for n, g in enumerate(pair):
                hd = kvh * GQA + g
                ot[hd * HEAD_DIM:(hd + 1) * HEAD_DIM, r0:r0 + WINDOW] = (
                    on[:, n * WINDOW:(n + 1) * WINDOW])

    kbuf[0:WINDOW, :] = kbuf[ATT_TQ:, :]
    krol[0:WINDOW, :] = krol[ATT_TQ:, :]
    vt[:, 0:HEAD_DIM, 0:WINDOW] = vt[:, 0:HEAD_DIM, ATT_TQ:]
    out_t = jnp.dot(wot_ref[...], ot[...], preferred_element_type=F32)
    xo_ref[...] = x + out_t.T


def _attn_prompt(x, gain, wqkv, wo_t, qg, kg, sinks, layer):
    b, l, d = x.shape
    wsel = lambda bi, i: (layer, 0, 0)
    tok = pl.BlockSpec((None, ATT_TQ, d), lambda bi, i: (bi, i, 0))
    last = pl.BlockSpec((None, WINDOW, KV_DIM), lambda bi, i: (bi, 0, 0))
    kvbuf = pltpu.VMEM((WINDOW + ATT_TQ, KV_DIM), BF16)
    return pl.pallas_call(
        _attn_prompt_kernel,
        grid=(b, l // ATT_TQ),
        in_specs=[
            pl.BlockSpec(memory_space=pltpu.SMEM),
            tok,
            _const_spec((1, d)),
            pl.BlockSpec((None, d, QKV_DIM), wsel),
            pl.BlockSpec((None, d, d), wsel),
            _const_spec((1, d)),
            _const_spec((1, KV_DIM)),
        ],
        out_specs=[tok, last, last],
        out_shape=[
            jax.ShapeDtypeStruct((b, l, d), F32),
            jax.ShapeDtypeStruct((b, WINDOW, KV_DIM), F32),
            jax.ShapeDtypeStruct((b, WINDOW, KV_DIM), F32),
        ],
        scratch_shapes=[
            kvbuf, kvbuf,
            pltpu.VMEM((N_KV, VT_ROWS, WINDOW + ATT_TQ), BF16),
            pltpu.VMEM((2 * WINDOW, 2 * WINDOW), jnp.int32),
            pltpu.VMEM((d, ATT_TQ), BF16),
        ],
        compiler_params=_cparams(("arbitrary", "arbitrary")),
        name="attn_prompt",
    )(sinks, x, gain, wqkv, wo_t, qg, kg)


ATS_TB = 8


def _attn_sample_kernel(sink_ref, x_ref, g_ref, wq_ref, wkv_ref, wo_ref, qg_ref, kg_ref,
                        ck_ref, cv_ref, xo_ref, ko_ref, vo_ref, r_scr):
    x = x_ref[...]
    h = _rms(x, g_ref[...]).astype(BF16)
    kv = jnp.dot(h, wkv_ref[...], preferred_element_type=F32)
    kn = _head_rms(kv[:, :KV_DIM], N_KV, kg_ref[...])
    v = kv[:, KV_DIM:]
    qg = qg_ref[...]
    q_g = [_head_rms(jnp.dot(h, wq_ref[g], preferred_element_type=F32), N_KV, qg)
           * (HEAD_DIM ** -0.5) for g in range(GQA)]

    rows = GQA * N_KV
    rkv = lax.broadcasted_iota(jnp.int32, (rows, KV_DIM), 0) % N_KV
    lkv = lax.broadcasted_iota(jnp.int32, (rows, KV_DIM), 1) // HEAD_DIM
    own = rkv == lkv
    colj = lax.broadcasted_iota(jnp.int32, (rows, WINDOW), 1)
    sink_col = jnp.concatenate(
        [jnp.full((1, 1), sink_ref[(r % N_KV) * GQA + r // N_KV], F32) for r in range(rows)],
        axis=0)

    for n in range(ATS_TB):
        ko_ref[n, 0:WINDOW - 1, :] = ck_ref[n, 1:WINDOW, :]
        ko_ref[n, WINDOW - 1:WINDOW, :] = kn[n:n + 1]
        vo_ref[n, 0:WINDOW - 1, :] = cv_ref[n, 1:WINDOW, :]
        vo_ref[n, WINDOW - 1:WINDOW, :] = v[n:n + 1]
    qbds = []
    for n in range(ATS_TB):
        qbd = jnp.concatenate(
            [jnp.broadcast_to(q_g[g][n:n + 1], (N_KV, KV_DIM)) for g in range(GQA)], axis=0)
        qbds.append(jnp.where(own, qbd, 0.0))
    scores = [lax.dot_general(qbds[n].astype(BF16), ck_ref[n].astype(BF16),
                              (((1,), (1,)), ((), ())), preferred_element_type=F32)
              for n in range(ATS_TB)]
    probs = []
    for n in range(ATS_TB):
        s = jnp.where(colj >= 1, scores[n], -jnp.inf)
        s_new = jnp.sum(qbds[n] * kn[n:n + 1], axis=-1, keepdims=True)
        m = jnp.maximum(jnp.maximum(jnp.max(s, axis=-1, keepdims=True), s_new), sink_col)
        e = jnp.exp(s - m)
        e_new = jnp.exp(s_new - m)
        rden = 1.0 / (jnp.sum(e, axis=-1, keepdims=True) + e_new + jnp.exp(sink_col - m))
        probs.append(((e * rden).astype(BF16), e_new * rden))
    outs = [jnp.dot(probs[n][0], cv_ref[n].astype(BF16), preferred_element_type=F32)
            for n in range(ATS_TB)]
    for n in range(ATS_TB):
        o = jnp.where(own, outs[n] + probs[n][1] * v[n:n + 1], 0.0)
        for g in range(GQA):
            r_scr[g, n:n + 1, :] = jnp.sum(o[g * N_KV:(g + 1) * N_KV], axis=0, keepdims=True)

    out = x
    for g in range(GQA):
        out = out + jnp.dot(r_scr[g].astype(BF16), wo_ref[g], preferred_element_type=F32)
    xo_ref[...] = out


def _attn_sample(x, gain, wq_g, wqkv, wo_g, qg, kg, sinks, cache_k, cache_v, layer):
    n, d = x.shape
    tok = pl.BlockSpec((ATS_TB, d), lambda i: (i, 0))
    cache = pl.BlockSpec((None, ATS_TB, WINDOW, KV_DIM), lambda i: (layer, i, 0, 0))
    in_specs = [
        pl.BlockSpec(memory_space=pltpu.SMEM),
        tok,
        _const_spec((1, d)),
        pl.BlockSpec((None, GQA, d, KV_DIM), lambda i: (layer, 0, 0, 0)),
        pl.BlockSpec((None, d, 2 * KV_DIM), lambda i: (layer, 0, D_MODEL // (2 * KV_DIM))),
        pl.BlockSpec((None, GQA, KV_DIM, d), lambda i: (layer, 0, 0, 0)),
        _const_spec((1, HEAD_DIM)),
        _const_spec((1, HEAD_DIM)),
        cache,
        cache,
    ]
    args = [sinks, x, gain, wq_g, wqkv, wo_g, qg, kg, cache_k, cache_v]
    return pl.pallas_call(
        _attn_sample_kernel,
        grid=(n // ATS_TB,),
        in_specs=in_specs,
        out_specs=[tok, cache, cache],
        out_shape=[
            jax.ShapeDtypeStruct((n, d), F32),
            jax.ShapeDtypeStruct(cache_k.shape, F32),
            jax.ShapeDtypeStruct(cache_v.shape, F32),
        ],
        scratch_shapes=[pltpu.VMEM((GQA, ATS_TB, KV_DIM), F32)],
        input_output_aliases={len(args) - 2: 1, len(args) - 1: 2},
        compiler_params=_cparams(("arbitrary",)),
        name="attn_sample",
    )(*args)


FFN_TM = 512
SSM_TC = FFN_TM // CHUNK
CARRY = 8


def _ffn_prompt_kernel(*refs, glu_in, uc_out):
    refs = list(refs)
    x_ref = refs.pop(0)
    if glu_in:
        yc_ref, gm_ref, dsk_ref, wglu_ref = (refs.pop(0) for _ in range(4))
    g_ref, wup_ref, cw_ref, cb_ref, wdn_ref = (refs.pop(0) for _ in range(5))
    if uc_out:
        gn_ref = refs.pop(0)
    xo_ref, cs_ref = refs.pop(0), refs.pop(0)
    if uc_out:
        uc_ref = refs.pop(0)
    carry, upbuf, act, hs, oscr = refs

    i = pl.program_id(1)
    hm = FFN_TM // 2
    nlb = FF_BLK // LANES
    dlb = D_MODEL // LANES

    @pl.when(i == 0)
    def _():
        carry[...] = jnp.zeros_like(carry)

    if glu_in:
        for gb in range(N_GB):
            for t in range(CHUNK):
                oscr[gb, pl.ds(t, SSM_TC, stride=CHUNK), :] = (
                    yc_ref[:, gb * GB_IN + t * LANES:gb * GB_IN + (t + 1) * LANES])
        y = jnp.concatenate([oscr[gb] for gb in range(dlb)], axis=1)
        xo_ref[...] = _glu_tail(x_ref[...], y, gm_ref, dsk_ref, wglu_ref)
        xin_ref = xo_ref
    else:
        xin_ref = x_ref

    hs[...] = _rms(xin_ref[...], g_ref[...]).astype(BF16)

    def up_into(j, slot):
        for half, c0 in enumerate((j * FF_BLK, D_FF + j * FF_BLK)):
            cols = pl.ds(c0, FF_BLK)
            up = jnp.dot(hs[...], wup_ref[:, cols], preferred_element_type=F32)
            for lb in range(nlb):
                lcols = pl.ds(c0 + lb * LANES, LANES)
                buf = upbuf.at[slot, half, lb]
                buf[0:CARRY, :] = carry[:, lcols]
                buf[CARRY:, :] = up[:, lb * LANES:(lb + 1) * LANES]
                carry[:, lcols] = buf[FFN_TM:, :]

    def conv(slot, half, c0):
        ev, od = [], []
        for lb in range(nlb):
            lcols = pl.ds(c0 + lb * LANES, LANES)
            buf = upbuf.at[slot, half, lb]
            tap = lambda off: buf[pl.ds(CARRY + off, hm, stride=2), :]
            em2, om1, e0, o1 = tap(-2), tap(-1), tap(0), tap(1)
            w0, w1, w2 = cw_ref[0:1, lcols], cw_ref[1:2, lcols], cw_ref[2:3, lcols]
            cb = cb_ref[:, lcols]
            ev.append(cb + w0 * em2 + w1 * om1 + w2 * e0)
            od.append(cb + w0 * om1 + w1 * e0 + w2 * o1)
        return jnp.concatenate(ev, axis=1), jnp.concatenate(od, axis=1)

    for j in range(N_FF_BLK):
        up_into(j, j % 2)
        cg = conv(j % 2, 0, j * FF_BLK)
        cv = conv(j % 2, 1, D_FF + j * FF_BLK)
        for p in range(2):
            act[p * hm:(p + 1) * hm, j * FF_BLK:(j + 1) * FF_BLK] = (
                jax.nn.silu(cg[p]) * cv[p]).astype(BF16)

    out = jnp.dot(act[...], wdn_ref[...], preferred_element_type=F32)
    for lb in range(dlb):
        for p in range(2):
            oscr[lb, pl.ds(p, hm, stride=2), :] = out[p * hm:(p + 1) * hm,
                                                      lb * LANES:(lb + 1) * LANES]
    xo = xin_ref[...] + jnp.concatenate([oscr[lb] for lb in range(dlb)], axis=1)
    xo_ref[...] = xo
    cs_ref[...] = carry[CARRY - 2:CARRY, :]

    if uc_out:
        h2 = _rms(xo, gn_ref[...])
        for gb in range(N_GB):
            oscr[gb] = h2[:, gb * LANES:(gb + 1) * LANES]
        for gb in range(N_GB):
            for t in range(CHUNK):
                uc_ref[:, gb * GB_IN + t * LANES:gb * GB_IN + (t + 1) * LANES] = (
                    oscr[gb, pl.ds(t, SSM_TC, stride=CHUNK), :].astype(BF16))


def _ffn_prompt(x, gain, wup, cw, cb, wdn, layer, glu=None, next_gain=None):
    b, l, d = x.shape
    nt = l // FFN_TM
    tok = pl.BlockSpec((None, FFN_TM, d), lambda bi, i: (bi, i, 0))
    chunks = pl.BlockSpec((SSM_TC, CHUNK * d), lambda bi, i: (bi * nt + i, 0))
    wsel = lambda bi, i: (layer, 0, 0)
    args, in_specs = [x], [tok]
    if glu is not None:
        yc, gm, dsk, wglu, mixer_layer = glu
        args += [yc, gm, dsk, wglu]
        in_specs += [chunks, _const_spec((1, d)), _const_spec((1, d)),
                     pl.BlockSpec((None, d, 2 * d), lambda bi, i: (mixer_layer, 0, 0),
                                  pipeline_mode=pl.Buffered(1))]
    args += [gain, wup, cw, cb, wdn]
    in_specs += [
        _const_spec((1, d)),
        pl.BlockSpec((None, d, F2), wsel, pipeline_mode=pl.Buffered(1)),
        pl.BlockSpec((None, 3, F2), wsel),
        pl.BlockSpec((None, 1, F2), wsel),
        pl.BlockSpec((None, D_FF, d), wsel, pipeline_mode=pl.Buffered(1)),
    ]
    out_specs = [tok, pl.BlockSpec((None, 2, F2), lambda bi, i: (bi, 0, 0))]
    out_shape = [jax.ShapeDtypeStruct((b, l, d), F32), jax.ShapeDtypeStruct((b, 2, F2), F32)]
    if next_gain is not None:
        args.append(next_gain)
        in_specs.append(_const_spec((1, d)))
        out_specs.append(chunks)
        out_shape.append(jax.ShapeDtypeStruct((b * l // CHUNK, CHUNK * d), BF16))
    return pl.pallas_call(
        functools.partial(_ffn_prompt_kernel, glu_in=glu is not None,
                          uc_out=next_gain is not None),
        grid=(b, nt),
        in_specs=in_specs,
        out_specs=out_specs,
        out_shape=out_shape,
        scratch_shapes=[
            pltpu.VMEM((CARRY, F2), F32),
            pltpu.VMEM((2, 2, FF_BLK // LANES, CARRY + FFN_TM, LANES), F32),
            pltpu.VMEM((FFN_TM, D_FF), BF16),
            pltpu.VMEM((FFN_TM, d), BF16),
            pltpu.VMEM((d // LANES, FFN_TM, LANES), F32),
        ],
        compiler_params=_cparams(("arbitrary", "arbitrary")),
        name="ffn_prompt",
    )(*args)


FFS_BLK = D_FF // 2
N_FFS_BLK = D_FF // FFS_BLK


def _ffn_sample_kernel(x_ref, g_ref, w_ref, cw_ref, cb_ref, wdn_ref, sc_ref,
                       xo_ref, cs_ref, gate):
    j = pl.program_id(0)

    @pl.when(j == 0)
    def _():
        xo_ref[...] = x_ref[...]

    h = _rms(x_ref[...], g_ref[...]).astype(BF16)
    up = jnp.dot(h, w_ref[...], preferred_element_type=F32)
    b0 = sc_ref[:, 0, :]
    b1 = sc_ref[:, 1, :]
    c = cb_ref[...]
    c = c + cw_ref[0:1, :] * b0
    c = c + cw_ref[1:2, :] * b1
    c = c + cw_ref[2:3, :] * up
    cs_ref[...] = jnp.stack([b1, up], axis=1)

    @pl.when(j < N_FFS_BLK)
    def _():
        gate[j] = jax.nn.silu(c)

    @pl.when(j >= N_FFS_BLK)
    def _():
        a = (gate[j - N_FFS_BLK] * c).astype(BF16)
        xo_ref[...] += jnp.dot(a, wdn_ref[...], preferred_element_type=F32)


def _ffn_sample(x, gain, wup, cw, cb, wdn, state, layer):
    n, d = x.shape
    nb = N_FFS_BLK
    sblk = pl.BlockSpec((None, n, 2, FFS_BLK), lambda j: (layer, 0, 0, j))
    in_specs = [
        _const_spec((n, d)),
        _const_spec((1, d)),
        pl.BlockSpec((None, d, FFS_BLK), lambda j: (layer, 0, j)),
        pl.BlockSpec((None, 3, FFS_BLK), lambda j: (layer, 0, j)),
        pl.BlockSpec((None, 1, FFS_BLK), lambda j: (layer, 0, j)),
        pl.BlockSpec((None, FFS_BLK, d), lambda j: (layer, jnp.maximum(j - nb, 0), 0)),
        sblk,
    ]
    args = [x, gain, wup, cw, cb, wdn, state]
    return pl.pallas_call(
        _ffn_sample_kernel,
        grid=(2 * nb,),
        in_specs=in_specs,
        out_specs=[_const_spec((n, d)), sblk],
        out_shape=[
            jax.ShapeDtypeStruct((n, d), F32),
            jax.ShapeDtypeStruct(state.shape, F32),
        ],
        scratch_shapes=[pltpu.VMEM((nb, n, FFS_BLK), F32)],
        input_output_aliases={len(args) - 1: 1},
        compiler_params=_cparams(("arbitrary",)),
        name="ffn_sample",
    )(*args)


def _tile_states(t):
    t2 = jnp.concatenate([t, t], axis=1)
    return jnp.concatenate([t2] * (GB_GROUPS // 2), axis=1)


def _ssm_prep_kernel(are_ref, aim_ref, dt_ref, bre_ref, bim_ref, cre_ref, cim_ref,
                     aflat_ref, iflat_ref, dtflat_ref,
                     w_ref, win_ref, cpt_ref, c0t_ref, a8_ref, a1_ref):
    a_re = are_ref[...]
    a_im = aim_ref[...]
    dt = jnp.exp(dt_ref[...])

    mag = jnp.exp(a_re * dt)
    ab_re, ab_im = mag * jnp.cos(a_im * dt), mag * jnp.sin(a_im * dt)
    pw = [(jnp.ones_like(ab_re), jnp.zeros_like(ab_re)), (ab_re, ab_im)]
    for _ in range(CHUNK - 1):
        p_re, p_im = pw[-1]
        pw.append((p_re * ab_re - p_im * ab_im, p_re * ab_im + p_im * ab_re))
    den = a_re * a_re + a_im * a_im
    z_re = ((ab_re - 1.0) * a_re + ab_im * a_im) / den
    z_im = (ab_im * a_re - (ab_re - 1.0) * a_im) / den
    b_re = bre_ref[...]
    b_im = bim_ref[...]
    bb_re = z_re * b_re - z_im * b_im
    bb_im = z_re * b_im + z_im * b_re
    c_re = cre_ref[...]
    c_im = cim_ref[...]

    rg = lax.broadcasted_iota(jnp.int32, (LANES, GB_ST2), 0) // SSM_GROUP
    lg = (lax.broadcasted_iota(jnp.int32, (LANES, GB_ST2), 1) % GB_ST) // SSM_STATE
    own = rg == lg

    def blockdiag(t_re, t_im):
        full = jnp.concatenate([_tile_states(t_re), _tile_states(t_im)], axis=1)
        return jnp.where(own, full, 0.0)

    c0t = blockdiag(c_re, -c_im)
    c0t_ref[...] = c0t.astype(BF16)

    lfull = []
    for d in range(CHUNK):
        if d == 0:
            l_re, l_im = bb_re, bb_im
        else:
            e_re, e_im = pw[d]
            l_re = e_re * bb_re - e_im * bb_im
            l_im = e_re * bb_im + e_im * bb_re
        lfull.append(blockdiag(l_re, l_im))

    def split(t):
        hi = t.astype(BF16)
        return hi, (t - hi.astype(F32)).astype(BF16)

    def dot_nt(a, b):
        return lax.dot_general(a, b, (((1,), (1,)), ((), ())), preferred_element_type=F32)

    c_hi, c_lo = split(c0t)
    kd = []
    for l in lfull:
        l_hi, l_lo = split(l)
        kd.append(dot_nt(l_hi, c_hi) + dot_nt(l_hi, c_lo) + dot_nt(l_lo, c_hi))
    zero = jnp.zeros((LANES, LANES), F32)
    for s in range(CHUNK):
        row = jnp.concatenate([kd[t - s] if t >= s else zero for t in range(CHUNK)], axis=1)
        w_ref[s * LANES:(s + 1) * LANES, :] = row.astype(BF16)
        win_ref[s * LANES:(s + 1) * LANES, :] = lfull[CHUNK - 1 - s].astype(BF16)
    for t in range(CHUNK):
        e_re, e_im = pw[t + 1]
        cpt_ref[t * LANES:(t + 1) * LANES, :] = blockdiag(
            c_re * e_re - c_im * e_im, -c_re * e_im - c_im * e_re).astype(BF16)

    af = aflat_ref[...]
    ai = iflat_ref[...]
    dtf = jnp.exp(dtflat_ref[...])
    magf = jnp.exp(af * dtf)
    f_re, f_im = magf * jnp.cos(ai * dtf), magf * jnp.sin(ai * dtf)
    a1_ref[...] = jnp.concatenate([f_re, f_im], axis=1)
    for _ in range(CHUNK.bit_length() - 1):
        f_re, f_im = f_re * f_re - f_im * f_im, 2.0 * f_re * f_im
    a8_ref[...] = jnp.concatenate([f_re, f_im], axis=1)


def _ssm_prep(a_re, a_im, log_dt, b_re, b_im, c_re, c_im):
    rep = lambda t: jnp.repeat(t, SSM_GROUP, axis=0)
    are_x = rep(a_re)
    aim_x = rep(a_im)
    dt_x = rep(log_dt[:, None])
    bt_re = jnp.transpose(b_re, (0, 2, 1)).reshape(N_GROUPS * SSM_GROUP, SSM_STATE)
    bt_im = jnp.transpose(b_im, (0, 2, 1)).reshape(N_GROUPS * SSM_GROUP, SSM_STATE)
    cr = c_re.reshape(N_GROUPS * SSM_GROUP, SSM_STATE)
    ci = c_im.reshape(N_GROUPS * SSM_GROUP, SSM_STATE)
    aflat = a_re.reshape(N_GB, 1, GB_ST)
    iflat = a_im.reshape(N_GB, 1, GB_ST)
    dtflat = jnp.repeat(log_dt, SSM_STATE).reshape(N_GB, 1, GB_ST)
    rows = pl.BlockSpec((LANES, SSM_STATE), lambda g: (g, 0))
    flat = pl.BlockSpec((None, 1, GB_ST), lambda g: (g, 0, 0))
    big = pl.BlockSpec((None, GB_IN, GB_ST2), lambda g: (g, 0, 0))
    vec = pl.BlockSpec((None, 1, GB_ST2), lambda g: (g, 0, 0))
    return pl.pallas_call(
        _ssm_prep_kernel,
        grid=(N_GB,),
        in_specs=[rows, rows, pl.BlockSpec((LANES, 1), lambda g: (g, 0)),
                  rows, rows, rows, rows, flat, flat, flat],
        out_specs=[big, big, big,
                   pl.BlockSpec((None, LANES, GB_ST2), lambda g: (g, 0, 0)), vec, vec],
        out_shape=[
            jax.ShapeDtypeStruct((N_GB, GB_IN, GB_IN), BF16),
            jax.ShapeDtypeStruct((N_GB, GB_IN, GB_ST2), BF16),
            jax.ShapeDtypeStruct((N_GB, GB_IN, GB_ST2), BF16),
            jax.ShapeDtypeStruct((N_GB, LANES, GB_ST2), BF16),
            jax.ShapeDtypeStruct((N_GB, 1, GB_ST2), F32),
            jax.ShapeDtypeStruct((N_GB, 1, GB_ST2), F32),
        ],
        compiler_params=_cparams(("arbitrary",)),
        name="ssm_prep",
    )(are_x, aim_x, dt_x, bt_re, bt_im, cr, ci, aflat, iflat, dtflat)


SSM_RT = 256
SCAN_UNROLL = 8
SCAN_PAD = 8


def _ssm_core_kernel(uc_ref, w_ref, win_ref, cpt_ref, a8_ref, yc_ref, sre_ref, sim_ref, zs,
                     *, n_seq, n_chunk):
    n_rows = n_seq * n_chunk
    nlb = GB_ST2 // LANES
    half = nlb // 2
    pitch = n_chunk + SCAN_PAD

    def zrows(r):
        seq, off = divmod(r * SSM_RT, n_chunk)
        return slice(seq * pitch + off, seq * pitch + off + SSM_RT)

    for r in range(n_rows // SSM_RT):
        rs = slice(r * SSM_RT, (r + 1) * SSM_RT)
        z = jnp.dot(uc_ref[rs, :], win_ref[...], preferred_element_type=F32)
        for c in range(nlb):
            zs[c, zrows(r), :] = z[:, c * LANES:(c + 1) * LANES]

    a8 = a8_ref[...]
    a_bl = [jnp.broadcast_to(a8[:, c * LANES:(c + 1) * LANES], (n_seq, LANES))
            for c in range(nlb)]

    def step(k, carry):
        rows = pl.ds(k, n_seq, stride=pitch)
        new = []
        for c in range(half):
            s_re, s_im = carry[c], carry[half + c]
            z_re = zs[c, rows, :]
            z_im = zs[half + c, rows, :]
            zs[c, rows, :] = s_re
            zs[half + c, rows, :] = s_im
            new.append((a_bl[c] * s_re - a_bl[half + c] * s_im + z_re,
                        a_bl[c] * s_im + a_bl[half + c] * s_re + z_im))
        return tuple(n[0] for n in new) + tuple(n[1] for n in new)

    zero = jnp.zeros((n_seq, LANES), F32)
    fin = lax.fori_loop(0, n_chunk, step, (zero,) * nlb, unroll=SCAN_UNROLL)
    sre_ref[...] = jnp.concatenate(fin[:half], axis=1)
    sim_ref[...] = jnp.concatenate(fin[half:], axis=1)

    for r in range(n_rows // SSM_RT):
        rs = slice(r * SSM_RT, (r + 1) * SSM_RT)
        sp = jnp.concatenate([zs[c, zrows(r), :] for c in range(nlb)], axis=1).astype(BF16)
        y = lax.dot_general(sp, cpt_ref[...], (((1,), (1,)), ((), ())),
                            preferred_element_type=F32)
        for m in range(CHUNK // 2):
            kin = (2 * m + 2) * LANES
            oc = slice(2 * m * LANES, (2 * m + 2) * LANES)
            yc_ref[rs, oc] = y[:, oc] + jnp.dot(uc_ref[rs, :kin], w_ref[:kin, oc],
                                               preferred_element_type=F32)


def _ssm_core(uc, w, win, cpt, a8, n_seq):
    n_rows = uc.shape[0]
    n_chunk = n_rows // n_seq
    mat = lambda r: pl.BlockSpec((None, r, GB_ST2), lambda g: (g, 0, 0))
    blk = pl.BlockSpec((n_rows, GB_IN), lambda g: (0, g))
    st = pl.BlockSpec((n_seq, GB_ST), lambda g: (0, g))
    return pl.pallas_call(
        functools.partial(_ssm_core_kernel, n_seq=n_seq, n_chunk=n_chunk),
        grid=(N_GB,),
        in_specs=[blk, mat(GB_IN), mat(GB_IN), mat(GB_IN), mat(1)],
        out_specs=[blk, st, st],
        out_shape=[
            jax.ShapeDtypeStruct((n_rows, N_GB * GB_IN), F32),
            jax.ShapeDtypeStruct((n_seq, N_GROUPS * SSM_STATE), F32),
            jax.ShapeDtypeStruct((n_seq, N_GROUPS * SSM_STATE), F32),
        ],
        scratch_shapes=[pltpu.VMEM((GB_ST2 // LANES, n_seq * (n_chunk + SCAN_PAD), LANES), F32)],
        compiler_params=_cparams(("arbitrary",)),
        name="ssm_core",
    )(uc, w, win, cpt, a8)


def _glu_tail(x, y, g_ref, dsk_ref, wglu_ref):
    h = _rms(x, g_ref[...])
    z = jax.nn.gelu(y + dsk_ref[...] * h).astype(BF16)
    zz = jnp.dot(z, wglu_ref[...], preferred_element_type=F32)
    return x + zz[:, :D_MODEL] * jax.nn.sigmoid(zz[:, D_MODEL:])


def _ssm_sample_kernel(x_ref, g_ref, bb_ref, c0t_ref, a1_ref, hre_ref, him_ref,
                       y_ref, sre_ref, sim_ref):
    gb = pl.program_id(0)
    x = x_ref[...]
    ms = jnp.mean(x * x, axis=-1, keepdims=True)
    cols = pl.ds(pl.multiple_of(gb * LANES, LANES), LANES)
    u = (x_ref[:, cols] * lax.rsqrt(ms + RMS_EPS) * g_ref[:, cols]).astype(BF16)
    bu = jnp.dot(u, bb_ref[...], preferred_element_type=F32)
    a1 = a1_ref[...]
    a_re = a1[:, :GB_ST]
    a_im = a1[:, GB_ST:]
    h_re = hre_ref[...]
    h_im = him_ref[...]
    s_re = a_re * h_re - a_im * h_im + bu[:, :GB_ST]
    s_im = a_re * h_im + a_im * h_re + bu[:, GB_ST:]
    sre_ref[...] = s_re
    sim_ref[...] = s_im
    s = jnp.concatenate([s_re, s_im], axis=1).astype(BF16)
    y_ref[...] = lax.dot_general(s, c0t_ref[...], (((1,), (1,)), ((), ())),
                                 preferred_element_type=F32)


def _ssm_sample(x, gain, win, c0t, a1, h_re, h_im):
    n, d = x.shape
    st = pl.BlockSpec((n, GB_ST), lambda g: (0, g))
    return pl.pallas_call(
        _ssm_sample_kernel,
        grid=(N_GB,),
        in_specs=[
            _const_spec((n, d)),
            _const_spec((1, d)),
            pl.BlockSpec((None, LANES, GB_ST2), lambda g: (g, CHUNK - 1, 0)),
            pl.BlockSpec((None, LANES, GB_ST2), lambda g: (g, 0, 0)),
            pl.BlockSpec((None, 1, GB_ST2), lambda g: (g, 0, 0)),
            st, st,
        ],
        out_specs=[pl.BlockSpec((n, LANES), lambda g: (0, g)), st, st],
        out_shape=[
            jax.ShapeDtypeStruct((n, d), F32),
            jax.ShapeDtypeStruct((n, N_GROUPS * SSM_STATE), F32),
            jax.ShapeDtypeStruct((n, N_GROUPS * SSM_STATE), F32),
        ],
        compiler_params=_cparams(("arbitrary",)),
        name="ssm_sample",
    )(x, gain, win, c0t, a1, h_re, h_im)


def _glu_sample_kernel(x_ref, y_ref, g_ref, dsk_ref, wglu_ref, xo_ref):
    xo_ref[...] = _glu_tail(x_ref[...], y_ref[...], g_ref, dsk_ref, wglu_ref)


def _glu_sample(x, y, gain, dsk, wglu, layer):
    n, d = x.shape
    return pl.pallas_call(
        _glu_sample_kernel,
        grid=(1,),
        in_specs=[_const_spec((n, d)), _const_spec((n, d)), _const_spec((1, d)),
                  _const_spec((1, d)), pl.BlockSpec((None, d, 2 * d), lambda i: (layer, 0, 0))],
        out_specs=_const_spec((n, d)),
        out_shape=jax.ShapeDtypeStruct((n, d), F32),
        compiler_params=_cparams(("arbitrary",)),
        name="glu_sample",
    )(x, y, gain, dsk, wglu)


def kernel(x_prompt, x_sample, cache_k, cache_v, state_ssm_re, state_ssm_im, state_conv,
           norm_mix, norm_ffn, w_qkv, w_o, q_norm, k_norm, sinks, ssm_a_re, ssm_a_im,
           ssm_log_dt, ssm_b_re, ssm_b_im, ssm_c_re, ssm_c_im, ssm_d, w_glu, w_up, conv_w,
           conv_b, w_down):
    depth = norm_mix.shape[0]
    nb, _, _ = x_prompt.shape
    ns = x_sample.shape[0]
    xp = x_prompt
    xs = x_sample.reshape(ns, D_MODEL)
    row = lambda t: t.reshape(1, -1)

    wqkv = w_qkv.astype(BF16)
    wo = w_o.astype(BF16)
    wo_t = jnp.swapaxes(wo, 1, 2)
    wglu = w_glu.astype(BF16)
    wup = w_up.astype(BF16)
    wdn = w_down.astype(BF16)
    cb = conv_b.reshape(depth, 1, F2)
    n_att = wqkv.shape[0]
    wq_g = wqkv[:, :, :D_MODEL].reshape(n_att, D_MODEL, N_KV, GQA, HEAD_DIM)
    wq_g = jnp.transpose(wq_g, (0, 3, 1, 2, 4)).reshape(n_att, GQA, D_MODEL, KV_DIM)
    wo_g = wo.reshape(n_att, N_KV, GQA, HEAD_DIM, D_MODEL)
    wo_g = jnp.transpose(wo_g, (0, 2, 1, 3, 4)).reshape(n_att, GQA, KV_DIM, D_MODEL)
    ck = cache_k.reshape(n_att, ns, WINDOW, KV_DIM)
    cv = cache_v.reshape(n_att, ns, WINDOW, KV_DIM)

    kps, vps = [], []
    srp, sip, srs, sis = [], [], [], []
    cps = []
    conv_s = state_conv
    glu = None
    for i in range(depth):
        j = i // 2
        gm = row(norm_mix[i])
        if i % 2 == 0:
            qg, kg = row(q_norm[j]), row(k_norm[j])
            xp, kp, vp = _attn_prompt(xp, gm, wqkv, wo_t, jnp.tile(qg, (1, N_HEADS)),
                                      jnp.tile(kg, (1, N_KV)), sinks[j], j)
            xs, ck, cv = _attn_sample(xs, gm, wq_g, wqkv, wo_g, qg, kg, sinks[j], ck, cv, j)
            kps.append(kp.reshape(nb, WINDOW, N_KV, HEAD_DIM))
            vps.append(vp.reshape(nb, WINDOW, N_KV, HEAD_DIM))
        else:
            w, win, cpt, c0t, a8, a1 = _ssm_prep(
                ssm_a_re[j], ssm_a_im[j], ssm_log_dt[j], ssm_b_re[j], ssm_b_im[j],
                ssm_c_re[j], ssm_c_im[j])
            dsk = row(ssm_d[j])
            yc, s_re, s_im = _ssm_core(uc, w, win, cpt, a8, nb)
            glu = (yc, gm, dsk, wglu, j)
            srp.append(s_re.reshape(nb, N_GROUPS, SSM_STATE))
            sip.append(s_im.reshape(nb, N_GROUPS, SSM_STATE))
            ys, t_re, t_im = _ssm_sample(
                xs, gm, win, c0t, a1,
                state_ssm_re[j].reshape(ns, N_GROUPS * SSM_STATE),
                state_ssm_im[j].reshape(ns, N_GROUPS * SSM_STATE))
            xs = _glu_sample(xs, ys, gm, dsk, wglu, j)
            srs.append(t_re.reshape(ns, N_GROUPS, SSM_STATE))
            sis.append(t_im.reshape(ns, N_GROUPS, SSM_STATE))

        gf = row(norm_ffn[i])
        next_gain = row(norm_mix[i + 1]) if (i % 2 == 0 and i + 1 < depth) else None
        res = _ffn_prompt(xp, gf, wup, conv_w, cb, wdn, i, glu=glu, next_gain=next_gain)
        glu = None
        xp, cp = res[0], res[1]
        if next_gain is not None:
            uc = res[2]
        xs, conv_s = _ffn_sample(xs, gf, wup, conv_w, cb, wdn, conv_s, i)
        cps.append(cp)

    k_s, v_s = (t.reshape(n_att, ns, WINDOW, N_KV, HEAD_DIM) for t in (ck, cv))
    return (xp, xs.reshape(ns, 1, D_MODEL),
            jnp.stack(kps), jnp.stack(vps), k_s, v_s,
            jnp.stack(srp), jnp.stack(sip), jnp.stack(srs), jnp.stack(sis),
            jnp.stack(cps), conv_s)
```

```python
import functools

import jax
import jax.numpy as jnp
from jax import lax
from jax.experimental import pallas as pl
from jax.experimental.pallas import tpu as pltpu

F32 = jnp.float32
BF16 = jnp.bfloat16

D_MODEL = 1024
HEAD_DIM = 64
N_HEADS = 16
N_KV = 4
GQA = 4
WINDOW = 128
KV_DIM = N_KV * HEAD_DIM
QKV_DIM = D_MODEL + 2 * KV_DIM
D_FF = 2816
F2 = 2 * D_FF
FF_BLK = 256
N_FF_BLK = D_FF // FF_BLK
SSM_GROUP = 16
N_GROUPS = 64
SSM_STATE = 64
RMS_EPS = 1e-6

LANES = 128
CHUNK = 8
GB_GROUPS = LANES // SSM_GROUP
N_GB = N_GROUPS // GB_GROUPS
GB_IN = CHUNK * LANES
GB_ST = GB_GROUPS * SSM_STATE
GB_ST2 = 2 * GB_ST

VMEM_LIMIT = 56 * 1024 * 1024


def _cparams(sem):
    return pltpu.CompilerParams(dimension_semantics=sem, vmem_limit_bytes=VMEM_LIMIT)


def _rms(x, g):
    ms = jnp.mean(x * x, axis=-1, keepdims=True)
    return x * lax.rsqrt(ms + RMS_EPS) * g


def _head_rms(t, n, g):
    return jnp.concatenate(
        [_rms(t[:, j * HEAD_DIM:(j + 1) * HEAD_DIM], g) for j in range(n)], axis=1)


def _const_spec(shape):
    nd = len(shape)
    return pl.BlockSpec(shape, lambda *_: (0,) * nd)


ATT_TQ = 512
ATT_NBLK = ATT_TQ // WINDOW
SEG_W = 256
LOG2E = 1.4426950408889634


def _seg_mean_sq(t, seg):
    sq = t * t
    hi = sq.astype(BF16)
    lo = (sq - hi.astype(F32)).astype(BF16)
    out = []
    for c in range(t.shape[1] // SEG_W):
        sl = slice(c * SEG_W, (c + 1) * SEG_W)
        out.append(jnp.dot(hi[:, sl], seg, preferred_element_type=F32)
                   + jnp.dot(lo[:, sl], seg, preferred_element_type=F32))
    return jnp.concatenate(out, axis=1) * (1.0 / HEAD_DIM)


VT_ROWS = HEAD_DIM + 16


def _attn_prompt_kernel(sink_ref, x_ref, g_ref, wqkv_ref, wot_ref, qg_ref, kg_ref,
                        xo_ref, kl_ref, vl_ref, kbuf, krol, vt, lo_scr, ot):
    i = pl.program_id(1)
    keys = 2 * WINDOW

    @pl.when(i == 0)
    def _():
        kbuf[0:WINDOW, :] = jnp.zeros((WINDOW, KV_DIM), BF16)
        krol[0:WINDOW, :] = jnp.zeros((WINDOW, KV_DIM), BF16)
        vt[:, 0:HEAD_DIM, 0:WINDOW] = jnp.zeros((N_KV, HEAD_DIM, WINDOW), BF16)
        vt[:, HEAD_DIM:, :] = jnp.ones((N_KV, VT_ROWS - HEAD_DIM, WINDOW + ATT_TQ), BF16)
        lo_scr[...] = jnp.full(lo_scr.shape, WINDOW, jnp.int32)

    @pl.when(i == 1)
    def _():
        lo_scr[...] = jnp.zeros(lo_scr.shape, jnp.int32)

    si = lax.broadcasted_iota(jnp.int32, (SEG_W, SEG_W), 0) // HEAD_DIM
    sj = lax.broadcasted_iota(jnp.int32, (SEG_W, SEG_W), 1) // HEAD_DIM
    seg = jnp.where(si == sj, 1.0, 0.0).astype(BF16)

    x = x_ref[...]
    h = _rms(x, g_ref[...]).astype(BF16)
    qkv = jnp.dot(h, wqkv_ref[...], preferred_element_type=F32)
    q = qkv[:, :D_MODEL]
    k = qkv[:, D_MODEL:D_MODEL + KV_DIM]
    v = qkv[:, D_MODEL + KV_DIM:]
    qn = q * lax.rsqrt(_seg_mean_sq(q, seg) + RMS_EPS) * qg_ref[...] * (HEAD_DIM ** -0.5 * LOG2E)
    kn = k * lax.rsqrt(_seg_mean_sq(k, seg) + RMS_EPS) * kg_ref[...]
    kl_ref[...] = kn[ATT_TQ - WINDOW:]
    vl_ref[...] = v[ATT_TQ - WINDOW:]
    kbuf[WINDOW:, :] = kn.astype(BF16)
    for c in range(KV_DIM // LANES):
        sl = slice(c * LANES, (c + 1) * LANES)
        krol[WINDOW:, sl] = pltpu.roll(kn[:, sl], HEAD_DIM, 1).astype(BF16)
    v_t = v.T
    for kvh in range(N_KV):
        vt[kvh, 0:HEAD_DIM, WINDOW:] = v_t[kvh * HEAD_DIM:(kvh + 1) * HEAD_DIM].astype(BF16)

    kc = lax.broadcasted_iota(jnp.int32, (keys, 2 * WINDOW), 0)
    qi = lax.broadcasted_iota(jnp.int32, (keys, 2 * WINDOW), 1) % WINDOW
    band = (kc > qi) & (kc <= qi + WINDOW)
    low = lax.broadcasted_iota(jnp.int32, (WINDOW, LANES), 1) < HEAD_DIM
    nt = (((1,), (1,)), ((), ()))

    for blk in range(ATT_NBLK):
        r0 = blk * WINDOW
        valid = band & (kc >= lo_scr[...]) if blk == 0 else band
        combos = []
        for kvh in range(N_KV):
            khalf = kvh % 2
            combos.append((kvh, (khalf, khalf + 2), kbuf))
            combos.append((kvh, (1 - khalf, 3 - khalf), krol))
        scores = []
        for kvh, pair, kref in combos:
            kcol = slice((kvh // 2) * LANES, (kvh // 2 + 1) * LANES)
            qm = []
            for g in pair:
                qcol = slice((kvh * 2 + g // 2) * LANES, (kvh * 2 + g // 2 + 1) * LANES)
                qm.append(jnp.where(low if g % 2 == 0 else ~low, qn[r0:r0 + WINDOW, qcol], 0.0))
            scores.append(lax.dot_general(kref[r0:r0 + keys, kcol],
                                          jnp.concatenate(qm, axis=0).astype(BF16), nt,
                                          preferred_element_type=F32))
        probs = []
        for (kvh, pair, _), s in zip(combos, scores):
            sink_row = jnp.concatenate(
                [jnp.full((1, WINDOW), sink_ref[kvh * GQA + g] * LOG2E, F32) for g in pair],
                axis=1)
            s = jnp.where(valid, s, -jnp.inf)
            m = jnp.maximum(jnp.max(s, axis=0, keepdims=True), sink_row)
            probs.append((jnp.exp2(s - m).astype(BF16), jnp.exp2(sink_row - m)))
        outs = [jnp.dot(vt[kvh, :, r0:r0 + keys], e, preferred_element_type=F32)
                for (kvh, _, _), (e, _) in zip(combos, probs)]
        for (kvh, pair, _), (_, e_sink), o in zip(combos, probs, outs):
            rden = 1.0 / (o[HEAD_DIM:HEAD_DIM + 1] + e_sink)
            on = (o[:HEAD_DIM] * rden).astype(BF16)
            for n, g in enumerate(pair):
                hd = kvh * GQA + g
                ot[hd * HEAD_DIM:(hd + 1) * HEAD_DIM, r0:r0 + WINDOW] = (
                    on[:, n * WINDOW:(n + 1) * WINDOW])

    kbuf[0:WINDOW, :] = kbuf[ATT_TQ:, :]
    krol[0:WINDOW, :] = krol[ATT_TQ:, :]
    vt[:, 0:HEAD_DIM, 0:WINDOW] = vt[:, 0:HEAD_DIM, ATT_TQ:]
    out_t = jnp.dot(wot_ref[...], ot[...], preferred_element_type=F32)
    xo_ref[...] = x + out_t.T


def _attn_prompt(x, gain, wqkv, wo_t, qg, kg, sinks, layer):
    b, l, d = x.shape
    wsel = lambda bi, i: (layer, 0, 0)
    tok = pl.BlockSpec((None, ATT_TQ, d), lambda bi, i: (bi, i, 0))
    last = pl.BlockSpec((None, WINDOW, KV_DIM), lambda bi, i: (bi, 0, 0))
    kvbuf = pltpu.VMEM((WINDOW + ATT_TQ, KV_DIM), BF16)
    return pl.pallas_call(
        _attn_prompt_kernel,
        grid=(b, l // ATT_TQ),
        in_specs=[
            pl.BlockSpec(memory_space=pltpu.SMEM),
            tok,
            _const_spec((1, d)),
            pl.BlockSpec((None, d, QKV_DIM), wsel),
            pl.BlockSpec((None, d, d), wsel),
            _const_spec((1, d)),
            _const_spec((1, KV_DIM)),
        ],
        out_specs=[tok, last, last],
        out_shape=[
            jax.ShapeDtypeStruct((b, l, d), F32),
            jax.ShapeDtypeStruct((b, WINDOW, KV_DIM), F32),
            jax.ShapeDtypeStruct((b, WINDOW, KV_DIM), F32),
        ],
        scratch_shapes=[
            kvbuf, kvbuf,
            pltpu.VMEM((N_KV, VT_ROWS, WINDOW + ATT_TQ), BF16),
            pltpu.VMEM((2 * WINDOW, 2 * WINDOW), jnp.int32),
            pltpu.VMEM((d, ATT_TQ), BF16),
        ],
        compiler_params=_cparams(("arbitrary", "arbitrary")),
        name="attn_prompt",
    )(sinks, x, gain, wqkv, wo_t, qg, kg)


ATS_TB = 8


def _attn_sample_kernel(sink_ref, x_ref, g_ref, wq_ref, wkv_ref, wo_ref, qg_ref, kg_ref,
                        ck_ref, cv_ref, xo_ref, ko_ref, vo_ref, r_scr):
    x = x_ref[...]
    h = _rms(x, g_ref[...]).astype(BF16)
    kv = jnp.dot(h, wkv_ref[...], preferred_element_type=F32)
    kn = _head_rms(kv[:, :KV_DIM], N_KV, kg_ref[...])
    v = kv[:, KV_DIM:]
    qg = qg_ref[...]
    q_g = [_head_rms(jnp.dot(h, wq_ref[g], preferred_element_type=F32), N_KV, qg)
           * (HEAD_DIM ** -0.5) for g in range(GQA)]

    rows = GQA * N_KV
    rkv = lax.broadcasted_iota(jnp.int32, (rows, KV_DIM), 0) % N_KV
    lkv = lax.broadcasted_iota(jnp.int32, (rows, KV_DIM), 1) // HEAD_DIM
    own = rkv == lkv
    colj = lax.broadcasted_iota(jnp.int32, (rows, WINDOW), 1)
    sink_col = jnp.concatenate(
        [jnp.full((1, 1), sink_ref[(r % N_KV) * GQA + r // N_KV], F32) for r in range(rows)],
        axis=0)

    for n in range(ATS_TB):
        ko_ref[n, 0:WINDOW - 1, :] = ck_ref[n, 1:WINDOW, :]
        ko_ref[n, WINDOW - 1:WINDOW, :] = kn[n:n + 1]
        vo_ref[n, 0:WINDOW - 1, :] = cv_ref[n, 1:WINDOW, :]
        vo_ref[n, WINDOW - 1:WINDOW, :] = v[n:n + 1]
    qbds = []
    for n in range(ATS_TB):
        qbd = jnp.concatenate(
            [jnp.broadcast_to(q_g[g][n:n + 1], (N_KV, KV_DIM)) for g in range(GQA)], axis=0)
        qbds.append(jnp.where(own, qbd, 0.0))
    scores = [lax.dot_general(qbds[n].astype(BF16), ck_ref[n].astype(BF16),
                              (((1,), (1,)), ((), ())), preferred_element_type=F32)
              for n in range(ATS_TB)]
    probs = []
    for n in range(ATS_TB):
        s = jnp.where(colj >= 1, scores[n], -jnp.inf)
        s_new = jnp.sum(qbds[n] * kn[n:n + 1], axis=-1, keepdims=True)
        m = jnp.maximum(jnp.maximum(jnp.max(s, axis=-1, keepdims=True), s_new), sink_col)
        e = jnp.exp(s - m)
        e_new = jnp.exp(s_new - m)
        rden = 1.0 / (jnp.sum(e, axis=-1, keepdims=True) + e_new + jnp.exp(sink_col - m))
        probs.append(((e * rden).astype(BF16), e_new * rden))
    outs = [jnp.dot(probs[n][0], cv_ref[n].astype(BF16), preferred_element_type=F32)
            for n in range(ATS_TB)]
    for n in range(ATS_TB):
        o = jnp.where(own, outs[n] + probs[n][1] * v[n:n + 1], 0.0)
        for g in range(GQA):
            r_scr[g, n:n + 1, :] = jnp.sum(o[g * N_KV:(g + 1) * N_KV], axis=0, keepdims=True)

    out = x
    for g in range(GQA):
        out = out + jnp.dot(r_scr[g].astype(BF16), wo_ref[g], preferred_element_type=F32)
    xo_ref[...] = out


def _attn_sample(x, gain, wq_g, wqkv, wo_g, qg, kg, sinks, cache_k, cache_v, layer):
    n, d = x.shape
    tok = pl.BlockSpec((ATS_TB, d), lambda i: (i, 0))
    cache = pl.BlockSpec((None, ATS_TB, WINDOW, KV_DIM), lambda i: (layer, i, 0, 0))
    in_specs = [
        pl.BlockSpec(memory_space=pltpu.SMEM),
        tok,
        _const_spec((1, d)),
        pl.BlockSpec((None, GQA, d, KV_DIM), lambda i: (layer, 0, 0, 0)),
        pl.BlockSpec((None, d, 2 * KV_DIM), lambda i: (layer, 0, D_MODEL // (2 * KV_DIM))),
        pl.BlockSpec((None, GQA, KV_DIM, d), lambda i: (layer, 0, 0, 0)),
        _const_spec((1, HEAD_DIM)),
        _const_spec((1, HEAD_DIM)),
        cache,
        cache,
    ]
    args = [sinks, x, gain, wq_g, wqkv, wo_g, qg, kg, cache_k, cache_v]
    return pl.pallas_call(
        _attn_sample_kernel,
        grid=(n // ATS_TB,),
        in_specs=in_specs,
        out_specs=[tok, cache, cache],
        out_shape=[
            jax.ShapeDtypeStruct((n, d), F32),
            jax.ShapeDtypeStruct(cache_k.shape, F32),
            jax.ShapeDtypeStruct(cache_v.shape, F32),
        ],
        scratch_shapes=[pltpu.VMEM((GQA, ATS_TB, KV_DIM), F32)],
        input_output_aliases={len(args) - 2: 1, len(args) - 1: 2},
        compiler_params=_cparams(("arbitrary",)),
        name="attn_sample",
    )(*args)


FFN_TM = 512
SSM_TC = FFN_TM // CHUNK
CARRY = 8
CONV_RC = 64


def _ffn_prompt_kernel(*refs, glu_in, uc_out):
    refs = list(refs)
    x_ref = refs.pop(0)
    if glu_in:
        yc_ref, gm_ref, dsk_ref, wglu_ref = (refs.pop(0) for _ in range(4))
    g_ref, wup_ref, cw_ref, cb_ref, wdn_ref = (refs.pop(0) for _ in range(5))
    if uc_out:
        gn_ref = refs.pop(0)
    xo_ref, cs_ref = refs.pop(0), refs.pop(0)
    if uc_out:
        uc_ref = refs.pop(0)
    carry, upbuf, act, hs, oscr = refs

    i = pl.program_id(1)
    hm = FFN_TM // 2
    nlb = FF_BLK // LANES
    dlb = D_MODEL // LANES

    @pl.when(i == 0)
    def _():
        carry[...] = jnp.zeros_like(carry)

    if glu_in:
        for gb in range(N_GB):
            for t in range(CHUNK):
                oscr[gb, pl.ds(t, SSM_TC, stride=CHUNK), :] = (
                    yc_ref[:, gb * GB_IN + t * LANES:gb * GB_IN + (t + 1) * LANES])
        y = jnp.concatenate([oscr[gb] for gb in range(dlb)], axis=1)
        xo_ref[...] = _glu_tail(x_ref[...], y, gm_ref, dsk_ref, wglu_ref)
        xin_ref = xo_ref
    else:
        xin_ref = x_ref

    hs[...] = _rms(xin_ref[...], g_ref[...]).astype(BF16)

    def up_into(j, slot):
        for half, c0 in enumerate((j * FF_BLK, D_FF + j * FF_BLK)):
            cols = pl.ds(c0, FF_BLK)
            up = jnp.dot(hs[...], wup_ref[:, cols], preferred_element_type=F32)
            for lb in range(nlb):
                lcols = pl.ds(c0 + lb * LANES, LANES)
                buf = upbuf.at[slot, half, lb]
                buf[0:CARRY, :] = carry[:, lcols]
                buf[CARRY:, :] = up[:, lb * LANES:(lb + 1) * LANES]
                carry[:, lcols] = buf[FFN_TM:, :]

    def conv(slot, half, lb, c0, m0):
        lcols = pl.ds(c0 + lb * LANES, LANES)
        buf = upbuf.at[slot, half, lb]
        tap = lambda off: buf[pl.ds(CARRY + off + 2 * m0, CONV_RC, stride=2), :]
        em2, om1, e0, o1 = tap(-2), tap(-1), tap(0), tap(1)
        w0, w1, w2 = cw_ref[0:1, lcols], cw_ref[1:2, lcols], cw_ref[2:3, lcols]
        cb = cb_ref[:, lcols]
        return (cb + w0 * em2 + w1 * om1 + w2 * e0, cb + w0 * om1 + w1 * e0 + w2 * o1)

    for j in range(N_FF_BLK):
        up_into(j, j % 2)
        for m0 in range(0, hm, CONV_RC):
            for lb in range(nlb):
                cg = conv(j % 2, 0, lb, j * FF_BLK, m0)
                cv = conv(j % 2, 1, lb, D_FF + j * FF_BLK, m0)
                acol = pl.ds(j * FF_BLK + lb * LANES, LANES)
                for p in range(2):
                    act[p * hm + m0:p * hm + m0 + CONV_RC, acol] = (
                        jax.nn.silu(cg[p]) * cv[p]).astype(BF16)

    out = jnp.dot(act[...], wdn_ref[...], preferred_element_type=F32)
    for lb in range(dlb):
        for p in range(2):
            oscr[lb, pl.ds(p, hm, stride=2), :] = out[p * hm:(p + 1) * hm,
                                                      lb * LANES:(lb + 1) * LANES]
    xo = xin_ref[...] + jnp.concatenate([oscr[lb] for lb in range(dlb)], axis=1)
    xo_ref[...] = xo
    cs_ref[...] = carry[CARRY - 2:CARRY, :]

    if uc_out:
        h2 = _rms(xo, gn_ref[...])
        for gb in range(N_GB):
            oscr[gb] = h2[:, gb * LANES:(gb + 1) * LANES]
        for gb in range(N_GB):
            for t in range(CHUNK):
                uc_ref[:, gb * GB_IN + t * LANES:gb * GB_IN + (t + 1) * LANES] = (
                    oscr[gb, pl.ds(t, SSM_TC, stride=CHUNK), :].astype(BF16))


def _ffn_prompt(x, gain, wup, cw, cb, wdn, layer, glu=None, next_gain=None):
    b, l, d = x.shape
    nt = l // FFN_TM
    tok = pl.BlockSpec((None, FFN_TM, d), lambda bi, i: (bi, i, 0))
    chunks = pl.BlockSpec((SSM_TC, CHUNK * d), lambda bi, i: (bi * nt + i, 0))
    wsel = lambda bi, i: (layer, 0, 0)
    args, in_specs = [x], [tok]
    if glu is not None:
        yc, gm, dsk, wglu, mixer_layer = glu
        args += [yc, gm, dsk, wglu]
        in_specs += [chunks, _const_spec((1, d)), _const_spec((1, d)),
                     pl.BlockSpec((None, d, 2 * d), lambda bi, i: (mixer_layer, 0, 0),
                                  pipeline_mode=pl.Buffered(1))]
    args += [gain, wup, cw, cb, wdn]
    in_specs += [
        _const_spec((1, d)),
        pl.BlockSpec((None, d, F2), wsel, pipeline_mode=pl.Buffered(1)),
        pl.BlockSpec((None, 3, F2), wsel),
        pl.BlockSpec((None, 1, F2), wsel),
        pl.BlockSpec((None, D_FF, d), wsel, pipeline_mode=pl.Buffered(1)),
    ]
    out_specs = [tok, pl.BlockSpec((None, 2, F2), lambda bi, i: (bi, 0, 0))]
    out_shape = [jax.ShapeDtypeStruct((b, l, d), F32), jax.ShapeDtypeStruct((b, 2, F2), F32)]
    if next_gain is not None:
        args.append(next_gain)
        in_specs.append(_const_spec((1, d)))
        out_specs.append(chunks)
        out_shape.append(jax.ShapeDtypeStruct((b * l // CHUNK, CHUNK * d), BF16))
    return pl.pallas_call(
        functools.partial(_ffn_prompt_kernel, glu_in=glu is not None,
                          uc_out=next_gain is not None),
        grid=(b, nt),
        in_specs=in_specs,
        out_specs=out_specs,
        out_shape=out_shape,
        scratch_shapes=[
            pltpu.VMEM((CARRY, F2), F32),
            pltpu.VMEM((2, 2, FF_BLK // LANES, CARRY + FFN_TM, LANES), F32),
            pltpu.VMEM((FFN_TM, D_FF), BF16),
            pltpu.VMEM((FFN_TM, d), BF16),
            pltpu.VMEM((d // LANES, FFN_TM, LANES), F32),
        ],
        compiler_params=_cparams(("arbitrary", "arbitrary")),
        name="ffn_prompt",
    )(*args)


FFS_BLK = D_FF // 2
N_FFS_BLK = D_FF // FFS_BLK


def _ffn_sample_kernel(x_ref, g_ref, w_ref, cw_ref, cb_ref, wdn_ref, sc_ref,
                       xo_ref, cs_ref, gate):
    j = pl.program_id(0)

    @pl.when(j == 0)
    def _():
        xo_ref[...] = x_ref[...]

    h = _rms(x_ref[...], g_ref[...]).astype(BF16)
    up = jnp.dot(h, w_ref[...], preferred_element_type=F32)
    b0 = sc_ref[:, 0, :]
    b1 = sc_ref[:, 1, :]
    c = cb_ref[...]
    c = c + cw_ref[0:1, :] * b0
    c = c + cw_ref[1:2, :] * b1
    c = c + cw_ref[2:3, :] * up
    cs_ref[...] = jnp.stack([b1, up], axis=1)

    @pl.when(j < N_FFS_BLK)
    def _():
        gate[j] = jax.nn.silu(c)

    @pl.when(j >= N_FFS_BLK)
    def _():
        a = (gate[j - N_FFS_BLK] * c).astype(BF16)
        xo_ref[...] += jnp.dot(a, wdn_ref[...], preferred_element_type=F32)


def _ffn_sample(x, gain, wup, cw, cb, wdn, state, layer):
    n, d = x.shape
    nb = N_FFS_BLK
    sblk = pl.BlockSpec((None, n, 2, FFS_BLK), lambda j: (layer, 0, 0, j))
    in_specs = [
        _const_spec((n, d)),
        _const_spec((1, d)),
        pl.BlockSpec((None, d, FFS_BLK), lambda j: (layer, 0, j)),
        pl.BlockSpec((None, 3, FFS_BLK), lambda j: (layer, 0, j)),
        pl.BlockSpec((None, 1, FFS_BLK), lambda j: (layer, 0, j)),
        pl.BlockSpec((None, FFS_BLK, d), lambda j: (layer, jnp.maximum(j - nb, 0), 0)),
        sblk,
    ]
    args = [x, gain, wup, cw, cb, wdn, state]
    return pl.pallas_call(
        _ffn_sample_kernel,
        grid=(2 * nb,),
        in_specs=in_specs,
        out_specs=[_const_spec((n, d)), sblk],
        out_shape=[
            jax.ShapeDtypeStruct((n, d), F32),
            jax.ShapeDtypeStruct(state.shape, F32),
        ],
        scratch_shapes=[pltpu.VMEM((nb, n, FFS_BLK), F32)],
        input_output_aliases={len(args) - 1: 1},
        compiler_params=_cparams(("arbitrary",)),
        name="ffn_sample",
    )(*args)


def _tile_states(t):
    t2 = jnp.concatenate([t, t], axis=1)
    return jnp.concatenate([t2] * (GB_GROUPS // 2), axis=1)


def _ssm_prep_kernel(are_ref, aim_ref, dt_ref, bre_ref, bim_ref, cre_ref, cim_ref,
                     aflat_ref, iflat_ref, dtflat_ref,
                     w_ref, win_ref, cpt_ref, c0t_ref, a8_ref, a1_ref):
    a_re = are_ref[...]
    a_im = aim_ref[...]
    dt = jnp.exp(dt_ref[...])

    mag = jnp.exp(a_re * dt)
    ab_re, ab_im = mag * jnp.cos(a_im * dt), mag * jnp.sin(a_im * dt)
    pw = [(jnp.ones_like(ab_re), jnp.zeros_like(ab_re)), (ab_re, ab_im)]
    for _ in range(CHUNK - 1):
        p_re, p_im = pw[-1]
        pw.append((p_re * ab_re - p_im * ab_im, p_re * ab_im + p_im * ab_re))
    den = a_re * a_re + a_im * a_im
    z_re = ((ab_re - 1.0) * a_re + ab_im * a_im) / den
    z_im = (ab_im * a_re - (ab_re - 1.0) * a_im) / den
    b_re = bre_ref[...]
    b_im = bim_ref[...]
    bb_re = z_re * b_re - z_im * b_im
    bb_im = z_re * b_im + z_im * b_re
    c_re = cre_ref[...]
    c_im = cim_ref[...]

    rg = lax.broadcasted_iota(jnp.int32, (LANES, GB_ST2), 0) // SSM_GROUP
    lg = (lax.broadcasted_iota(jnp.int32, (LANES, GB_ST2), 1) % GB_ST) // SSM_STATE
    own = rg == lg

    def blockdiag(t_re, t_im):
        full = jnp.concatenate([_tile_states(t_re), _tile_states(t_im)], axis=1)
        return jnp.where(own, full, 0.0)

    c0t = blockdiag(c_re, -c_im)
    c0t_ref[...] = c0t.astype(BF16)

    lfull = []
    for d in range(CHUNK):
        if d == 0:
            l_re, l_im = bb_re, bb_im
        else:
            e_re, e_im = pw[d]
            l_re = e_re * bb_re - e_im * bb_im
            l_im = e_re * bb_im + e_im * bb_re
        lfull.append(blockdiag(l_re, l_im))

    def split(t):
        hi = t.astype(BF16)
        return hi, (t - hi.astype(F32)).astype(BF16)

    def dot_nt(a, b):
        return lax.dot_general(a, b, (((1,), (1,)), ((), ())), preferred_element_type=F32)

    c_hi, c_lo = split(c0t)
    kd = []
    for l in lfull:
        l_hi, l_lo = split(l)
        kd.append(dot_nt(l_hi, c_hi) + dot_nt(l_hi, c_lo) + dot_nt(l_lo, c_hi))
    zero = jnp.zeros((LANES, LANES), F32)
    for s in range(CHUNK):
        row = jnp.concatenate([kd[t - s] if t >= s else zero for t in range(CHUNK)], axis=1)
        w_ref[s * LANES:(s + 1) * LANES, :] = row.astype(BF16)
        win_ref[s * LANES:(s + 1) * LANES, :] = lfull[CHUNK - 1 - s].astype(BF16)
    for t in range(CHUNK):
        e_re, e_im = pw[t + 1]
        cpt_ref[t * LANES:(t + 1) * LANES, :] = blockdiag(
            c_re * e_re - c_im * e_im, -c_re * e_im - c_im * e_re).astype(BF16)

    af = aflat_ref[...]
    ai = iflat_ref[...]
    dtf = jnp.exp(dtflat_ref[...])
    magf = jnp.exp(af * dtf)
    f_re, f_im = magf * jnp.cos(ai * dtf), magf * jnp.sin(ai * dtf)
    a1_ref[...] = jnp.concatenate([f_re, f_im], axis=1)
    for _ in range(CHUNK.bit_length() - 1):
        f_re, f_im = f_re * f_re - f_im * f_im, 2.0 * f_re * f_im
    a8_ref[...] = jnp.concatenate([f_re, f_im], axis=1)


def _ssm_prep(a_re, a_im, log_dt, b_re, b_im, c_re, c_im):
    rep = lambda t: jnp.repeat(t, SSM_GROUP, axis=0)
    are_x = rep(a_re)
    aim_x = rep(a_im)
    dt_x = rep(log_dt[:, None])
    bt_re = jnp.transpose(b_re, (0, 2, 1)).reshape(N_GROUPS * SSM_GROUP, SSM_STATE)
    bt_im = jnp.transpose(b_im, (0, 2, 1)).reshape(N_GROUPS * SSM_GROUP, SSM_STATE)
    cr = c_re.reshape(N_GROUPS * SSM_GROUP, SSM_STATE)
    ci = c_im.reshape(N_GROUPS * SSM_GROUP, SSM_STATE)
    aflat = a_re.reshape(N_GB, 1, GB_ST)
    iflat = a_im.reshape(N_GB, 1, GB_ST)
    dtflat = jnp.repeat(log_dt, SSM_STATE).reshape(N_GB, 1, GB_ST)
    rows = pl.BlockSpec((LANES, SSM_STATE), lambda g: (g, 0))
    flat = pl.BlockSpec((None, 1, GB_ST), lambda g: (g, 0, 0))
    big = pl.BlockSpec((None, GB_IN, GB_ST2), lambda g: (g, 0, 0))
    vec = pl.BlockSpec((None, 1, GB_ST2), lambda g: (g, 0, 0))
    return pl.pallas_call(
        _ssm_prep_kernel,
        grid=(N_GB,),
        in_specs=[rows, rows, pl.BlockSpec((LANES, 1), lambda g: (g, 0)),
                  rows, rows, rows, rows, flat, flat, flat],
        out_specs=[big, big, big,
                   pl.BlockSpec((None, LANES, GB_ST2), lambda g: (g, 0, 0)), vec, vec],
        out_shape=[
            jax.ShapeDtypeStruct((N_GB, GB_IN, GB_IN), BF16),
            jax.ShapeDtypeStruct((N_GB, GB_IN, GB_ST2), BF16),
            jax.ShapeDtypeStruct((N_GB, GB_IN, GB_ST2), BF16),
            jax.ShapeDtypeStruct((N_GB, LANES, GB_ST2), BF16),
            jax.ShapeDtypeStruct((N_GB, 1, GB_ST2), F32),
            jax.ShapeDtypeStruct((N_GB, 1, GB_ST2), F32),
        ],
        compiler_params=_cparams(("arbitrary",)),
        name="ssm_prep",
    )(are_x, aim_x, dt_x, bt_re, bt_im, cr, ci, aflat, iflat, dtflat)


SSM_RT = 256
SCAN_PAD = 8


def _ssm_core_kernel(uc_ref, w_ref, win_ref, cpt_ref, a8_ref, yc_ref, sre_ref, sim_ref, zs,
                     *, n_seq, n_chunk):
    n_rows = n_seq * n_chunk
    nlb = GB_ST2 // LANES
    half = nlb // 2
    pitch = n_chunk + SCAN_PAD

    def zrows(r):
        seq, off = divmod(r * SSM_RT, n_chunk)
        return slice(seq * pitch + off, seq * pitch + off + SSM_RT)

    for r in range(n_rows // SSM_RT):
        rs = slice(r * SSM_RT, (r + 1) * SSM_RT)
        z = jnp.dot(uc_ref[rs, :], win_ref[...], preferred_element_type=F32)
        for c in range(nlb):
            zs[c, zrows(r), :] = z[:, c * LANES:(c + 1) * LANES]

    a8 = a8_ref[...]
    a_bl = [jnp.broadcast_to(a8[:, c * LANES:(c + 1) * LANES], (n_seq, LANES))
            for c in range(nlb)]

    def step(k, carry):
        rows = pl.ds(k, n_seq, stride=pitch)
        new = []
        for c in range(half):
            s_re, s_im = carry[c], carry[half + c]
            z_re = zs[c, rows, :]
            z_im = zs[half + c, rows, :]
            zs[c, rows, :] = s_re
            zs[half + c, rows, :] = s_im
            new.append((a_bl[c] * s_re - a_bl[half + c] * s_im + z_re,
                        a_bl[c] * s_im + a_bl[half + c] * s_re + z_im))
        return tuple(n[0] for n in new) + tuple(n[1] for n in new)

    n_rt = n_rows // SSM_RT
    steps_per = n_chunk // n_rt
    fin = (jnp.zeros((n_seq, LANES), F32),) * nlb
    for r in range(n_rt):
        rs = slice(r * SSM_RT, (r + 1) * SSM_RT)
        for m in range(CHUNK // 2):
            kin = (2 * m + 2) * LANES
            oc = slice(2 * m * LANES, (2 * m + 2) * LANES)
            yc_ref[rs, oc] = jnp.dot(uc_ref[rs, :kin], w_ref[:kin, oc],
                                     preferred_element_type=F32)
        for k in range(r * steps_per, (r + 1) * steps_per):
            fin = step(k, fin)
    sre_ref[...] = jnp.concatenate(fin[:half], axis=1)
    sim_ref[...] = jnp.concatenate(fin[half:], axis=1)

    for r in range(n_rt):
        rs = slice(r * SSM_RT, (r + 1) * SSM_RT)
        sp = jnp.concatenate([zs[c, zrows(r), :] for c in range(nlb)], axis=1).astype(BF16)
        yc_ref[rs, :] += lax.dot_general(sp, cpt_ref[...], (((1,), (1,)), ((), ())),
                                         preferred_element_type=F32)


def _ssm_core(uc, w, win, cpt, a8, n_seq):
    n_rows = uc.shape[0]
    n_chunk = n_rows // n_seq
    mat = lambda r: pl.BlockSpec((None, r, GB_ST2), lambda g: (g, 0, 0))
    blk = pl.BlockSpec((n_rows, GB_IN), lambda g: (0, g))
    st = pl.BlockSpec((n_seq, GB_ST), lambda g: (0, g))
    return pl.pallas_call(
        functools.partial(_ssm_core_kernel, n_seq=n_seq, n_chunk=n_chunk),
        grid=(N_GB,),
        in_specs=[blk, mat(GB_IN), mat(GB_IN), mat(GB_IN), mat(1)],
        out_specs=[blk, st, st],
        out_shape=[
            jax.ShapeDtypeStruct((n_rows, N_GB * GB_IN), F32),
            jax.ShapeDtypeStruct((n_seq, N_GROUPS * SSM_STATE), F32),
            jax.ShapeDtypeStruct((n_seq, N_GROUPS * SSM_STATE), F32),
        ],
        scratch_shapes=[pltpu.VMEM((GB_ST2 // LANES, n_seq * (n_chunk + SCAN_PAD), LANES), F32)],
        compiler_params=_cparams(("arbitrary",)),
        name="ssm_core",
    )(uc, w, win, cpt, a8)


def _glu_tail(x, y, g_ref, dsk_ref, wglu_ref):
    h = _rms(x, g_ref[...])
    z = jax.nn.gelu(y + dsk_ref[...] * h).astype(BF16)
    zz = jnp.dot(z, wglu_ref[...], preferred_element_type=F32)
    return x + zz[:, :D_MODEL] * jax.nn.sigmoid(zz[:, D_MODEL:])


def _ssm_sample_kernel(x_ref, g_ref, bb_ref, c0t_ref, a1_ref, hre_ref, him_ref,
                       y_ref, sre_ref, sim_ref):
    gb = pl.program_id(0)
    x = x_ref[...]
    ms = jnp.mean(x * x, axis=-1, keepdims=True)
    cols = pl.ds(pl.multiple_of(gb * LANES, LANES), LANES)
    u = (x_ref[:, cols] * lax.rsqrt(ms + RMS_EPS) * g_ref[:, cols]).astype(BF16)
    bu = jnp.dot(u, bb_ref[...], preferred_element_type=F32)
    a1 = a1_ref[...]
    a_re = a1[:, :GB_ST]
    a_im = a1[:, GB_ST:]
    h_re = hre_ref[...]
    h_im = him_ref[...]
    s_re = a_re * h_re - a_im * h_im + bu[:, :GB_ST]
    s_im = a_re * h_im + a_im * h_re + bu[:, GB_ST:]
    sre_ref[...] = s_re
    sim_ref[...] = s_im
    s = jnp.concatenate([s_re, s_im], axis=1).astype(BF16)
    y_ref[...] = lax.dot_general(s, c0t_ref[...], (((1,), (1,)), ((), ())),
                                 preferred_element_type=F32)


def _ssm_sample(x, gain, win, c0t, a1, h_re, h_im):
    n, d = x.shape
    st = pl.BlockSpec((n, GB_ST), lambda g: (0, g))
    return pl.pallas_call(
        _ssm_sample_kernel,
        grid=(N_GB,),
        in_specs=[
            _const_spec((n, d)),
            _const_spec((1, d)),
            pl.BlockSpec((None, LANES, GB_ST2), lambda g: (g, CHUNK - 1, 0)),
            pl.BlockSpec((None, LANES, GB_ST2), lambda g: (g, 0, 0)),
            pl.BlockSpec((None, 1, GB_ST2), lambda g: (g, 0, 0)),
            st, st,
        ],
        out_specs=[pl.BlockSpec((n, LANES), lambda g: (0, g)), st, st],
        out_shape=[
            jax.ShapeDtypeStruct((n, d), F32),
            jax.ShapeDtypeStruct((n, N_GROUPS * SSM_STATE), F32),
            jax.ShapeDtypeStruct((n, N_GROUPS * SSM_STATE), F32),
        ],
        compiler_params=_cparams(("arbitrary",)),
        name="ssm_sample",
    )(x, gain, win, c0t, a1, h_re, h_im)


def _glu_sample_kernel(x_ref, y_ref, g_ref, dsk_ref, wglu_ref, xo_ref):
    xo_ref[...] = _glu_tail(x_ref[...], y_ref[...], g_ref, dsk_ref, wglu_ref)


def _glu_sample(x, y, gain, dsk, wglu, layer):
    n, d = x.shape
    return pl.pallas_call(
        _glu_sample_kernel,
        grid=(1,),
        in_specs=[_const_spec((n, d)), _const_spec((n, d)), _const_spec((1, d)),
                  _const_spec((1, d)), pl.BlockSpec((None, d, 2 * d), lambda i: (layer, 0, 0))],
        out_specs=_const_spec((n, d)),
        out_shape=jax.ShapeDtypeStruct((n, d), F32),
        compiler_params=_cparams(("arbitrary",)),
        name="glu_sample",
    )(x, y, gain, dsk, wglu)


def kernel(x_prompt, x_sample, cache_k, cache_v, state_ssm_re, state_ssm_im, state_conv,
           norm_mix, norm_ffn, w_qkv, w_o, q_norm, k_norm, sinks, ssm_a_re, ssm_a_im,
           ssm_log_dt, ssm_b_re, ssm_b_im, ssm_c_re, ssm_c_im, ssm_d, w_glu, w_up, conv_w,
           conv_b, w_down):
    depth = norm_mix.shape[0]
    nb, _, _ = x_prompt.shape
    ns = x_sample.shape[0]
    xp = x_prompt
    xs = x_sample.reshape(ns, D_MODEL)
    row = lambda t: t.reshape(1, -1)

    wqkv = w_qkv.astype(BF16)
    wo = w_o.astype(BF16)
    wo_t = jnp.swapaxes(wo, 1, 2)
    wglu = w_glu.astype(BF16)
    wup = w_up.astype(BF16)
    wdn = w_down.astype(BF16)
    cb = conv_b.reshape(depth, 1, F2)
    n_att = wqkv.shape[0]
    wq_g = wqkv[:, :, :D_MODEL].reshape(n_att, D_MODEL, N_KV, GQA, HEAD_DIM)
    wq_g = jnp.transpose(wq_g, (0, 3, 1, 2, 4)).reshape(n_att, GQA, D_MODEL, KV_DIM)
    wo_g = wo.reshape(n_att, N_KV, GQA, HEAD_DIM, D_MODEL)
    wo_g = jnp.transpose(wo_g, (0, 2, 1, 3, 4)).reshape(n_att, GQA, KV_DIM, D_MODEL)
    ck = cache_k.reshape(n_att, ns, WINDOW, KV_DIM)
    cv = cache_v.reshape(n_att, ns, WINDOW, KV_DIM)

    kps, vps = [], []
    srp, sip, srs, sis = [], [], [], []
    cps = []
    conv_s = state_conv
    glu = None
    for i in range(depth):
        j = i // 2
        gm = row(norm_mix[i])
        if i % 2 == 0:
            qg, kg = row(q_norm[j]), row(k_norm[j])
            xp, kp, vp = _attn_prompt(xp, gm, wqkv, wo_t, jnp.tile(qg, (1, N_HEADS)),
                                      jnp.tile(kg, (1, N_KV)), sinks[j], j)
            xs, ck, cv = _attn_sample(xs, gm, wq_g, wqkv, wo_g, qg, kg, sinks[j], ck, cv, j)
            kps.append(kp.reshape(nb, WINDOW, N_KV, HEAD_DIM))
            vps.append(vp.reshape(nb, WINDOW, N_KV, HEAD_DIM))
        else:
            w, win, cpt, c0t, a8, a1 = _ssm_prep(
                ssm_a_re[j], ssm_a_im[j], ssm_log_dt[j], ssm_b_re[j], ssm_b_im[j],
                ssm_c_re[j], ssm_c_im[j])
            dsk = row(ssm_d[j])
            yc, s_re, s_im = _ssm_core(uc, w, win, cpt, a8, nb)
            glu = (yc, gm, dsk, wglu, j)
            srp.append(s_re.reshape(nb, N_GROUPS, SSM_STATE))
            sip.append(s_im.reshape(nb, N_GROUPS, SSM_STATE))
            ys, t_re, t_im = _ssm_sample(
                xs, gm, win, c0t, a1,
                state_ssm_re[j].reshape(ns, N_GROUPS * SSM_STATE),
                state_ssm_im[j].reshape(ns, N_GROUPS * SSM_STATE))
            xs = _glu_sample(xs, ys, gm, dsk, wglu, j)
            srs.append(t_re.reshape(ns, N_GROUPS, SSM_STATE))
            sis.append(t_im.reshape(ns, N_GROUPS, SSM_STATE))

        gf = row(norm_ffn[i])
        next_gain = row(norm_mix[i + 1]) if (i % 2 == 0 and i + 1 < depth) else None
        res = _ffn_prompt(xp, gf, wup, conv_w, cb, wdn, i, glu=glu, next_gain=next_gain)
        glu = None
        xp, cp = res[0], res[1]
        if next_gain is not None:
            uc = res[2]
        xs, conv_s = _ffn_sample(xs, gf, wup, conv_w, cb, wdn, conv_s, i)
        cps.append(cp)

    k_s, v_s = (t.reshape(n_att, ns, WINDOW, N_KV, HEAD_DIM) for t in (ck, cv))
    return (xp, xs.reshape(ns, 1, D_MODEL),
            jnp.stack(kps), jnp.stack(vps), k_s, v_s,
            jnp.stack(srp), jnp.stack(sip), jnp.stack(srs), jnp.stack(sis),
            jnp.stack(cps), conv_s)
```

```python
import functools

import jax
import jax.numpy as jnp
from jax import lax
from jax.experimental import pallas as pl
from jax.experimental.pallas import tpu as pltpu

F32 = jnp.float32
BF16 = jnp.bfloat16

D_MODEL = 1024
HEAD_DIM = 64
N_HEADS = 16
N_KV = 4
GQA = 4
WINDOW = 128
KV_DIM = N_KV * HEAD_DIM
QKV_DIM = D_MODEL + 2 * KV_DIM
D_FF = 2816
F2 = 2 * D_FF
FF_BLK = 256
N_FF_BLK = D_FF // FF_BLK
SSM_GROUP = 16
N_GROUPS = 64
SSM_STATE = 64
RMS_EPS = 1e-6

LANES = 128
CHUNK = 8
GB_GROUPS = LANES // SSM_GROUP
N_GB = N_GROUPS // GB_GROUPS
GB_IN = CHUNK * LANES
GB_ST = GB_GROUPS * SSM_STATE
GB_ST2 = 2 * GB_ST

VMEM_LIMIT = 56 * 1024 * 1024


def _cparams(sem):
    return pltpu.CompilerParams(dimension_semantics=sem, vmem_limit_bytes=VMEM_LIMIT)


def _rms(x, g):
    ms = jnp.mean(x * x, axis=-1, keepdims=True)
    return x * lax.rsqrt(ms + RMS_EPS) * g


def _head_rms(t, n, g):
    return jnp.concatenate(
        [_rms(t[:, j * HEAD_DIM:(j + 1) * HEAD_DIM], g) for j in range(n)], axis=1)


def _const_spec(shape):
    nd = len(shape)
    return pl.BlockSpec(shape, lambda *_: (0,) * nd)


ATT_TQ = 512
ATT_NBLK = ATT_TQ // WINDOW
SEG_W = 256
LOG2E = 1.4426950408889634


def _seg_mean_sq(t, seg):
    sq = t * t
    hi = sq.astype(BF16)
    lo = (sq - hi.astype(F32)).astype(BF16)
    out = []
    for c in range(t.shape[1] // SEG_W):
        sl = slice(c * SEG_W, (c + 1) * SEG_W)
        out.append(jnp.dot(hi[:, sl], seg, preferred_element_type=F32)
                   + jnp.dot(lo[:, sl], seg, preferred_element_type=F32))
    return jnp.concatenate(out, axis=1) * (1.0 / HEAD_DIM)


VT_ROWS = HEAD_DIM + 16


def _attn_prompt_kernel(sink_ref, x_ref, g_ref, wqkv_ref, wot_ref, qg_ref, kg_ref,
                        xo_ref, kl_ref, vl_ref, kbuf, krol, vt, lo_scr, ot):
    i = pl.program_id(1)
    keys = 2 * WINDOW

    @pl.when(i == 0)
    def _():
        kbuf[0:WINDOW, :] = jnp.zeros((WINDOW, KV_DIM), BF16)
        krol[0:WINDOW, :] = jnp.zeros((WINDOW, KV_DIM), BF16)
        vt[:, 0:HEAD_DIM, 0:WINDOW] = jnp.zeros((N_KV, HEAD_DIM, WINDOW), BF16)
        vt[:, HEAD_DIM:, :] = jnp.ones((N_KV, VT_ROWS - HEAD_DIM, WINDOW + ATT_TQ), BF16)
        lo_scr[...] = jnp.full(lo_scr.shape, WINDOW, jnp.int32)

    @pl.when(i == 1)
    def _():
        lo_scr[...] = jnp.zeros(lo_scr.shape, jnp.int32)

    si = lax.broadcasted_iota(jnp.int32, (SEG_W, SEG_W), 0) // HEAD_DIM
    sj = lax.broadcasted_iota(jnp.int32, (SEG_W, SEG_W), 1) // HEAD_DIM
    seg = jnp.where(si == sj, 1.0, 0.0).astype(BF16)

    x = x_ref[...]
    h = _rms(x, g_ref[...]).astype(BF16)
    qkv = jnp.dot(h, wqkv_ref[...], preferred_element_type=F32)
    q = qkv[:, :D_MODEL]
    k = qkv[:, D_MODEL:D_MODEL + KV_DIM]
    v = qkv[:, D_MODEL + KV_DIM:]
    qn = q * lax.rsqrt(_seg_mean_sq(q, seg) + RMS_EPS) * qg_ref[...] * (HEAD_DIM ** -0.5 * LOG2E)
    kn = k * lax.rsqrt(_seg_mean_sq(k, seg) + RMS_EPS) * kg_ref[...]
    kl_ref[...] = kn[ATT_TQ - WINDOW:]
    vl_ref[...] = v[ATT_TQ - WINDOW:]
    kbuf[WINDOW:, :] = kn.astype(BF16)
    for c in range(KV_DIM // LANES):
        sl = slice(c * LANES, (c + 1) * LANES)
        krol[WINDOW:, sl] = pltpu.roll(kn[:, sl], HEAD_DIM, 1).astype(BF16)
    v_t = v.T
    for kvh in range(N_KV):
        vt[kvh, 0:HEAD_DIM, WINDOW:] = v_t[kvh * HEAD_DIM:(kvh + 1) * HEAD_DIM].astype(BF16)

    kc = lax.broadcasted_iota(jnp.int32, (keys, 2 * WINDOW), 0)
    qi = lax.broadcasted_iota(jnp.int32, (keys, 2 * WINDOW), 1) % WINDOW
    band = (kc > qi) & (kc <= qi + WINDOW)
    low = lax.broadcasted_iota(jnp.int32, (WINDOW, LANES), 1) < HEAD_DIM
    nt = (((1,), (1,)), ((), ()))

    for blk in range(ATT_NBLK):
        r0 = blk * WINDOW
        valid = band & (kc >= lo_scr[...]) if blk == 0 else band
        combos = []
        for kvh in range(N_KV):
            khalf = kvh % 2
            combos.append((kvh, (khalf, khalf + 2), kbuf))
            combos.append((kvh, (1 - khalf, 3 - khalf), krol))
        scores = []
        for kvh, pair, kref in combos:
            kcol = slice((kvh // 2) * LANES, (kvh // 2 + 1) * LANES)
            qm = []
            for g in pair:
                qcol = slice((kvh * 2 + g // 2) * LANES, (kvh * 2 + g // 2 + 1) * LANES)
                qm.append(jnp.where(low if g % 2 == 0 else ~low, qn[r0:r0 + WINDOW, qcol], 0.0))
            scores.append(lax.dot_general(kref[r0:r0 + keys, kcol],
                                          jnp.concatenate(qm, axis=0).astype(BF16), nt,
                                          preferred_element_type=F32))
        probs = []
        for (kvh, pair, _), s in zip(combos, scores):
            sink_row = jnp.concatenate(
                [jnp.full((1, WINDOW), sink_ref[kvh * GQA + g] * LOG2E, F32) for g in pair],
                axis=1)
            s = jnp.where(valid, s, -jnp.inf)
            m = jnp.maximum(jnp.max(s, axis=0, keepdims=True), sink_row)
            probs.append((jnp.exp2(s - m).astype(BF16), jnp.exp2(sink_row - m)))
        outs = [jnp.dot(vt[kvh, :, r0:r0 + keys], e, preferred_element_type=F32)
                for (kvh, _, _), (e, _) in zip(combos, probs)]
        for (kvh, pair, _), (_, e_sink), o in zip(combos, probs, outs):
            rden = 1.0 / (o[HEAD_DIM:HEAD_DIM + 1] + e_sink)
            on = (o[:HEAD_DIM] * rden).astype(BF16)
            for n, g in enumerate(pair):
                hd = kvh * GQA + g
                ot[hd * HEAD_DIM:(hd + 1) * HEAD_DIM, r0:r0 + WINDOW] = (
                    on[:, n * WINDOW:(n + 1) * WINDOW])

    kbuf[0:WINDOW, :] = kbuf[ATT_TQ:, :]
    krol[0:WINDOW, :] = krol[ATT_TQ:, :]
    vt[:, 0:HEAD_DIM, 0:WINDOW] = vt[:, 0:HEAD_DIM, ATT_TQ:]
    out_t = jnp.dot(wot_ref[...], ot[...], preferred_element_type=F32)
    xo_ref[...] = x + out_t.T


def _attn_prompt(x, gain, wqkv, wo_t, qg, kg, sinks, layer):
    b, l, d = x.shape
    wsel = lambda bi, i: (layer, 0, 0)
    tok = pl.BlockSpec((None, ATT_TQ, d), lambda bi, i: (bi, i, 0))
    last = pl.BlockSpec((None, WINDOW, KV_DIM), lambda bi, i: (bi, 0, 0))
    kvbuf = pltpu.VMEM((WINDOW + ATT_TQ, KV_DIM), BF16)
    return pl.pallas_call(
        _attn_prompt_kernel,
        grid=(b, l // ATT_TQ),
        in_specs=[
            pl.BlockSpec(memory_space=pltpu.SMEM),
            tok,
            _const_spec((1, d)),
            pl.BlockSpec((None, d, QKV_DIM), wsel),
            pl.BlockSpec((None, d, d), wsel),
            _const_spec((1, d)),
            _const_spec((1, KV_DIM)),
        ],
        out_specs=[tok, last, last],
        out_shape=[
            jax.ShapeDtypeStruct((b, l, d), F32),
            jax.ShapeDtypeStruct((b, WINDOW, KV_DIM), F32),
            jax.ShapeDtypeStruct((b, WINDOW, KV_DIM), F32),
        ],
        scratch_shapes=[
            kvbuf, kvbuf,
            pltpu.VMEM((N_KV, VT_ROWS, WINDOW + ATT_TQ), BF16),
            pltpu.VMEM((2 * WINDOW, 2 * WINDOW), jnp.int32),
            pltpu.VMEM((d, ATT_TQ), BF16),
        ],
        compiler_params=_cparams(("arbitrary", "arbitrary")),
        name="attn_prompt",
    )(sinks, x, gain, wqkv, wo_t, qg, kg)


ATS_TB = 16


def _attn_sample_kernel(sink_ref, x_ref, g_ref, wq_ref, wkv_ref, wo_ref, qg_ref, kg_ref,
                        ck_ref, cv_ref, xo_ref, ko_ref, vo_ref, r_scr):
    x = x_ref[...]
    h = _rms(x, g_ref[...]).astype(BF16)
    kv = jnp.dot(h, wkv_ref[...], preferred_element_type=F32)
    kn = _head_rms(kv[:, :KV_DIM], N_KV, kg_ref[...])
    v = kv[:, KV_DIM:]
    qg = qg_ref[...]
    q_g = [_head_rms(jnp.dot(h, wq_ref[g], preferred_element_type=F32), N_KV, qg)
           * (HEAD_DIM ** -0.5) for g in range(GQA)]

    rows = GQA * N_KV
    rkv = lax.broadcasted_iota(jnp.int32, (rows, KV_DIM), 0) % N_KV
    lkv = lax.broadcasted_iota(jnp.int32, (rows, KV_DIM), 1) // HEAD_DIM
    own = rkv == lkv
    colj = lax.broadcasted_iota(jnp.int32, (rows, WINDOW), 1)
    sink_col = jnp.concatenate(
        [jnp.full((1, 1), sink_ref[(r % N_KV) * GQA + r // N_KV], F32) for r in range(rows)],
        axis=0)

    for n in range(ATS_TB):
        ko_ref[n, 0:WINDOW - 1, :] = ck_ref[n, 1:WINDOW, :]
        ko_ref[n, WINDOW - 1:WINDOW, :] = kn[n:n + 1]
        vo_ref[n, 0:WINDOW - 1, :] = cv_ref[n, 1:WINDOW, :]
        vo_ref[n, WINDOW - 1:WINDOW, :] = v[n:n + 1]
    qbds = []
    for n in range(ATS_TB):
        qbd = jnp.concatenate(
            [jnp.broadcast_to(q_g[g][n:n + 1], (N_KV, KV_DIM)) for g in range(GQA)], axis=0)
        qbds.append(jnp.where(own, qbd, 0.0))
    scores = [lax.dot_general(qbds[n].astype(BF16), ck_ref[n].astype(BF16),
                              (((1,), (1,)), ((), ())), preferred_element_type=F32)
              for n in range(ATS_TB)]
    probs = []
    for n in range(ATS_TB):
        s = jnp.where(colj >= 1, scores[n], -jnp.inf)
        s_new = jnp.sum(qbds[n] * kn[n:n + 1], axis=-1, keepdims=True)
        m = jnp.maximum(jnp.maximum(jnp.max(s, axis=-1, keepdims=True), s_new), sink_col)
        e = jnp.exp(s - m)
        e_new = jnp.exp(s_new - m)
        rden = 1.0 / (jnp.sum(e, axis=-1, keepdims=True) + e_new + jnp.exp(sink_col - m))
        probs.append(((e * rden).astype(BF16), e_new * rden))
    outs = [jnp.dot(probs[n][0], cv_ref[n].astype(BF16), preferred_element_type=F32)
            for n in range(ATS_TB)]
    for n in range(ATS_TB):
        o = jnp.where(own, outs[n] + probs[n][1] * v[n:n + 1], 0.0)
        for g in range(GQA):
            r_scr[g, n:n + 1, :] = jnp.sum(o[g * N_KV:(g + 1) * N_KV], axis=0, keepdims=True)

    out = x
    for g in range(GQA):
        out = out + jnp.dot(r_scr[g].astype(BF16), wo_ref[g], preferred_element_type=F32)
    xo_ref[...] = out


def _attn_sample(x, gain, wq_g, wqkv, wo_g, qg, kg, sinks, cache_k, cache_v, layer):
    n, d = x.shape
    tok = pl.BlockSpec((ATS_TB, d), lambda i: (i, 0))
    cache = pl.BlockSpec((None, ATS_TB, WINDOW, KV_DIM), lambda i: (layer, i, 0, 0))
    in_specs = [
        pl.BlockSpec(memory_space=pltpu.SMEM),
        tok,
        _const_spec((1, d)),
        pl.BlockSpec((None, GQA, d, KV_DIM), lambda i: (layer, 0, 0, 0)),
        pl.BlockSpec((None, d, 2 * KV_DIM), lambda i: (layer, 0, D_MODEL // (2 * KV_DIM))),
        pl.BlockSpec((None, GQA, KV_DIM, d), lambda i: (layer, 0, 0, 0)),
        _const_spec((1, HEAD_DIM)),
        _const_spec((1, HEAD_DIM)),
        cache,
        cache,
    ]
    args = [sinks, x, gain, wq_g, wqkv, wo_g, qg, kg, cache_k, cache_v]
    return pl.pallas_call(
        _attn_sample_kernel,
        grid=(n // ATS_TB,),
        in_specs=in_specs,
        out_specs=[tok, cache, cache],
        out_shape=[
            jax.ShapeDtypeStruct((n, d), F32),
            jax.ShapeDtypeStruct(cache_k.shape, F32),
            jax.ShapeDtypeStruct(cache_v.shape, F32),
        ],
        scratch_shapes=[pltpu.VMEM((GQA, ATS_TB, KV_DIM), F32)],
        input_output_aliases={len(args) - 2: 1, len(args) - 1: 2},
        compiler_params=_cparams(("arbitrary",)),
        name="attn_sample",
    )(*args)


FFN_TM = 512
SSM_TC = FFN_TM // CHUNK
CARRY = 8
CONV_RC = 64


def _ffn_prompt_kernel(*refs, glu_in, uc_out):
    refs = list(refs)
    x_ref = refs.pop(0)
    if glu_in:
        yc_ref, gm_ref, dsk_ref, wglu_ref = (refs.pop(0) for _ in range(4))
    g_ref, wup_ref, cw_ref, cb_ref, wdn_ref = (refs.pop(0) for _ in range(5))
    if uc_out:
        gn_ref = refs.pop(0)
    xo_ref, cs_ref = refs.pop(0), refs.pop(0)
    if uc_out:
        uc_ref = refs.pop(0)
    carry, upbuf, act, hs, oscr = refs

    i = pl.program_id(1)
    hm = FFN_TM // 2
    nlb = FF_BLK // LANES
    dlb = D_MODEL // LANES

    @pl.when(i == 0)
    def _():
        carry[...] = jnp.zeros_like(carry)

    if glu_in:
        for gb in range(N_GB):
            for t in range(CHUNK):
                oscr[gb, pl.ds(t, SSM_TC, stride=CHUNK), :] = (
                    yc_ref[:, gb * GB_IN + t * LANES:gb * GB_IN + (t + 1) * LANES])
        y = jnp.concatenate([oscr[gb] for gb in range(dlb)], axis=1)
        xo_ref[...] = _glu_tail(x_ref[...], y, gm_ref, dsk_ref, wglu_ref)
        xin_ref = xo_ref
    else:
        xin_ref = x_ref

    hs[...] = _rms(xin_ref[...], g_ref[...]).astype(BF16)

    def up_into(j, slot):
        for half, c0 in enumerate((j * FF_BLK, D_FF + j * FF_BLK)):
            cols = pl.ds(c0, FF_BLK)
            up = jnp.dot(hs[...], wup_ref[:, cols], preferred_element_type=F32)
            for lb in range(nlb):
                lcols = pl.ds(c0 + lb * LANES, LANES)
                buf = upbuf.at[slot, half, lb]
                buf[0:CARRY, :] = carry[:, lcols]
                buf[CARRY:, :] = up[:, lb * LANES:(lb + 1) * LANES]
                carry[:, lcols] = buf[FFN_TM:, :]

    def conv(slot, half, lb, c0, m0):
        lcols = pl.ds(c0 + lb * LANES, LANES)
        buf = upbuf.at[slot, half, lb]
        tap = lambda off: buf[pl.ds(CARRY + off + 2 * m0, CONV_RC, stride=2), :]
        em2, om1, e0, o1 = tap(-2), tap(-1), tap(0), tap(1)
        w0, w1, w2 = cw_ref[0:1, lcols], cw_ref[1:2, lcols], cw_ref[2:3, lcols]
        cb = cb_ref[:, lcols]
        return (cb + w0 * em2 + w1 * om1 + w2 * e0, cb + w0 * om1 + w1 * e0 + w2 * o1)

    for j in range(N_FF_BLK):
        up_into(j, j % 2)
        for m0 in range(0, hm, CONV_RC):
            for lb in range(nlb):
                cg = conv(j % 2, 0, lb, j * FF_BLK, m0)
                cv = conv(j % 2, 1, lb, D_FF + j * FF_BLK, m0)
                acol = pl.ds(j * FF_BLK + lb * LANES, LANES)
                for p in range(2):
                    act[p * hm + m0:p * hm + m0 + CONV_RC, acol] = (
                        jax.nn.silu(cg[p]) * cv[p]).astype(BF16)

    out = jnp.dot(act[...], wdn_ref[...], preferred_element_type=F32)
    for lb in range(dlb):
        for p in range(2):
            oscr[lb, pl.ds(p, hm, stride=2), :] = out[p * hm:(p + 1) * hm,
                                                      lb * LANES:(lb + 1) * LANES]
    xo = xin_ref[...] + jnp.concatenate([oscr[lb] for lb in range(dlb)], axis=1)
    xo_ref[...] = xo
    cs_ref[...] = carry[CARRY - 2:CARRY, :]

    if uc_out:
        h2 = _rms(xo, gn_ref[...])
        for gb in range(N_GB):
            oscr[gb] = h2[:, gb * LANES:(gb + 1) * LANES]
        for gb in range(N_GB):
            for t in range(CHUNK):
                uc_ref[:, gb * GB_IN + t * LANES:gb * GB_IN + (t + 1) * LANES] = (
                    oscr[gb, pl.ds(t, SSM_TC, stride=CHUNK), :].astype(BF16))


def _ffn_prompt(x, gain, wup, cw, cb, wdn, layer, glu=None, next_gain=None):
    b, l, d = x.shape
    nt = l // FFN_TM
    tok = pl.BlockSpec((None, FFN_TM, d), lambda bi, i: (bi, i, 0))
    chunks = pl.BlockSpec((SSM_TC, CHUNK * d), lambda bi, i: (bi * nt + i, 0))
    wsel = lambda bi, i: (layer, 0, 0)
    args, in_specs = [x], [tok]
    if glu is not None:
        yc, gm, dsk, wglu, mixer_layer = glu
        args += [yc, gm, dsk, wglu]
        in_specs += [chunks, _const_spec((1, d)), _const_spec((1, d)),
                     pl.BlockSpec((None, d, 2 * d), lambda bi, i: (mixer_layer, 0, 0),
                                  pipeline_mode=pl.Buffered(1))]
    args += [gain, wup, cw, cb, wdn]
    in_specs += [
        _const_spec((1, d)),
        pl.BlockSpec((None, d, F2), wsel, pipeline_mode=pl.Buffered(1)),
        pl.BlockSpec((None, 3, F2), wsel),
        pl.BlockSpec((None, 1, F2), wsel),
        pl.BlockSpec((None, D_FF, d), wsel, pipeline_mode=pl.Buffered(1)),
    ]
    out_specs = [tok, pl.BlockSpec((None, 2, F2), lambda bi, i: (bi, 0, 0))]
    out_shape = [jax.ShapeDtypeStruct((b, l, d), F32), jax.ShapeDtypeStruct((b, 2, F2), F32)]
    if next_gain is not None:
        args.append(next_gain)
        in_specs.append(_const_spec((1, d)))
        out_specs.append(chunks)
        out_shape.append(jax.ShapeDtypeStruct((b * l // CHUNK, CHUNK * d), BF16))
    return pl.pallas_call(
        functools.partial(_ffn_prompt_kernel, glu_in=glu is not None,
                          uc_out=next_gain is not None),
        grid=(b, nt),
        in_specs=in_specs,
        out_specs=out_specs,
        out_shape=out_shape,
        scratch_shapes=[
            pltpu.VMEM((CARRY, F2), F32),
            pltpu.VMEM((2, 2, FF_BLK // LANES, CARRY + FFN_TM, LANES), F32),
            pltpu.VMEM((FFN_TM, D_FF), BF16),
            pltpu.VMEM((FFN_TM, d), BF16),
            pltpu.VMEM((d // LANES, FFN_TM, LANES), F32),
        ],
        compiler_params=_cparams(("arbitrary", "arbitrary")),
        name="ffn_prompt",
    )(*args)


FFS_BLK = D_FF // 2
N_FFS_BLK = D_FF // FFS_BLK


def _ffn_sample_kernel(x_ref, g_ref, w_ref, cw_ref, cb_ref, wdn_ref, sc_ref,
                       xo_ref, cs_ref, gate):
    j = pl.program_id(0)

    @pl.when(j == 0)
    def _():
        xo_ref[...] = x_ref[...]

    h = _rms(x_ref[...], g_ref[...]).astype(BF16)
    up = jnp.dot(h, w_ref[...], preferred_element_type=F32)
    b0 = sc_ref[:, 0, :]
    b1 = sc_ref[:, 1, :]
    c = cb_ref[...]
    c = c + cw_ref[0:1, :] * b0
    c = c + cw_ref[1:2, :] * b1
    c = c + cw_ref[2:3, :] * up
    cs_ref[...] = jnp.stack([b1, up], axis=1)

    @pl.when(j < N_FFS_BLK)
    def _():
        gate[j] = jax.nn.silu(c)

    @pl.when(j >= N_FFS_BLK)
    def _():
        a = (gate[j - N_FFS_BLK] * c).astype(BF16)
        xo_ref[...] += jnp.dot(a, wdn_ref[...], preferred_element_type=F32)


def _ffn_sample(x, gain, wup, cw, cb, wdn, state, layer):
    n, d = x.shape
    nb = N_FFS_BLK
    sblk = pl.BlockSpec((None, n, 2, FFS_BLK), lambda j: (layer, 0, 0, j))
    in_specs = [
        _const_spec((n, d)),
        _const_spec((1, d)),
        pl.BlockSpec((None, d, FFS_BLK), lambda j: (layer, 0, j)),
        pl.BlockSpec((None, 3, FFS_BLK), lambda j: (layer, 0, j)),
        pl.BlockSpec((None, 1, FFS_BLK), lambda j: (layer, 0, j)),
        pl.BlockSpec((None, FFS_BLK, d), lambda j: (layer, jnp.maximum(j - nb, 0), 0)),
        sblk,
    ]
    args = [x, gain, wup, cw, cb, wdn, state]
    return pl.pallas_call(
        _ffn_sample_kernel,
        grid=(2 * nb,),
        in_specs=in_specs,
        out_specs=[_const_spec((n, d)), sblk],
        out_shape=[
            jax.ShapeDtypeStruct((n, d), F32),
            jax.ShapeDtypeStruct(state.shape, F32),
        ],
        scratch_shapes=[pltpu.VMEM((nb, n, FFS_BLK), F32)],
        input_output_aliases={len(args) - 1: 1},
        compiler_params=_cparams(("arbitrary",)),
        name="ffn_sample",
    )(*args)


def _tile_states(t):
    t2 = jnp.concatenate([t, t], axis=1)
    return jnp.concatenate([t2] * (GB_GROUPS // 2), axis=1)


def _ssm_prep_kernel(are_ref, aim_ref, dt_ref, bre_ref, bim_ref, cre_ref, cim_ref,
                     aflat_ref, iflat_ref, dtflat_ref,
                     w_ref, win_ref, cpt_ref, c0t_ref, a8_ref, a1_ref):
    a_re = are_ref[...]
    a_im = aim_ref[...]
    dt = jnp.exp(dt_ref[...])

    mag = jnp.exp(a_re * dt)
    ab_re, ab_im = mag * jnp.cos(a_im * dt), mag * jnp.sin(a_im * dt)
    pw = [(jnp.ones_like(ab_re), jnp.zeros_like(ab_re)), (ab_re, ab_im)]
    for _ in range(CHUNK - 1):
        p_re, p_im = pw[-1]
        pw.append((p_re * ab_re - p_im * ab_im, p_re * ab_im + p_im * ab_re))
    den = a_re * a_re + a_im * a_im
    z_re = ((ab_re - 1.0) * a_re + ab_im * a_im) / den
    z_im = (ab_im * a_re - (ab_re - 1.0) * a_im) / den
    b_re = bre_ref[...]
    b_im = bim_ref[...]
    bb_re = z_re * b_re - z_im * b_im
    bb_im = z_re * b_im + z_im * b_re
    c_re = cre_ref[...]
    c_im = cim_ref[...]

    rg = lax.broadcasted_iota(jnp.int32, (LANES, GB_ST2), 0) // SSM_GROUP
    lg = (lax.broadcasted_iota(jnp.int32, (LANES, GB_ST2), 1) % GB_ST) // SSM_STATE
    own = rg == lg

    def blockdiag(t_re, t_im):
        full = jnp.concatenate([_tile_states(t_re), _tile_states(t_im)], axis=1)
        return jnp.where(own, full, 0.0)

    c0t = blockdiag(c_re, -c_im)
    c0t_ref[...] = c0t.astype(BF16)

    lfull = []
    for d in range(CHUNK):
        if d == 0:
            l_re, l_im = bb_re, bb_im
        else:
            e_re, e_im = pw[d]
            l_re = e_re * bb_re - e_im * bb_im
            l_im = e_re * bb_im + e_im * bb_re
        lfull.append(blockdiag(l_re, l_im))

    def split(t):
        hi = t.astype(BF16)
        return hi, (t - hi.astype(F32)).astype(BF16)

    def dot_nt(a, b):
        return lax.dot_general(a, b, (((1,), (1,)), ((), ())), preferred_element_type=F32)

    c_hi, c_lo = split(c0t)
    kd = []
    for l in lfull:
        l_hi, l_lo = split(l)
        kd.append(dot_nt(l_hi, c_hi) + dot_nt(l_hi, c_lo) + dot_nt(l_lo, c_hi))
    zero = jnp.zeros((LANES, LANES), F32)
    for s in range(CHUNK):
        row = jnp.concatenate([kd[t - s] if t >= s else zero for t in range(CHUNK)], axis=1)
        w_ref[s * LANES:(s + 1) * LANES, :] = row.astype(BF16)
        win_ref[s * LANES:(s + 1) * LANES, :] = lfull[CHUNK - 1 - s].astype(BF16)
    for t in range(CHUNK):
        e_re, e_im = pw[t + 1]
        cpt_ref[t * LANES:(t + 1) * LANES, :] = blockdiag(
            c_re * e_re - c_im * e_im, -c_re * e_im - c_im * e_re).astype(BF16)

    af = aflat_ref[...]
    ai = iflat_ref[...]
    dtf = jnp.exp(dtflat_ref[...])
    magf = jnp.exp(af * dtf)
    f_re, f_im = magf * jnp.cos(ai * dtf), magf * jnp.sin(ai * dtf)
    a1_ref[...] = jnp.concatenate([f_re, f_im], axis=1)
    for _ in range(CHUNK.bit_length() - 1):
        f_re, f_im = f_re * f_re - f_im * f_im, 2.0 * f_re * f_im
    a8_ref[...] = jnp.concatenate([f_re, f_im], axis=1)


def _ssm_prep(a_re, a_im, log_dt, b_re, b_im, c_re, c_im):
    rep = lambda t: jnp.repeat(t, SSM_GROUP, axis=0)
    are_x = rep(a_re)
    aim_x = rep(a_im)
    dt_x = rep(log_dt[:, None])
    bt_re = jnp.transpose(b_re, (0, 2, 1)).reshape(N_GROUPS * SSM_GROUP, SSM_STATE)
    bt_im = jnp.transpose(b_im, (0, 2, 1)).reshape(N_GROUPS * SSM_GROUP, SSM_STATE)
    cr = c_re.reshape(N_GROUPS * SSM_GROUP, SSM_STATE)
    ci = c_im.reshape(N_GROUPS * SSM_GROUP, SSM_STATE)
    aflat = a_re.reshape(N_GB, 1, GB_ST)
    iflat = a_im.reshape(N_GB, 1, GB_ST)
    dtflat = jnp.repeat(log_dt, SSM_STATE).reshape(N_GB, 1, GB_ST)
    rows = pl.BlockSpec((LANES, SSM_STATE), lambda g: (g, 0))
    flat = pl.BlockSpec((None, 1, GB_ST), lambda g: (g, 0, 0))
    big = pl.BlockSpec((None, GB_IN, GB_ST2), lambda g: (g, 0, 0))
    vec = pl.BlockSpec((None, 1, GB_ST2), lambda g: (g, 0, 0))
    return pl.pallas_call(
        _ssm_prep_kernel,
        grid=(N_GB,),
        in_specs=[rows, rows, pl.BlockSpec((LANES, 1), lambda g: (g, 0)),
                  rows, rows, rows, rows, flat, flat, flat],
        out_specs=[big, big, big,
                   pl.BlockSpec((None, LANES, GB_ST2), lambda g: (g, 0, 0)), vec, vec],
        out_shape=[
            jax.ShapeDtypeStruct((N_GB, GB_IN, GB_IN), BF16),
            jax.ShapeDtypeStruct((N_GB, GB_IN, GB_ST2), BF16),
            jax.ShapeDtypeStruct((N_GB, GB_IN, GB_ST2), BF16),
            jax.ShapeDtypeStruct((N_GB, LANES, GB_ST2), BF16),
            jax.ShapeDtypeStruct((N_GB, 1, GB_ST2), F32),
            jax.ShapeDtypeStruct((N_GB, 1, GB_ST2), F32),
        ],
        compiler_params=_cparams(("arbitrary",)),
        name="ssm_prep",
    )(are_x, aim_x, dt_x, bt_re, bt_im, cr, ci, aflat, iflat, dtflat)


SSM_RT = 256
SCAN_PAD = 8


def _ssm_core_kernel(uc_ref, w_ref, win_ref, cpt_ref, a8_ref, yc_ref, sre_ref, sim_ref, zs,
                     *, n_seq, n_chunk):
    n_rows = n_seq * n_chunk
    nlb = GB_ST2 // LANES
    half = nlb // 2
    pitch = n_chunk + SCAN_PAD

    def zrows(r):
        seq, off = divmod(r * SSM_RT, n_chunk)
        return slice(seq * pitch + off, seq * pitch + off + SSM_RT)

    for r in range(n_rows // SSM_RT):
        rs = slice(r * SSM_RT, (r + 1) * SSM_RT)
        z = jnp.dot(uc_ref[rs, :], win_ref[...], preferred_element_type=F32)
        for c in range(nlb):
            zs[c, zrows(r), :] = z[:, c * LANES:(c + 1) * LANES]

    a8 = a8_ref[...]
    a_bl = [jnp.broadcast_to(a8[:, c * LANES:(c + 1) * LANES], (n_seq, LANES))
            for c in range(nlb)]

    def step(k, carry):
        rows = pl.ds(k, n_seq, stride=pitch)
        new = []
        for c in range(half):
            s_re, s_im = carry[c], carry[half + c]
            z_re = zs[c, rows, :]
            z_im = zs[half + c, rows, :]
            zs[c, rows, :] = s_re
            zs[half + c, rows, :] = s_im
            new.append((a_bl[c] * s_re - a_bl[half + c] * s_im + z_re,
                        a_bl[c] * s_im + a_bl[half + c] * s_re + z_im))
        return tuple(n[0] for n in new) + tuple(n[1] for n in new)

    n_rt = n_rows // SSM_RT
    steps_per = n_chunk // n_rt
    fin = (jnp.zeros((n_seq, LANES), F32),) * nlb
    for r in range(n_rt):
        rs = slice(r * SSM_RT, (r + 1) * SSM_RT)
        for m in range(CHUNK // 2):
            kin = (2 * m + 2) * LANES
            oc = slice(2 * m * LANES, (2 * m + 2) * LANES)
            yc_ref[rs, oc] = jnp.dot(uc_ref[rs, :kin], w_ref[:kin, oc],
                                     preferred_element_type=F32)
        for k in range(r * steps_per, (r + 1) * steps_per):
            fin = step(k, fin)
    sre_ref[...] = jnp.concatenate(fin[:half], axis=1)
    sim_ref[...] = jnp.concatenate(fin[half:], axis=1)

    for r in range(n_rt):
        rs = slice(r * SSM_RT, (r + 1) * SSM_RT)
        sp = jnp.concatenate([zs[c, zrows(r), :] for c in range(nlb)], axis=1).astype(BF16)
        yc_ref[rs, :] += lax.dot_general(sp, cpt_ref[...], (((1,), (1,)), ((), ())),
                                         preferred_element_type=F32)


def _ssm_core(uc, w, win, cpt, a8, n_seq):
    n_rows = uc.shape[0]
    n_chunk = n_rows // n_seq
    mat = lambda r: pl.BlockSpec((None, r, GB_ST2), lambda g: (g, 0, 0))
    blk = pl.BlockSpec((n_rows, GB_IN), lambda g: (0, g))
    st = pl.BlockSpec((n_seq, GB_ST), lambda g: (0, g))
    return pl.pallas_call(
        functools.partial(_ssm_core_kernel, n_seq=n_seq, n_chunk=n_chunk),
        grid=(N_GB,),
        in_specs=[blk, mat(GB_IN), mat(GB_IN), mat(GB_IN), mat(1)],
        out_specs=[blk, st, st],
        out_shape=[
            jax.ShapeDtypeStruct((n_rows, N_GB * GB_IN), F32),
            jax.ShapeDtypeStruct((n_seq, N_GROUPS * SSM_STATE), F32),
            jax.ShapeDtypeStruct((n_seq, N_GROUPS * SSM_STATE), F32),
        ],
        scratch_shapes=[pltpu.VMEM((GB_ST2 // LANES, n_seq * (n_chunk + SCAN_PAD), LANES), F32)],
        compiler_params=_cparams(("arbitrary",)),
        name="ssm_core",
    )(uc, w, win, cpt, a8)


def _glu_tail(x, y, g_ref, dsk_ref, wglu_ref):
    h = _rms(x, g_ref[...])
    z = jax.nn.gelu(y + dsk_ref[...] * h).astype(BF16)
    zz = jnp.dot(z, wglu_ref[...], preferred_element_type=F32)
    return x + zz[:, :D_MODEL] * jax.nn.sigmoid(zz[:, D_MODEL:])


def _ssm_sample_kernel(x_ref, g_ref, bb_ref, c0t_ref, a1_ref, hre_ref, him_ref,
                       y_ref, sre_ref, sim_ref):
    gb = pl.program_id(0)
    x = x_ref[...]
    ms = jnp.mean(x * x, axis=-1, keepdims=True)
    cols = pl.ds(pl.multiple_of(gb * LANES, LANES), LANES)
    u = (x_ref[:, cols] * lax.rsqrt(ms + RMS_EPS) * g_ref[:, cols]).astype(BF16)
    bu = jnp.dot(u, bb_ref[...], preferred_element_type=F32)
    a1 = a1_ref[...]
    a_re = a1[:, :GB_ST]
    a_im = a1[:, GB_ST:]
    h_re = hre_ref[...]
    h_im = him_ref[...]
    s_re = a_re * h_re - a_im * h_im + bu[:, :GB_ST]
    s_im = a_re * h_im + a_im * h_re + bu[:, GB_ST:]
    sre_ref[...] = s_re
    sim_ref[...] = s_im
    s = jnp.concatenate([s_re, s_im], axis=1).astype(BF16)
    y_ref[...] = lax.dot_general(s, c0t_ref[...], (((1,), (1,)), ((), ())),
                                 preferred_element_type=F32)


def _ssm_sample(x, gain, win, c0t, a1, h_re, h_im):
    n, d = x.shape
    st = pl.BlockSpec((n, GB_ST), lambda g: (0, g))
    return pl.pallas_call(
        _ssm_sample_kernel,
        grid=(N_GB,),
        in_specs=[
            _const_spec((n, d)),
            _const_spec((1, d)),
            pl.BlockSpec((None, LANES, GB_ST2), lambda g: (g, CHUNK - 1, 0)),
            pl.BlockSpec((None, LANES, GB_ST2), lambda g: (g, 0, 0)),
            pl.BlockSpec((None, 1, GB_ST2), lambda g: (g, 0, 0)),
            st, st,
        ],
        out_specs=[pl.BlockSpec((n, LANES), lambda g: (0, g)), st, st],
        out_shape=[
            jax.ShapeDtypeStruct((n, d), F32),
            jax.ShapeDtypeStruct((n, N_GROUPS * SSM_STATE), F32),
            jax.ShapeDtypeStruct((n, N_GROUPS * SSM_STATE), F32),
        ],
        compiler_params=_cparams(("arbitrary",)),
        name="ssm_sample",
    )(x, gain, win, c0t, a1, h_re, h_im)


def _glu_sample_kernel(x_ref, y_ref, g_ref, dsk_ref, wglu_ref, xo_ref):
    xo_ref[...] = _glu_tail(x_ref[...], y_ref[...], g_ref, dsk_ref, wglu_ref)


def _glu_sample(x, y, gain, dsk, wglu, layer):
    n, d = x.shape
    return pl.pallas_call(
        _glu_sample_kernel,
        grid=(1,),
        in_specs=[_const_spec((n, d)), _const_spec((n, d)), _const_spec((1, d)),
                  _const_spec((1, d)), pl.BlockSpec((None, d, 2 * d), lambda i: (layer, 0, 0))],
        out_specs=_const_spec((n, d)),
        out_shape=jax.ShapeDtypeStruct((n, d), F32),
        compiler_params=_cparams(("arbitrary",)),
        name="glu_sample",
    )(x, y, gain, dsk, wglu)


def kernel(x_prompt, x_sample, cache_k, cache_v, state_ssm_re, state_ssm_im, state_conv,
           norm_mix, norm_ffn, w_qkv, w_o, q_norm, k_norm, sinks, ssm_a_re, ssm_a_im,
           ssm_log_dt, ssm_b_re, ssm_b_im, ssm_c_re, ssm_c_im, ssm_d, w_glu, w_up, conv_w,
           conv_b, w_down):
    depth = norm_mix.shape[0]
    nb, _, _ = x_prompt.shape
    ns = x_sample.shape[0]
    xp = x_prompt
    xs = x_sample.reshape(ns, D_MODEL)
    row = lambda t: t.reshape(1, -1)

    wqkv = w_qkv.astype(BF16)
    wo = w_o.astype(BF16)
    wo_t = jnp.swapaxes(wo, 1, 2)
    wglu = w_glu.astype(BF16)
    wup = w_up.astype(BF16)
    wdn = w_down.astype(BF16)
    cb = conv_b.reshape(depth, 1, F2)
    n_att = wqkv.shape[0]
    wq_g = wqkv[:, :, :D_MODEL].reshape(n_att, D_MODEL, N_KV, GQA, HEAD_DIM)
    wq_g = jnp.transpose(wq_g, (0, 3, 1, 2, 4)).reshape(n_att, GQA, D_MODEL, KV_DIM)
    wo_g = wo.reshape(n_att, N_KV, GQA, HEAD_DIM, D_MODEL)
    wo_g = jnp.transpose(wo_g, (0, 2, 1, 3, 4)).reshape(n_att, GQA, KV_DIM, D_MODEL)
    ck = cache_k.reshape(n_att, ns, WINDOW, KV_DIM)
    cv = cache_v.reshape(n_att, ns, WINDOW, KV_DIM)

    kps, vps = [], []
    srp, sip, srs, sis = [], [], [], []
    cps = []
    conv_s = state_conv
    glu = None
    for i in range(depth):
        j = i // 2
        gm = row(norm_mix[i])
        if i % 2 == 0:
            qg, kg = row(q_norm[j]), row(k_norm[j])
            xp, kp, vp = _attn_prompt(xp, gm, wqkv, wo_t, jnp.tile(qg, (1, N_HEADS)),
                                      jnp.tile(kg, (1, N_KV)), sinks[j], j)
            xs, ck, cv = _attn_sample(xs, gm, wq_g, wqkv, wo_g, qg, kg, sinks[j], ck, cv, j)
            kps.append(kp.reshape(nb, WINDOW, N_KV, HEAD_DIM))
            vps.append(vp.reshape(nb, WINDOW, N_KV, HEAD_DIM))
        else:
            w, win, cpt, c0t, a8, a1 = _ssm_prep(
                ssm_a_re[j], ssm_a_im[j], ssm_log_dt[j], ssm_b_re[j], ssm_b_im[j],
                ssm_c_re[j], ssm_c_im[j])
            dsk = row(ssm_d[j])
            yc, s_re, s_im = _ssm_core(uc, w, win, cpt, a8, nb)
            glu = (yc, gm, dsk, wglu, j)
            srp.append(s_re.reshape(nb, N_GROUPS, SSM_STATE))
            sip.append(s_im.reshape(nb, N_GROUPS, SSM_STATE))
            ys, t_re, t_im = _ssm_sample(
                xs, gm, win, c0t, a1,
                state_ssm_re[j].reshape(ns, N_GROUPS * SSM_STATE),
                state_ssm_im[j].reshape(ns, N_GROUPS * SSM_STATE))
            xs = _glu_sample(xs, ys, gm, dsk, wglu, j)
            srs.append(t_re.reshape(ns, N_GROUPS, SSM_STATE))
            sis.append(t_im.reshape(ns, N_GROUPS, SSM_STATE))

        gf = row(norm_ffn[i])
        next_gain = row(norm_mix[i + 1]) if (i % 2 == 0 and i + 1 < depth) else None
        res = _ffn_prompt(xp, gf, wup, conv_w, cb, wdn, i, glu=glu, next_gain=next_gain)
        glu = None
        xp, cp = res[0], res[1]
        if next_gain is not None:
            uc = res[2]
        xs, conv_s = _ffn_sample(xs, gf, wup, conv_w, cb, wdn, conv_s, i)
        cps.append(cp)

    k_s, v_s = (t.reshape(n_att, ns, WINDOW, N_KV, HEAD_DIM) for t in (ck, cv))
    return (xp, xs.reshape(ns, 1, D_MODEL),
            jnp.stack(kps), jnp.stack(vps), k_s, v_s,
            jnp.stack(srp), jnp.stack(sip), jnp.stack(srs), jnp.stack(sis),
            jnp.stack(cps), conv_s)
```

```python
import functools

import jax
import jax.numpy as jnp
from jax import lax
from jax.experimental import pallas as pl
from jax.experimental.pallas import tpu as pltpu

F32 = jnp.float32
BF16 = jnp.bfloat16

D_MODEL = 1024
HEAD_DIM = 64
N_HEADS = 16
N_KV = 4
GQA = 4
WINDOW = 128
KV_DIM = N_KV * HEAD_DIM
QKV_DIM = D_MODEL + 2 * KV_DIM
D_FF = 2816
F2 = 2 * D_FF
FF_BLK = 256
N_FF_BLK = D_FF // FF_BLK
SSM_GROUP = 16
N_GROUPS = 64
SSM_STATE = 64
RMS_EPS = 1e-6

LANES = 128
CHUNK = 8
GB_GROUPS = LANES // SSM_GROUP
N_GB = N_GROUPS // GB_GROUPS
GB_IN = CHUNK * LANES
GB_ST = GB_GROUPS * SSM_STATE
GB_ST2 = 2 * GB_ST

VMEM_LIMIT = 56 * 1024 * 1024


def _cparams(sem):
    return pltpu.CompilerParams(dimension_semantics=sem, vmem_limit_bytes=VMEM_LIMIT)


def _rms(x, g):
    ms = jnp.mean(x * x, axis=-1, keepdims=True)
    return x * lax.rsqrt(ms + RMS_EPS) * g


def _head_rms(t, n, g):
    return jnp.concatenate(
        [_rms(t[:, j * HEAD_DIM:(j + 1) * HEAD_DIM], g) for j in range(n)], axis=1)


def _const_spec(shape):
    nd = len(shape)
    return pl.BlockSpec(shape, lambda *_: (0,) * nd)


ATT_TQ = 1024
ATT_SUB = 512
SEG_W = 256
LOG2E = 1.4426950408889634


def _seg_mean_sq(t, seg):
    sq = t * t
    hi = sq.astype(BF16)
    lo = (sq - hi.astype(F32)).astype(BF16)
    out = []
    for c in range(t.shape[1] // SEG_W):
        sl = slice(c * SEG_W, (c + 1) * SEG_W)
        out.append(jnp.dot(hi[:, sl], seg, preferred_element_type=F32)
                   + jnp.dot(lo[:, sl], seg, preferred_element_type=F32))
    return jnp.concatenate(out, axis=1) * (1.0 / HEAD_DIM)


VT_ROWS = HEAD_DIM + 16


def _attn_prompt_kernel(sink_ref, x_ref, g_ref, wqkv_ref, wot_ref, qg_ref, kg_ref,
                        xo_ref, kl_ref, vl_ref, kbuf, krol, vt, lo_scr, ot, qs):
    i = pl.program_id(1)
    keys = 2 * WINDOW

    @pl.when(i == 0)
    def _():
        kbuf[0:WINDOW, :] = jnp.zeros((WINDOW, KV_DIM), BF16)
        krol[0:WINDOW, :] = jnp.zeros((WINDOW, KV_DIM), BF16)
        vt[:, 0:HEAD_DIM, 0:WINDOW] = jnp.zeros((N_KV, HEAD_DIM, WINDOW), BF16)
        vt[:, HEAD_DIM:, :] = jnp.ones((N_KV, VT_ROWS - HEAD_DIM, WINDOW + ATT_TQ), BF16)
        lo_scr[...] = jnp.full(lo_scr.shape, WINDOW, jnp.int32)

    @pl.when(i == 1)
    def _():
        lo_scr[...] = jnp.zeros(lo_scr.shape, jnp.int32)

    si = lax.broadcasted_iota(jnp.int32, (SEG_W, SEG_W), 0) // HEAD_DIM
    sj = lax.broadcasted_iota(jnp.int32, (SEG_W, SEG_W), 1) // HEAD_DIM
    seg = jnp.where(si == sj, 1.0, 0.0).astype(BF16)

    kc = lax.broadcasted_iota(jnp.int32, (keys, 2 * WINDOW), 0)
    qi = lax.broadcasted_iota(jnp.int32, (keys, 2 * WINDOW), 1) % WINDOW
    band = (kc > qi) & (kc <= qi + WINDOW)
    low = lax.broadcasted_iota(jnp.int32, (WINDOW, LANES), 1) < HEAD_DIM
    nt = (((1,), (1,)), ((), ()))

    def project(h0):
        rows = slice(h0, h0 + ATT_SUB)
        krows = slice(WINDOW + h0, WINDOW + h0 + ATT_SUB)
        h = _rms(x_ref[rows, :], g_ref[...]).astype(BF16)
        qkv = jnp.dot(h, wqkv_ref[...], preferred_element_type=F32)
        q = qkv[:, :D_MODEL]
        k = qkv[:, D_MODEL:D_MODEL + KV_DIM]
        v = qkv[:, D_MODEL + KV_DIM:]
        qs[rows, :] = (q * lax.rsqrt(_seg_mean_sq(q, seg) + RMS_EPS) * qg_ref[...]
                       * (HEAD_DIM ** -0.5 * LOG2E))
        kn = k * lax.rsqrt(_seg_mean_sq(k, seg) + RMS_EPS) * kg_ref[...]
        if h0 + ATT_SUB == ATT_TQ:
            kl_ref[...] = kn[ATT_SUB - WINDOW:]
            vl_ref[...] = v[ATT_SUB - WINDOW:]
        kbuf[krows, :] = kn.astype(BF16)
        for c in range(KV_DIM // LANES):
            sl = slice(c * LANES, (c + 1) * LANES)
            krol[krows, sl] = pltpu.roll(kn[:, sl], HEAD_DIM, 1).astype(BF16)
        v_t = v.T
        for kvh in range(N_KV):
            vt[kvh, 0:HEAD_DIM, krows] = v_t[kvh * HEAD_DIM:(kvh + 1) * HEAD_DIM].astype(BF16)

    def attend(r0):
        valid = band & (kc >= lo_scr[...]) if r0 == 0 else band
        combos = []
        for kvh in range(N_KV):
            khalf = kvh % 2
            combos.append((kvh, (khalf, khalf + 2), kbuf))
            combos.append((kvh, (1 - khalf, 3 - khalf), krol))
        scores = []
        for kvh, pair, kref in combos:
            kcol = slice((kvh // 2) * LANES, (kvh // 2 + 1) * LANES)
            qm = []
            for g in pair:
                qcol = slice((kvh * 2 + g // 2) * LANES, (kvh * 2 + g // 2 + 1) * LANES)
                qm.append(jnp.where(low if g % 2 == 0 else ~low, qs[r0:r0 + WINDOW, qcol], 0.0))
            scores.append(lax.dot_general(kref[r0:r0 + keys, kcol],
                                          jnp.concatenate(qm, axis=0).astype(BF16), nt,
                                          preferred_element_type=F32))
        probs = []
        for (kvh, pair, _), s in zip(combos, scores):
            sink_row = jnp.concatenate(
                [jnp.full((1, WINDOW), sink_ref[kvh * GQA + g] * LOG2E, F32) for g in pair],
                axis=1)
            s = jnp.where(valid, s, -jnp.inf)
            m = jnp.maximum(jnp.max(s, axis=0, keepdims=True), sink_row)
            probs.append((jnp.exp2(s - m).astype(BF16), jnp.exp2(sink_row - m)))
        outs = [jnp.dot(vt[kvh, :, r0:r0 + keys], e, preferred_element_type=F32)
                for (kvh, _, _), (e, _) in zip(combos, probs)]
        for (kvh, pair, _), (_, e_sink), o in zip(combos, probs, outs):
            rden = 1.0 / (o[HEAD_DIM:HEAD_DIM + 1] + e_sink)
            on = (o[:HEAD_DIM] * rden).astype(BF16)
            for n, g in enumerate(pair):
                hd = kvh * GQA + g
                ot[hd * HEAD_DIM:(hd + 1) * HEAD_DIM, r0:r0 + WINDOW] = (
                    on[:, n * WINDOW:(n + 1) * WINDOW])

    def output(h0):
        rows = slice(h0, h0 + ATT_SUB)
        out_t = jnp.dot(wot_ref[...], ot[:, rows], preferred_element_type=F32)
        xo_ref[rows, :] = x_ref[rows, :] + out_t.T

    n_sub = ATT_TQ // ATT_SUB
    blocks = ATT_SUB // WINDOW
    project(0)
    for t in range(n_sub):
        h0 = t * ATT_SUB
        for blk in range(blocks):
            attend(h0 + blk * WINDOW)
            if blk == 0 and t + 1 < n_sub:
                project(h0 + ATT_SUB)
            if blk == 1 and t > 0:
                output(h0 - ATT_SUB)
    output(ATT_TQ - ATT_SUB)

    kbuf[0:WINDOW, :] = kbuf[ATT_TQ:, :]
    krol[0:WINDOW, :] = krol[ATT_TQ:, :]
    vt[:, 0:HEAD_DIM, 0:WINDOW] = vt[:, 0:HEAD_DIM, ATT_TQ:]


def _attn_prompt(x, gain, wqkv, wo_t, qg, kg, sinks, layer):
    b, l, d = x.shape
    wsel = lambda bi, i: (layer, 0, 0)
    tok = pl.BlockSpec((None, ATT_TQ, d), lambda bi, i: (bi, i, 0))
    last = pl.BlockSpec((None, WINDOW, KV_DIM), lambda bi, i: (bi, 0, 0))
    kvbuf = pltpu.VMEM((WINDOW + ATT_TQ, KV_DIM), BF16)
    return pl.pallas_call(
        _attn_prompt_kernel,
        grid=(b, l // ATT_TQ),
        in_specs=[
            pl.BlockSpec(memory_space=pltpu.SMEM),
            tok,
            _const_spec((1, d)),
            pl.BlockSpec((None, d, QKV_DIM), wsel),
            pl.BlockSpec((None, d, d), wsel),
            _const_spec((1, d)),
            _const_spec((1, KV_DIM)),
        ],
        out_specs=[tok, last, last],
        out_shape=[
            jax.ShapeDtypeStruct((b, l, d), F32),
            jax.ShapeDtypeStruct((b, WINDOW, KV_DIM), F32),
            jax.ShapeDtypeStruct((b, WINDOW, KV_DIM), F32),
        ],
        scratch_shapes=[
            kvbuf, kvbuf,
            pltpu.VMEM((N_KV, VT_ROWS, WINDOW + ATT_TQ), BF16),
            pltpu.VMEM((2 * WINDOW, 2 * WINDOW), jnp.int32),
            pltpu.VMEM((d, ATT_TQ), BF16),
            pltpu.VMEM((ATT_TQ, d), F32),
        ],
        compiler_params=_cparams(("arbitrary", "arbitrary")),
        name="attn_prompt",
    )(sinks, x, gain, wqkv, wo_t, qg, kg)


ATS_TB = 16


def _attn_sample_kernel(sink_ref, x_ref, g_ref, wq_ref, wkv_ref, wo_ref, qg_ref, kg_ref,
                        ck_ref, cv_ref, xo_ref, ko_ref, vo_ref, r_scr):
    x = x_ref[...]
    h = _rms(x, g_ref[...]).astype(BF16)
    kv = jnp.dot(h, wkv_ref[...], preferred_element_type=F32)
    kn = _head_rms(kv[:, :KV_DIM], N_KV, kg_ref[...])
    v = kv[:, KV_DIM:]
    qg = qg_ref[...]
    q_g = [_head_rms(jnp.dot(h, wq_ref[g], preferred_element_type=F32), N_KV, qg)
           * (HEAD_DIM ** -0.5) for g in range(GQA)]

    rows = GQA * N_KV
    rkv = lax.broadcasted_iota(jnp.int32, (rows, KV_DIM), 0) % N_KV
    lkv = lax.broadcasted_iota(jnp.int32, (rows, KV_DIM), 1) // HEAD_DIM
    own = rkv == lkv
    colj = lax.broadcasted_iota(jnp.int32, (rows, WINDOW), 1)
    sink_col = jnp.concatenate(
        [jnp.full((1, 1), sink_ref[(r % N_KV) * GQA + r // N_KV], F32) for r in range(rows)],
        axis=0)

    for n in range(ATS_TB):
        ko_ref[n, 0:WINDOW - 1, :] = ck_ref[n, 1:WINDOW, :]
        ko_ref[n, WINDOW - 1:WINDOW, :] = kn[n:n + 1]
        vo_ref[n, 0:WINDOW - 1, :] = cv_ref[n, 1:WINDOW, :]
        vo_ref[n, WINDOW - 1:WINDOW, :] = v[n:n + 1]
    qbds = []
    for n in range(ATS_TB):
        qbd = jnp.concatenate(
            [jnp.broadcast_to(q_g[g][n:n + 1], (N_KV, KV_DIM)) for g in range(GQA)], axis=0)
        qbds.append(jnp.where(own, qbd, 0.0))
    scores = [lax.dot_general(qbds[n].astype(BF16), ck_ref[n].astype(BF16),
                              (((1,), (1,)), ((), ())), preferred_element_type=F32)
              for n in range(ATS_TB)]
    probs = []
    for n in range(ATS_TB):
        s = jnp.where(colj >= 1, scores[n], -jnp.inf)
        s_new = jnp.sum(qbds[n] * kn[n:n + 1], axis=-1, keepdims=True)
        m = jnp.maximum(jnp.maximum(jnp.max(s, axis=-1, keepdims=True), s_new), sink_col)
        e = jnp.exp(s - m)
        e_new = jnp.exp(s_new - m)
        rden = 1.0 / (jnp.sum(e, axis=-1, keepdims=True) + e_new + jnp.exp(sink_col - m))
        probs.append(((e * rden).astype(BF16), e_new * rden))
    outs = [jnp.dot(probs[n][0], cv_ref[n].astype(BF16), preferred_element_type=F32)
            for n in range(ATS_TB)]
    for n in range(ATS_TB):
        o = jnp.where(own, outs[n] + probs[n][1] * v[n:n + 1], 0.0)
        for g in range(GQA):
            r_scr[g, n:n + 1, :] = jnp.sum(o[g * N_KV:(g + 1) * N_KV], axis=0, keepdims=True)

    out = x
    for g in range(GQA):
        out = out + jnp.dot(r_scr[g].astype(BF16), wo_ref[g], preferred_element_type=F32)
    xo_ref[...] = out


def _attn_sample(x, gain, wq_g, wqkv, wo_g, qg, kg, sinks, cache_k, cache_v, layer):
    n, d = x.shape
    tok = pl.BlockSpec((ATS_TB, d), lambda i: (i, 0))
    cache = pl.BlockSpec((None, ATS_TB, WINDOW, KV_DIM), lambda i: (layer, i, 0, 0))
    in_specs = [
        pl.BlockSpec(memory_space=pltpu.SMEM),
        tok,
        _const_spec((1, d)),
        pl.BlockSpec((None, GQA, d, KV_DIM), lambda i: (layer, 0, 0, 0)),
        pl.BlockSpec((None, d, 2 * KV_DIM), lambda i: (layer, 0, D_MODEL // (2 * KV_DIM))),
        pl.BlockSpec((None, GQA, KV_DIM, d), lambda i: (layer, 0, 0, 0)),
        _const_spec((1, HEAD_DIM)),
        _const_spec((1, HEAD_DIM)),
        cache,
        cache,
    ]
    args = [sinks, x, gain, wq_g, wqkv, wo_g, qg, kg, cache_k, cache_v]
    return pl.pallas_call(
        _attn_sample_kernel,
        grid=(n // ATS_TB,),
        in_specs=in_specs,
        out_specs=[tok, cache, cache],
        out_shape=[
            jax.ShapeDtypeStruct((n, d), F32),
            jax.ShapeDtypeStruct(cache_k.shape, F32),
            jax.ShapeDtypeStruct(cache_v.shape, F32),
        ],
        scratch_shapes=[pltpu.VMEM((GQA, ATS_TB, KV_DIM), F32)],
        input_output_aliases={len(args) - 2: 1, len(args) - 1: 2},
        compiler_params=_cparams(("arbitrary",)),
        name="attn_sample",
    )(*args)


FFN_TM = 512
SSM_TC = FFN_TM // CHUNK
CARRY = 8
CONV_RC = 64


def _ffn_prompt_kernel(*refs, glu_in, uc_out):
    refs = list(refs)
    x_ref = refs.pop(0)
    if glu_in:
        yc_ref, gm_ref, dsk_ref, wglu_ref = (refs.pop(0) for _ in range(4))
    g_ref, wup_ref, cw_ref, cb_ref, wdn_ref = (refs.pop(0) for _ in range(5))
    if uc_out:
        gn_ref = refs.pop(0)
    xo_ref, cs_ref = refs.pop(0), refs.pop(0)
    if uc_out:
        uc_ref = refs.pop(0)
    carry, upbuf, act, hs, oscr = refs

    i = pl.program_id(1)
    hm = FFN_TM // 2
    nlb = FF_BLK // LANES
    dlb = D_MODEL // LANES

    @pl.when(i == 0)
    def _():
        carry[...] = jnp.zeros_like(carry)

    if glu_in:
        for gb in range(N_GB):
            for t in range(CHUNK):
                oscr[gb, pl.ds(t, SSM_TC, stride=CHUNK), :] = (
                    yc_ref[:, gb * GB_IN + t * LANES:gb * GB_IN + (t + 1) * LANES])
        y = jnp.concatenate([oscr[gb] for gb in range(dlb)], axis=1)
        xo_ref[...] = _glu_tail(x_ref[...], y, gm_ref, dsk_ref, wglu_ref)
        xin_ref = xo_ref
    else:
        xin_ref = x_ref

    hs[...] = _rms(xin_ref[...], g_ref[...]).astype(BF16)

    def up_into(j, slot):
        for half, c0 in enumerate((j * FF_BLK, D_FF + j * FF_BLK)):
            cols = pl.ds(c0, FF_BLK)
            up = jnp.dot(hs[...], wup_ref[:, cols], preferred_element_type=F32)
            for lb in range(nlb):
                lcols = pl.ds(c0 + lb * LANES, LANES)
                buf = upbuf.at[slot, half, lb]
                buf[0:CARRY, :] = carry[:, lcols]
                buf[CARRY:, :] = up[:, lb * LANES:(lb + 1) * LANES]
                carry[:, lcols] = buf[FFN_TM:, :]

    def conv(slot, half, lb, c0, m0):
        lcols = pl.ds(c0 + lb * LANES, LANES)
        buf = upbuf.at[slot, half, lb]
        tap = lambda off: buf[pl.ds(CARRY + off + 2 * m0, CONV_RC, stride=2), :]
        em2, om1, e0, o1 = tap(-2), tap(-1), tap(0), tap(1)
        w0, w1, w2 = cw_ref[0:1, lcols], cw_ref[1:2, lcols], cw_ref[2:3, lcols]
        cb = cb_ref[:, lcols]
        return (cb + w0 * em2 + w1 * om1 + w2 * e0, cb + w0 * om1 + w1 * e0 + w2 * o1)

    for j in range(N_FF_BLK):
        up_into(j, j % 2)
        for m0 in range(0, hm, CONV_RC):
            for lb in range(nlb):
                cg = conv(j % 2, 0, lb, j * FF_BLK, m0)
                cv = conv(j % 2, 1, lb, D_FF + j * FF_BLK, m0)
                acol = pl.ds(j * FF_BLK + lb * LANES, LANES)
                for p in range(2):
                    act[p * hm + m0:p * hm + m0 + CONV_RC, acol] = (
                        jax.nn.silu(cg[p]) * cv[p]).astype(BF16)

    out = jnp.dot(act[...], wdn_ref[...], preferred_element_type=F32)
    for lb in range(dlb):
        for p in range(2):
            oscr[lb, pl.ds(p, hm, stride=2), :] = out[p * hm:(p + 1) * hm,
                                                      lb * LANES:(lb + 1) * LANES]
    xo = xin_ref[...] + jnp.concatenate([oscr[lb] for lb in range(dlb)], axis=1)
    xo_ref[...] = xo
    cs_ref[...] = carry[CARRY - 2:CARRY, :]

    if uc_out:
        h2 = _rms(xo, gn_ref[...])
        for gb in range(N_GB):
            oscr[gb] = h2[:, gb * LANES:(gb + 1) * LANES]
        for gb in range(N_GB):
            for t in range(CHUNK):
                uc_ref[:, gb * GB_IN + t * LANES:gb * GB_IN + (t + 1) * LANES] = (
                    oscr[gb, pl.ds(t, SSM_TC, stride=CHUNK), :].astype(BF16))


def _ffn_prompt(x, gain, wup, cw, cb, wdn, layer, glu=None, next_gain=None):
    b, l, d = x.shape
    nt = l // FFN_TM
    tok = pl.BlockSpec((None, FFN_TM, d), lambda bi, i: (bi, i, 0))
    chunks = pl.BlockSpec((SSM_TC, CHUNK * d), lambda bi, i: (bi * nt + i, 0))
    wsel = lambda bi, i: (layer, 0, 0)
    args, in_specs = [x], [tok]
    if glu is not None:
        yc, gm, dsk, wglu, mixer_layer = glu
        args += [yc, gm, dsk, wglu]
        in_specs += [chunks, _const_spec((1, d)), _const_spec((1, d)),
                     pl.BlockSpec((None, d, 2 * d), lambda bi, i: (mixer_layer, 0, 0),
                                  pipeline_mode=pl.Buffered(1))]
    args += [gain, wup, cw, cb, wdn]
    in_specs += [
        _const_spec((1, d)),
        pl.BlockSpec((None, d, F2), wsel, pipeline_mode=pl.Buffered(1)),
        pl.BlockSpec((None, 3, F2), wsel),
        pl.BlockSpec((None, 1, F2), wsel),
        pl.BlockSpec((None, D_FF, d), wsel, pipeline_mode=pl.Buffered(1)),
    ]
    out_specs = [tok, pl.BlockSpec((None, 2, F2), lambda bi, i: (bi, 0, 0))]
    out_shape = [jax.ShapeDtypeStruct((b, l, d), F32), jax.ShapeDtypeStruct((b, 2, F2), F32)]
    if next_gain is not None:
        args.append(next_gain)
        in_specs.append(_const_spec((1, d)))
        out_specs.append(chunks)
        out_shape.append(jax.ShapeDtypeStruct((b * l // CHUNK, CHUNK * d), BF16))
    return pl.pallas_call(
        functools.partial(_ffn_prompt_kernel, glu_in=glu is not None,
                          uc_out=next_gain is not None),
        grid=(b, nt),
        in_specs=in_specs,
        out_specs=out_specs,
        out_shape=out_shape,
        scratch_shapes=[
            pltpu.VMEM((CARRY, F2), F32),
            pltpu.VMEM((2, 2, FF_BLK // LANES, CARRY + FFN_TM, LANES), F32),
            pltpu.VMEM((FFN_TM, D_FF), BF16),
            pltpu.VMEM((FFN_TM, d), BF16),
            pltpu.VMEM((d // LANES, FFN_TM, LANES), F32),
        ],
        compiler_params=_cparams(("arbitrary", "arbitrary")),
        name="ffn_prompt",
    )(*args)


FFS_BLK = D_FF // 2
N_FFS_BLK = D_FF // FFS_BLK


def _ffn_sample_kernel(x_ref, g_ref, w_ref, cw_ref, cb_ref, wdn_ref, sc_ref,
                       xo_ref, cs_ref, gate):
    j = pl.program_id(0)

    @pl.when(j == 0)
    def _():
        xo_ref[...] = x_ref[...]

    h = _rms(x_ref[...], g_ref[...]).astype(BF16)
    up = jnp.dot(h, w_ref[...], preferred_element_type=F32)
    b0 = sc_ref[:, 0, :]
    b1 = sc_ref[:, 1, :]
    c = cb_ref[...]
    c = c + cw_ref[0:1, :] * b0
    c = c + cw_ref[1:2, :] * b1
    c = c + cw_ref[2:3, :] * up
    cs_ref[...] = jnp.stack([b1, up], axis=1)

    @pl.when(j < N_FFS_BLK)
    def _():
        gate[j] = jax.nn.silu(c)

    @pl.when(j >= N_FFS_BLK)
    def _():
        a = (gate[j - N_FFS_BLK] * c).astype(BF16)
        xo_ref[...] += jnp.dot(a, wdn_ref[...], preferred_element_type=F32)


def _ffn_sample(x, gain, wup, cw, cb, wdn, state, layer):
    n, d = x.shape
    nb = N_FFS_BLK
    sblk = pl.BlockSpec((None, n, 2, FFS_BLK), lambda j: (layer, 0, 0, j))
    in_specs = [
        _const_spec((n, d)),
        _const_spec((1, d)),
        pl.BlockSpec((None, d, FFS_BLK), lambda j: (layer, 0, j)),
        pl.BlockSpec((None, 3, FFS_BLK), lambda j: (layer, 0, j)),
        pl.BlockSpec((None, 1, FFS_BLK), lambda j: (layer, 0, j)),
        pl.BlockSpec((None, FFS_BLK, d), lambda j: (layer, jnp.maximum(j - nb, 0), 0)),
        sblk,
    ]
    args = [x, gain, wup, cw, cb, wdn, state]
    return pl.pallas_call(
        _ffn_sample_kernel,
        grid=(2 * nb,),
        in_specs=in_specs,
        out_specs=[_const_spec((n, d)), sblk],
        out_shape=[
            jax.ShapeDtypeStruct((n, d), F32),
            jax.ShapeDtypeStruct(state.shape, F32),
        ],
        scratch_shapes=[pltpu.VMEM((nb, n, FFS_BLK), F32)],
        input_output_aliases={len(args) - 1: 1},
        compiler_params=_cparams(("arbitrary",)),
        name="ffn_sample",
    )(*args)


def _tile_states(t):
    t2 = jnp.concatenate([t, t], axis=1)
    return jnp.concatenate([t2] * (GB_GROUPS // 2), axis=1)


def _ssm_prep_kernel(are_ref, aim_ref, dt_ref, bre_ref, bim_ref, cre_ref, cim_ref,
                     aflat_ref, iflat_ref, dtflat_ref,
                     w_ref, win_ref, cpt_ref, c0t_ref, a8_ref, a1_ref):
    a_re = are_ref[...]
    a_im = aim_ref[...]
    dt = jnp.exp(dt_ref[...])

    mag = jnp.exp(a_re * dt)
    ab_re, ab_im = mag * jnp.cos(a_im * dt), mag * jnp.sin(a_im * dt)
    pw = [(jnp.ones_like(ab_re), jnp.zeros_like(ab_re)), (ab_re, ab_im)]
    for _ in range(CHUNK - 1):
        p_re, p_im = pw[-1]
        pw.append((p_re * ab_re - p_im * ab_im, p_re * ab_im + p_im * ab_re))
    den = a_re * a_re + a_im * a_im
    z_re = ((ab_re - 1.0) * a_re + ab_im * a_im) / den
    z_im = (ab_im * a_re - (ab_re - 1.0) * a_im) / den
    b_re = bre_ref[...]
    b_im = bim_ref[...]
    bb_re = z_re * b_re - z_im * b_im
    bb_im = z_re * b_im + z_im * b_re
    c_re = cre_ref[...]
    c_im = cim_ref[...]

    rg = lax.broadcasted_iota(jnp.int32, (LANES, GB_ST2), 0) // SSM_GROUP
    lg = (lax.broadcasted_iota(jnp.int32, (LANES, GB_ST2), 1) % GB_ST) // SSM_STATE
    own = rg == lg

    def blockdiag(t_re, t_im):
        full = jnp.concatenate([_tile_states(t_re), _tile_states(t_im)], axis=1)
        return jnp.where(own, full, 0.0)

    c0t = blockdiag(c_re, -c_im)
    c0t_ref[...] = c0t.astype(BF16)

    lfull = []
    for d in range(CHUNK):
        if d == 0:
            l_re, l_im = bb_re, bb_im
        else:
            e_re, e_im = pw[d]
            l_re = e_re * bb_re - e_im * bb_im
            l_im = e_re * bb_im + e_im * bb_re
        lfull.append(blockdiag(l_re, l_im))

    def split(t):
        hi = t.astype(BF16)
        return hi, (t - hi.astype(F32)).astype(BF16)

    def dot_nt(a, b):
        return lax.dot_general(a, b, (((1,), (1,)), ((), ())), preferred_element_type=F32)

    c_hi, c_lo = split(c0t)
    kd = []
    for l in lfull:
        l_hi, l_lo = split(l)
        kd.append(dot_nt(l_hi, c_hi) + dot_nt(l_hi, c_lo) + dot_nt(l_lo, c_hi))
    zero = jnp.zeros((LANES, LANES), F32)
    for s in range(CHUNK):
        row = jnp.concatenate([kd[t - s] if t >= s else zero for t in range(CHUNK)], axis=1)
        w_ref[s * LANES:(s + 1) * LANES, :] = row.astype(BF16)
        win_ref[s * LANES:(s + 1) * LANES, :] = lfull[CHUNK - 1 - s].astype(BF16)
    for t in range(CHUNK):
        e_re, e_im = pw[t + 1]
        cpt_ref[t * LANES:(t + 1) * LANES, :] = blockdiag(
            c_re * e_re - c_im * e_im, -c_re * e_im - c_im * e_re).astype(BF16)

    af = aflat_ref[...]
    ai = iflat_ref[...]
    dtf = jnp.exp(dtflat_ref[...])
    magf = jnp.exp(af * dtf)
    f_re, f_im = magf * jnp.cos(ai * dtf), magf * jnp.sin(ai * dtf)
    a1_ref[...] = jnp.concatenate([f_re, f_im], axis=1)
    for _ in range(CHUNK.bit_length() - 1):
        f_re, f_im = f_re * f_re - f_im * f_im, 2.0 * f_re * f_im
    a8_ref[...] = jnp.concatenate([f_re, f_im], axis=1)


def _ssm_prep(a_re, a_im, log_dt, b_re, b_im, c_re, c_im):
    rep = lambda t: jnp.repeat(t, SSM_GROUP, axis=0)
    are_x = rep(a_re)
    aim_x = rep(a_im)
    dt_x = rep(log_dt[:, None])
    bt_re = jnp.transpose(b_re, (0, 2, 1)).reshape(N_GROUPS * SSM_GROUP, SSM_STATE)
    bt_im = jnp.transpose(b_im, (0, 2, 1)).reshape(N_GROUPS * SSM_GROUP, SSM_STATE)
    cr = c_re.reshape(N_GROUPS * SSM_GROUP, SSM_STATE)
    ci = c_im.reshape(N_GROUPS * SSM_GROUP, SSM_STATE)
    aflat = a_re.reshape(N_GB, 1, GB_ST)
    iflat = a_im.reshape(N_GB, 1, GB_ST)
    dtflat = jnp.repeat(log_dt, SSM_STATE).reshape(N_GB, 1, GB_ST)
    rows = pl.BlockSpec((LANES, SSM_STATE), lambda g: (g, 0))
    flat = pl.BlockSpec((None, 1, GB_ST), lambda g: (g, 0, 0))
    big = pl.BlockSpec((None, GB_IN, GB_ST2), lambda g: (g, 0, 0))
    vec = pl.BlockSpec((None, 1, GB_ST2), lambda g: (g, 0, 0))
    return pl.pallas_call(
        _ssm_prep_kernel,
        grid=(N_GB,),
        in_specs=[rows, rows, pl.BlockSpec((LANES, 1), lambda g: (g, 0)),
                  rows, rows, rows, rows, flat, flat, flat],
        out_specs=[big, big, big,
                   pl.BlockSpec((None, LANES, GB_ST2), lambda g: (g, 0, 0)), vec, vec],
        out_shape=[
            jax.ShapeDtypeStruct((N_GB, GB_IN, GB_IN), BF16),
            jax.ShapeDtypeStruct((N_GB, GB_IN, GB_ST2), BF16),
            jax.ShapeDtypeStruct((N_GB, GB_IN, GB_ST2), BF16),
            jax.ShapeDtypeStruct((N_GB, LANES, GB_ST2), BF16),
            jax.ShapeDtypeStruct((N_GB, 1, GB_ST2), F32),
            jax.ShapeDtypeStruct((N_GB, 1, GB_ST2), F32),
        ],
        compiler_params=_cparams(("arbitrary",)),
        name="ssm_prep",
    )(are_x, aim_x, dt_x, bt_re, bt_im, cr, ci, aflat, iflat, dtflat)


SSM_RT = 256
SCAN_PAD = 8


def _ssm_core_kernel(uc_ref, w_ref, win_ref, cpt_ref, a8_ref, yc_ref, sre_ref, sim_ref, zs,
                     *, n_seq, n_chunk):
    n_rows = n_seq * n_chunk
    nlb = GB_ST2 // LANES
    half = nlb // 2
    pitch = n_chunk + SCAN_PAD

    def zrows(r):
        seq, off = divmod(r * SSM_RT, n_chunk)
        return slice(seq * pitch + off, seq * pitch + off + SSM_RT)

    for r in range(n_rows // SSM_RT):
        rs = slice(r * SSM_RT, (r + 1) * SSM_RT)
        z = jnp.dot(uc_ref[rs, :], win_ref[...], preferred_element_type=F32)
        for c in range(nlb):
            zs[c, zrows(r), :] = z[:, c * LANES:(c + 1) * LANES]

    a8 = a8_ref[...]
    a_bl = [jnp.broadcast_to(a8[:, c * LANES:(c + 1) * LANES], (n_seq, LANES))
            for c in range(nlb)]

    def step(k, carry):
        rows = pl.ds(k, n_seq, stride=pitch)
        new = []
        for c in range(half):
            s_re, s_im = carry[c], carry[half + c]
            z_re = zs[c, rows, :]
            z_im = zs[half + c, rows, :]
            zs[c, rows, :] = s_re
            zs[half + c, rows, :] = s_im
            new.append((a_bl[c] * s_re - a_bl[half + c] * s_im + z_re,
                        a_bl[c] * s_im + a_bl[half + c] * s_re + z_im))
        return tuple(n[0] for n in new) + tuple(n[1] for n in new)

    n_rt = n_rows // SSM_RT
    steps_per = n_chunk // n_rt
    fin = (jnp.zeros((n_seq, LANES), F32),) * nlb
    for r in range(n_rt):
        rs = slice(r * SSM_RT, (r + 1) * SSM_RT)
        for m in range(CHUNK // 2):
            kin = (2 * m + 2) * LANES
            oc = slice(2 * m * LANES, (2 * m + 2) * LANES)
            yc_ref[rs, oc] = jnp.dot(uc_ref[rs, :kin], w_ref[:kin, oc],
                                     preferred_element_type=F32)
        for k in range(r * steps_per, (r + 1) * steps_per):
            fin = step(k, fin)
    sre_ref[...] = jnp.concatenate(fin[:half], axis=1)
    sim_ref[...] = jnp.concatenate(fin[half:], axis=1)

    for r in range(n_rt):
        rs = slice(r * SSM_RT, (r + 1) * SSM_RT)
        sp = jnp.concatenate([zs[c, zrows(r), :] for c in range(nlb)], axis=1).astype(BF16)
        yc_ref[rs, :] += lax.dot_general(sp, cpt_ref[...], (((1,), (1,)), ((), ())),
                                         preferred_element_type=F32)


def _ssm_core(uc, w, win, cpt, a8, n_seq):
    n_rows = uc.shape[0]
    n_chunk = n_rows // n_seq
    mat = lambda r: pl.BlockSpec((None, r, GB_ST2), lambda g: (g, 0, 0))
    blk = pl.BlockSpec((n_rows, GB_IN), lambda g: (0, g))
    st = pl.BlockSpec((n_seq, GB_ST), lambda g: (0, g))
    return pl.pallas_call(
        functools.partial(_ssm_core_kernel, n_seq=n_seq, n_chunk=n_chunk),
        grid=(N_GB,),
        in_specs=[blk, mat(GB_IN), mat(GB_IN), mat(GB_IN), mat(1)],
        out_specs=[blk, st, st],
        out_shape=[
            jax.ShapeDtypeStruct((n_rows, N_GB * GB_IN), F32),
            jax.ShapeDtypeStruct((n_seq, N_GROUPS * SSM_STATE), F32),
            jax.ShapeDtypeStruct((n_seq, N_GROUPS * SSM_STATE), F32),
        ],
        scratch_shapes=[pltpu.VMEM((GB_ST2 // LANES, n_seq * (n_chunk + SCAN_PAD), LANES), F32)],
        compiler_params=_cparams(("arbitrary",)),
        name="ssm_core",
    )(uc, w, win, cpt, a8)


def _glu_tail(x, y, g_ref, dsk_ref, wglu_ref):
    h = _rms(x, g_ref[...])
    z = jax.nn.gelu(y + dsk_ref[...] * h).astype(BF16)
    zz = jnp.dot(z, wglu_ref[...], preferred_element_type=F32)
    return x + zz[:, :D_MODEL] * jax.nn.sigmoid(zz[:, D_MODEL:])


def _ssm_sample_kernel(x_ref, g_ref, bb_ref, c0t_ref, a1_ref, hre_ref, him_ref,
                       y_ref, sre_ref, sim_ref):
    gb = pl.program_id(0)
    x = x_ref[...]
    ms = jnp.mean(x * x, axis=-1, keepdims=True)
    cols = pl.ds(pl.multiple_of(gb * LANES, LANES), LANES)
    u = (x_ref[:, cols] * lax.rsqrt(ms + RMS_EPS) * g_ref[:, cols]).astype(BF16)
    bu = jnp.dot(u, bb_ref[...], preferred_element_type=F32)
    a1 = a1_ref[...]
    a_re = a1[:, :GB_ST]
    a_im = a1[:, GB_ST:]
    h_re = hre_ref[...]
    h_im = him_ref[...]
    s_re = a_re * h_re - a_im * h_im + bu[:, :GB_ST]
    s_im = a_re * h_im + a_im * h_re + bu[:, GB_ST:]
    sre_ref[...] = s_re
    sim_ref[...] = s_im
    s = jnp.concatenate([s_re, s_im], axis=1).astype(BF16)
    y_ref[...] = lax.dot_general(s, c0t_ref[...], (((1,), (1,)), ((), ())),
                                 preferred_element_type=F32)


def _ssm_sample(x, gain, win, c0t, a1, h_re, h_im):
    n, d = x.shape
    st = pl.BlockSpec((n, GB_ST), lambda g: (0, g))
    return pl.pallas_call(
        _ssm_sample_kernel,
        grid=(N_GB,),
        in_specs=[
            _const_spec((n, d)),
            _const_spec((1, d)),
            pl.BlockSpec((None, LANES, GB_ST2), lambda g: (g, CHUNK - 1, 0)),
            pl.BlockSpec((None, LANES, GB_ST2), lambda g: (g, 0, 0)),
            pl.BlockSpec((None, 1, GB_ST2), lambda g: (g, 0, 0)),
            st, st,
        ],
        out_specs=[pl.BlockSpec((n, LANES), lambda g: (0, g)), st, st],
        out_shape=[
            jax.ShapeDtypeStruct((n, d), F32),
            jax.ShapeDtypeStruct((n, N_GROUPS * SSM_STATE), F32),
            jax.ShapeDtypeStruct((n, N_GROUPS * SSM_STATE), F32),
        ],
        compiler_params=_cparams(("arbitrary",)),
        name="ssm_sample",
    )(x, gain, win, c0t, a1, h_re, h_im)


def _glu_sample_kernel(x_ref, y_ref, g_ref, dsk_ref, wglu_ref, xo_ref):
    xo_ref[...] = _glu_tail(x_ref[...], y_ref[...], g_ref, dsk_ref, wglu_ref)


def _glu_sample(x, y, gain, dsk, wglu, layer):
    n, d = x.shape
    return pl.pallas_call(
        _glu_sample_kernel,
        grid=(1,),
        in_specs=[_const_spec((n, d)), _const_spec((n, d)), _const_spec((1, d)),
                  _const_spec((1, d)), pl.BlockSpec((None, d, 2 * d), lambda i: (layer, 0, 0))],
        out_specs=_const_spec((n, d)),
        out_shape=jax.ShapeDtypeStruct((n, d), F32),
        compiler_params=_cparams(("arbitrary",)),
        name="glu_sample",
    )(x, y, gain, dsk, wglu)


def kernel(x_prompt, x_sample, cache_k, cache_v, state_ssm_re, state_ssm_im, state_conv,
           norm_mix, norm_ffn, w_qkv, w_o, q_norm, k_norm, sinks, ssm_a_re, ssm_a_im,
           ssm_log_dt, ssm_b_re, ssm_b_im, ssm_c_re, ssm_c_im, ssm_d, w_glu, w_up, conv_w,
           conv_b, w_down):
    depth = norm_mix.shape[0]
    nb, _, _ = x_prompt.shape
    ns = x_sample.shape[0]
    xp = x_prompt
    xs = x_sample.reshape(ns, D_MODEL)
    row = lambda t: t.reshape(1, -1)

    wqkv = w_qkv.astype(BF16)
    wo = w_o.astype(BF16)
    wo_t = jnp.swapaxes(wo, 1, 2)
    wglu = w_glu.astype(BF16)
    wup = w_up.astype(BF16)
    wdn = w_down.astype(BF16)
    cb = conv_b.reshape(depth, 1, F2)
    n_att = wqkv.shape[0]
    wq_g = wqkv[:, :, :D_MODEL].reshape(n_att, D_MODEL, N_KV, GQA, HEAD_DIM)
    wq_g = jnp.transpose(wq_g, (0, 3, 1, 2, 4)).reshape(n_att, GQA, D_MODEL, KV_DIM)
    wo_g = wo.reshape(n_att, N_KV, GQA, HEAD_DIM, D_MODEL)
    wo_g = jnp.transpose(wo_g, (0, 2, 1, 3, 4)).reshape(n_att, GQA, KV_DIM, D_MODEL)
    ck = cache_k.reshape(n_att, ns, WINDOW, KV_DIM)
    cv = cache_v.reshape(n_att, ns, WINDOW, KV_DIM)

    kps, vps = [], []
    srp, sip, srs, sis = [], [], [], []
    cps = []
    conv_s = state_conv
    glu = None
    for i in range(depth):
        j = i // 2
        gm = row(norm_mix[i])
        if i % 2 == 0:
            qg, kg = row(q_norm[j]), row(k_norm[j])
            xp, kp, vp = _attn_prompt(xp, gm, wqkv, wo_t, jnp.tile(qg, (1, N_HEADS)),
                                      jnp.tile(kg, (1, N_KV)), sinks[j], j)
            xs, ck, cv = _attn_sample(xs, gm, wq_g, wqkv, wo_g, qg, kg, sinks[j], ck, cv, j)
            kps.append(kp.reshape(nb, WINDOW, N_KV, HEAD_DIM))
            vps.append(vp.reshape(nb, WINDOW, N_KV, HEAD_DIM))
        else:
            w, win, cpt, c0t, a8, a1 = _ssm_prep(
                ssm_a_re[j], ssm_a_im[j], ssm_log_dt[j], ssm_b_re[j], ssm_b_im[j],
                ssm_c_re[j], ssm_c_im[j])
            dsk = row(ssm_d[j])
            yc, s_re, s_im = _ssm_core(uc, w, win, cpt, a8, nb)
            glu = (yc, gm, dsk, wglu, j)
            srp.append(s_re.reshape(nb, N_GROUPS, SSM_STATE))
            sip.append(s_im.reshape(nb, N_GROUPS, SSM_STATE))
            ys, t_re, t_im = _ssm_sample(
                xs, gm, win, c0t, a1,
                state_ssm_re[j].reshape(ns, N_GROUPS * SSM_STATE),
                state_ssm_im[j].reshape(ns, N_GROUPS * SSM_STATE))
            xs = _glu_sample(xs, ys, gm, dsk, wglu, j)
            srs.append(t_re.reshape(ns, N_GROUPS, SSM_STATE))
            sis.append(t_im.reshape(ns, N_GROUPS, SSM_STATE))

        gf = row(norm_ffn[i])
        next_gain = row(norm_mix[i + 1]) if (i % 2 == 0 and i + 1 < depth) else None
        res = _ffn_prompt(xp, gf, wup, conv_w, cb, wdn, i, glu=glu, next_gain=next_gain)
        glu = None
        xp, cp = res[0], res[1]
        if next_gain is not None:
            uc = res[2]
        xs, conv_s = _ffn_sample(xs, gf, wup, conv_w, cb, wdn, conv_s, i)
        cps.append(cp)

    k_s, v_s = (t.reshape(n_att, ns, WINDOW, N_KV, HEAD_DIM) for t in (ck, cv))
    return (xp, xs.reshape(ns, 1, D_MODEL),
            jnp.stack(kps), jnp.stack(vps), k_s, v_s,
            jnp.stack(srp), jnp.stack(sip), jnp.stack(srs), jnp.stack(sis),
            jnp.stack(cps), conv_s)
```

```python
import functools

import jax
import jax.numpy as jnp
from jax import lax
from jax.experimental import pallas as pl
from jax.experimental.pallas import tpu as pltpu

F32 = jnp.float32
BF16 = jnp.bfloat16

D_MODEL = 1024
HEAD_DIM = 64
N_HEADS = 16
N_KV = 4
GQA = 4
WINDOW = 128
KV_DIM = N_KV * HEAD_DIM
QKV_DIM = D_MODEL + 2 * KV_DIM
D_FF = 2816
F2 = 2 * D_FF
FF_BLK = 256
N_FF_BLK = D_FF // FF_BLK
SSM_GROUP = 16
N_GROUPS = 64
SSM_STATE = 64
RMS_EPS = 1e-6

LANES = 128
SUBLANES = 8
BF16_ROWS = 16
MXU_W = 256
CHUNK = SUBLANES
assert CHUNK & (CHUNK - 1) == 0 and FF_BLK == MXU_W
GB_GROUPS = LANES // SSM_GROUP
N_GB = N_GROUPS // GB_GROUPS
GB_IN = CHUNK * LANES
GB_ST = GB_GROUPS * SSM_STATE
GB_ST2 = 2 * GB_ST

VMEM_LIMIT = 56 * 1024 * 1024


def _cparams(sem):
    return pltpu.CompilerParams(dimension_semantics=sem, vmem_limit_bytes=VMEM_LIMIT)


def _rms(x, g):
    ms = jnp.mean(x * x, axis=-1, keepdims=True)
    return x * lax.rsqrt(ms + RMS_EPS) * g


def _head_rms(t, n, g):
    return jnp.concatenate(
        [_rms(t[:, j * HEAD_DIM:(j + 1) * HEAD_DIM], g) for j in range(n)], axis=1)


def _const_spec(shape):
    nd = len(shape)
    return pl.BlockSpec(shape, lambda *_: (0,) * nd)


ATT_TQ = 1024
ATT_SUB = 512
SEG_W = MXU_W
LOG2E = 1.4426950408889634


def _seg_mean_sq(t, seg):
    sq = t * t
    hi = sq.astype(BF16)
    lo = (sq - hi.astype(F32)).astype(BF16)
    out = []
    for c in range(t.shape[1] // SEG_W):
        sl = slice(c * SEG_W, (c + 1) * SEG_W)
        out.append(jnp.dot(hi[:, sl], seg, preferred_element_type=F32)
                   + jnp.dot(lo[:, sl], seg, preferred_element_type=F32))
    return jnp.concatenate(out, axis=1) * (1.0 / HEAD_DIM)


VT_ROWS = HEAD_DIM + BF16_ROWS


def _attn_prompt_kernel(sink_ref, x_ref, g_ref, wqkv_ref, wot_ref, qg_ref, kg_ref,
                        xo_ref, kl_ref, vl_ref, kbuf, krol, vt, lo_scr, ot, qs):
    i = pl.program_id(1)
    keys = 2 * WINDOW

    @pl.when(i == 0)
    def _():
        kbuf[0:WINDOW, :] = jnp.zeros((WINDOW, KV_DIM), BF16)
        krol[0:WINDOW, :] = jnp.zeros((WINDOW, KV_DIM), BF16)
        vt[:, 0:HEAD_DIM, 0:WINDOW] = jnp.zeros((N_KV, HEAD_DIM, WINDOW), BF16)
        vt[:, HEAD_DIM:, :] = jnp.ones((N_KV, VT_ROWS - HEAD_DIM, WINDOW + ATT_TQ), BF16)
        lo_scr[...] = jnp.full(lo_scr.shape, WINDOW, jnp.int32)

    @pl.when(i == 1)
    def _():
        lo_scr[...] = jnp.zeros(lo_scr.shape, jnp.int32)

    si = lax.broadcasted_iota(jnp.int32, (SEG_W, SEG_W), 0) // HEAD_DIM
    sj = lax.broadcasted_iota(jnp.int32, (SEG_W, SEG_W), 1) // HEAD_DIM
    seg = jnp.where(si == sj, 1.0, 0.0).astype(BF16)

    kc = lax.broadcasted_iota(jnp.int32, (keys, 2 * WINDOW), 0)
    qi = lax.broadcasted_iota(jnp.int32, (keys, 2 * WINDOW), 1) % WINDOW
    band = (kc > qi) & (kc <= qi + WINDOW)
    low = lax.broadcasted_iota(jnp.int32, (WINDOW, LANES), 1) < HEAD_DIM
    nt = (((1,), (1,)), ((), ()))

    def project(h0):
        rows = slice(h0, h0 + ATT_SUB)
        krows = slice(WINDOW + h0, WINDOW + h0 + ATT_SUB)
        h = _rms(x_ref[rows, :], g_ref[...]).astype(BF16)
        qkv = jnp.dot(h, wqkv_ref[...], preferred_element_type=F32)
        q = qkv[:, :D_MODEL]
        k = qkv[:, D_MODEL:D_MODEL + KV_DIM]
        v = qkv[:, D_MODEL + KV_DIM:]
        qs[rows, :] = (q * lax.rsqrt(_seg_mean_sq(q, seg) + RMS_EPS) * qg_ref[...]
                       * (HEAD_DIM ** -0.5 * LOG2E))
        kn = k * lax.rsqrt(_seg_mean_sq(k, seg) + RMS_EPS) * kg_ref[...]
        if h0 + ATT_SUB == ATT_TQ:
            kl_ref[...] = kn[ATT_SUB - WINDOW:]
            vl_ref[...] = v[ATT_SUB - WINDOW:]
        kbuf[krows, :] = kn.astype(BF16)
        for c in range(KV_DIM // LANES):
            sl = slice(c * LANES, (c + 1) * LANES)
            krol[krows, sl] = pltpu.roll(kn[:, sl], HEAD_DIM, 1).astype(BF16)
        v_t = v.T
        for kvh in range(N_KV):
            vt[kvh, 0:HEAD_DIM, krows] = v_t[kvh * HEAD_DIM:(kvh + 1) * HEAD_DIM].astype(BF16)

    def attend(r0):
        valid = band & (kc >= lo_scr[...]) if r0 == 0 else band
        combos = []
        for kvh in range(N_KV):
            khalf = kvh % 2
            combos.append((kvh, (khalf, khalf + 2), kbuf))
            combos.append((kvh, (1 - khalf, 3 - khalf), krol))
        scores = []
        for kvh, pair, kref in combos:
            kcol = slice((kvh // 2) * LANES, (kvh // 2 + 1) * LANES)
            qm = []
            for g in pair:
                qcol = slice((kvh * 2 + g // 2) * LANES, (kvh * 2 + g // 2 + 1) * LANES)
                qm.append(jnp.where(low if g % 2 == 0 else ~low, qs[r0:r0 + WINDOW, qcol], 0.0))
            scores.append(lax.dot_general(kref[r0:r0 + keys, kcol],
                                          jnp.concatenate(qm, axis=0).astype(BF16), nt,
                                          preferred_element_type=F32))
        probs = []
        for (kvh, pair, _), s in zip(combos, scores):
            sink_row = jnp.concatenate(
                [jnp.full((1, WINDOW), sink_ref[kvh * GQA + g] * LOG2E, F32) for g in pair],
                axis=1)
            s = jnp.where(valid, s, -jnp.inf)
            m = jnp.maximum(jnp.max(s, axis=0, keepdims=True), sink_row)
            probs.append((jnp.exp2(s - m).astype(BF16), jnp.exp2(sink_row - m)))
        outs = [jnp.dot(vt[kvh, :, r0:r0 + keys], e, preferred_element_type=F32)
                for (kvh, _, _), (e, _) in zip(combos, probs)]
        for (kvh, pair, _), (_, e_sink), o in zip(combos, probs, outs):
            rden = 1.0 / (o[HEAD_DIM:HEAD_DIM + 1] + e_sink)
            on = (o[:HEAD_DIM] * rden).astype(BF16)
            for n, g in enumerate(pair):
                hd = kvh * GQA + g
                ot[hd * HEAD_DIM:(hd + 1) * HEAD_DIM, r0:r0 + WINDOW] = (
                    on[:, n * WINDOW:(n + 1) * WINDOW])

    def output(h0):
        rows = slice(h0, h0 + ATT_SUB)
        out_t = jnp.dot(wot_ref[...], ot[:, rows], preferred_element_type=F32)
        xo_ref[rows, :] = x_ref[rows, :] + out_t.T

    n_sub = ATT_TQ // ATT_SUB
    blocks = ATT_SUB // WINDOW
    project(0)
    for t in range(n_sub):
        h0 = t * ATT_SUB
        for blk in range(blocks):
            attend(h0 + blk * WINDOW)
            if blk == 0 and t + 1 < n_sub:
                project(h0 + ATT_SUB)
            if blk == 1 and t > 0:
                output(h0 - ATT_SUB)
    output(ATT_TQ - ATT_SUB)

    kbuf[0:WINDOW, :] = kbuf[ATT_TQ:, :]
    krol[0:WINDOW, :] = krol[ATT_TQ:, :]
    vt[:, 0:HEAD_DIM, 0:WINDOW] = vt[:, 0:HEAD_DIM, ATT_TQ:]


def _attn_prompt(x, gain, wqkv, wo_t, qg, kg, sinks, layer):
    b, l, d = x.shape
    wsel = lambda bi, i: (layer, 0, 0)
    tok = pl.BlockSpec((None, ATT_TQ, d), lambda bi, i: (bi, i, 0))
    last = pl.BlockSpec((None, WINDOW, KV_DIM), lambda bi, i: (bi, 0, 0))
    kvbuf = pltpu.VMEM((WINDOW + ATT_TQ, KV_DIM), BF16)
    return pl.pallas_call(
        _attn_prompt_kernel,
        grid=(b, l // ATT_TQ),
        in_specs=[
            pl.BlockSpec(memory_space=pltpu.SMEM),
            tok,
            _const_spec((1, d)),
            pl.BlockSpec((None, d, QKV_DIM), wsel),
            pl.BlockSpec((None, d, d), wsel),
            _const_spec((1, d)),
            _const_spec((1, KV_DIM)),
        ],
        out_specs=[tok, last, last],
        out_shape=[
            jax.ShapeDtypeStruct((b, l, d), F32),
            jax.ShapeDtypeStruct((b, WINDOW, KV_DIM), F32),
            jax.ShapeDtypeStruct((b, WINDOW, KV_DIM), F32),
        ],
        scratch_shapes=[
            kvbuf, kvbuf,
            pltpu.VMEM((N_KV, VT_ROWS, WINDOW + ATT_TQ), BF16),
            pltpu.VMEM((2 * WINDOW, 2 * WINDOW), jnp.int32),
            pltpu.VMEM((d, ATT_TQ), BF16),
            pltpu.VMEM((ATT_TQ, d), F32),
        ],
        compiler_params=_cparams(("arbitrary", "arbitrary")),
        name="attn_prompt",
    )(sinks, x, gain, wqkv, wo_t, qg, kg)


ATS_TB = 16


def _attn_sample_kernel(sink_ref, x_ref, g_ref, wq_ref, wkv_ref, wo_ref, qg_ref, kg_ref,
                        ck_ref, cv_ref, xo_ref, ko_ref, vo_ref, r_scr):
    x = x_ref[...]
    h = _rms(x, g_ref[...]).astype(BF16)
    kv = jnp.dot(h, wkv_ref[...], preferred_element_type=F32)
    kn = _head_rms(kv[:, :KV_DIM], N_KV, kg_ref[...])
    v = kv[:, KV_DIM:]
    qg = qg_ref[...]
    q_g = [_head_rms(jnp.dot(h, wq_ref[g], preferred_element_type=F32), N_KV, qg)
           * (HEAD_DIM ** -0.5) for g in range(GQA)]

    rows = GQA * N_KV
    rkv = lax.broadcasted_iota(jnp.int32, (rows, KV_DIM), 0) % N_KV
    lkv = lax.broadcasted_iota(jnp.int32, (rows, KV_DIM), 1) // HEAD_DIM
    own = rkv == lkv
    colj = lax.broadcasted_iota(jnp.int32, (rows, WINDOW), 1)
    sink_col = jnp.concatenate(
        [jnp.full((1, 1), sink_ref[(r % N_KV) * GQA + r // N_KV], F32) for r in range(rows)],
        axis=0)

    for n in range(ATS_TB):
        ko_ref[n, 0:WINDOW - 1, :] = ck_ref[n, 1:WINDOW, :]
        ko_ref[n, WINDOW - 1:WINDOW, :] = kn[n:n + 1]
        vo_ref[n, 0:WINDOW - 1, :] = cv_ref[n, 1:WINDOW, :]
        vo_ref[n, WINDOW - 1:WINDOW, :] = v[n:n + 1]
    qbds = []
    for n in range(ATS_TB):
        qbd = jnp.concatenate(
            [jnp.broadcast_to(q_g[g][n:n + 1], (N_KV, KV_DIM)) for g in range(GQA)], axis=0)
        qbds.append(jnp.where(own, qbd, 0.0))
    scores = [lax.dot_general(qbds[n].astype(BF16), ck_ref[n].astype(BF16),
                              (((1,), (1,)), ((), ())), preferred_element_type=F32)
              for n in range(ATS_TB)]
    probs = []
    for n in range(ATS_TB):
        s = jnp.where(colj >= 1, scores[n], -jnp.inf)
        s_new = jnp.sum(qbds[n] * kn[n:n + 1], axis=-1, keepdims=True)
        m = jnp.maximum(jnp.maximum(jnp.max(s, axis=-1, keepdims=True), s_new), sink_col)
        e = jnp.exp(s - m)
        e_new = jnp.exp(s_new - m)
        rden = 1.0 / (jnp.sum(e, axis=-1, keepdims=True) + e_new + jnp.exp(sink_col - m))
        probs.append(((e * rden).astype(BF16), e_new * rden))
    outs = [jnp.dot(probs[n][0], cv_ref[n].astype(BF16), preferred_element_type=F32)
            for n in range(ATS_TB)]
    for n in range(ATS_TB):
        o = jnp.where(own, outs[n] + probs[n][1] * v[n:n + 1], 0.0)
        for g in range(GQA):
            r_scr[g, n:n + 1, :] = jnp.sum(o[g * N_KV:(g + 1) * N_KV], axis=0, keepdims=True)

    out = x
    for g in range(GQA):
        out = out + jnp.dot(r_scr[g].astype(BF16), wo_ref[g], preferred_element_type=F32)
    xo_ref[...] = out


def _attn_sample(x, gain, wq_g, wqkv, wo_g, qg, kg, sinks, cache_k, cache_v, layer):
    n, d = x.shape
    tok = pl.BlockSpec((ATS_TB, d), lambda i: (i, 0))
    cache = pl.BlockSpec((None, ATS_TB, WINDOW, KV_DIM), lambda i: (layer, i, 0, 0))
    in_specs = [
        pl.BlockSpec(memory_space=pltpu.SMEM),
        tok,
        _const_spec((1, d)),
        pl.BlockSpec((None, GQA, d, KV_DIM), lambda i: (layer, 0, 0, 0)),
        pl.BlockSpec((None, d, 2 * KV_DIM), lambda i: (layer, 0, D_MODEL // (2 * KV_DIM))),
        pl.BlockSpec((None, GQA, KV_DIM, d), lambda i: (layer, 0, 0, 0)),
        _const_spec((1, HEAD_DIM)),
        _const_spec((1, HEAD_DIM)),
        cache,
        cache,
    ]
    args = [sinks, x, gain, wq_g, wqkv, wo_g, qg, kg, cache_k, cache_v]
    return pl.pallas_call(
        _attn_sample_kernel,
        grid=(n // ATS_TB,),
        in_specs=in_specs,
        out_specs=[tok, cache, cache],
        out_shape=[
            jax.ShapeDtypeStruct((n, d), F32),
            jax.ShapeDtypeStruct(cache_k.shape, F32),
            jax.ShapeDtypeStruct(cache_v.shape, F32),
        ],
        scratch_shapes=[pltpu.VMEM((GQA, ATS_TB, KV_DIM), F32)],
        input_output_aliases={len(args) - 2: 1, len(args) - 1: 2},
        compiler_params=_cparams(("arbitrary",)),
        name="attn_sample",
    )(*args)


FFN_TM = 512
SSM_TC = FFN_TM // CHUNK
CARRY = SUBLANES
CONV_RC = 64


def _ffn_prompt_kernel(*refs, glu_in, uc_out):
    refs = list(refs)
    x_ref = refs.pop(0)
    if glu_in:
        yc_ref, gm_ref, dsk_ref, wglu_ref = (refs.pop(0) for _ in range(4))
    g_ref, wup_ref, cw_ref, cb_ref, wdn_ref = (refs.pop(0) for _ in range(5))
    if uc_out:
        gn_ref = refs.pop(0)
    xo_ref, cs_ref = refs.pop(0), refs.pop(0)
    if uc_out:
        uc_ref = refs.pop(0)
    carry, upbuf, act, hs, oscr = refs

    i = pl.program_id(1)
    hm = FFN_TM // 2
    nlb = FF_BLK // LANES
    dlb = D_MODEL // LANES

    @pl.when(i == 0)
    def _():
        carry[...] = jnp.zeros_like(carry)

    if glu_in:
        for gb in range(N_GB):
            for t in range(CHUNK):
                oscr[gb, pl.ds(t, SSM_TC, stride=CHUNK), :] = (
                    yc_ref[:, gb * GB_IN + t * LANES:gb * GB_IN + (t + 1) * LANES])
        y = jnp.concatenate([oscr[gb] for gb in range(dlb)], axis=1)
        xo_ref[...] = _glu_tail(x_ref[...], y, gm_ref, dsk_ref, wglu_ref)
        xin_ref = xo_ref
    else:
        xin_ref = x_ref

    hs[...] = _rms(xin_ref[...], g_ref[...]).astype(BF16)

    def up_into(j, slot):
        for half, c0 in enumerate((j * FF_BLK, D_FF + j * FF_BLK)):
            cols = pl.ds(c0, FF_BLK)
            up = jnp.dot(hs[...], wup_ref[:, cols], preferred_element_type=F32)
            for lb in range(nlb):
                lcols = pl.ds(c0 + lb * LANES, LANES)
                buf = upbuf.at[slot, half, lb]
                buf[0:CARRY, :] = carry[:, lcols]
                buf[CARRY:, :] = up[:, lb * LANES:(lb + 1) * LANES]
                carry[:, lcols] = buf[FFN_TM:, :]

    def conv(slot, half, lb, c0, m0):
        lcols = pl.ds(c0 + lb * LANES, LANES)
        buf = upbuf.at[slot, half, lb]
        tap = lambda off: buf[pl.ds(CARRY + off + 2 * m0, CONV_RC, stride=2), :]
        em2, om1, e0, o1 = tap(-2), tap(-1), tap(0), tap(1)
        w0, w1, w2 = cw_ref[0:1, lcols], cw_ref[1:2, lcols], cw_ref[2:3, lcols]
        cb = cb_ref[:, lcols]
        return (cb + w0 * em2 + w1 * om1 + w2 * e0, cb + w0 * om1 + w1 * e0 + w2 * o1)

    for j in range(N_FF_BLK):
        up_into(j, j % 2)
        for m0 in range(0, hm, CONV_RC):
            for lb in range(nlb):
                cg = conv(j % 2, 0, lb, j * FF_BLK, m0)
                cv = conv(j % 2, 1, lb, D_FF + j * FF_BLK, m0)
                acol = pl.ds(j * FF_BLK + lb * LANES, LANES)
                for p in range(2):
                    act[p * hm + m0:p * hm + m0 + CONV_RC, acol] = (
                        jax.nn.silu(cg[p]) * cv[p]).astype(BF16)

    out = jnp.dot(act[...], wdn_ref[...], preferred_element_type=F32)
    for lb in range(dlb):
        for p in range(2):
            oscr[lb, pl.ds(p, hm, stride=2), :] = out[p * hm:(p + 1) * hm,
                                                      lb * LANES:(lb + 1) * LANES]
    xo = xin_ref[...] + jnp.concatenate([oscr[lb] for lb in range(dlb)], axis=1)
    xo_ref[...] = xo
    cs_ref[...] = carry[CARRY - 2:CARRY, :]

    if uc_out:
        h2 = _rms(xo, gn_ref[...])
        for gb in range(N_GB):
            oscr[gb] = h2[:, gb * LANES:(gb + 1) * LANES]
        for gb in range(N_GB):
            for t in range(CHUNK):
                uc_ref[:, gb * GB_IN + t * LANES:gb * GB_IN + (t + 1) * LANES] = (
                    oscr[gb, pl.ds(t, SSM_TC, stride=CHUNK), :].astype(BF16))


def _ffn_prompt(x, gain, wup, cw, cb, wdn, layer, glu=None, next_gain=None):
    b, l, d = x.shape
    nt = l // FFN_TM
    tok = pl.BlockSpec((None, FFN_TM, d), lambda bi, i: (bi, i, 0))
    chunks = pl.BlockSpec((SSM_TC, CHUNK * d), lambda bi, i: (bi * nt + i, 0))
    wsel = lambda bi, i: (layer, 0, 0)
    args, in_specs = [x], [tok]
    if glu is not None:
        yc, gm, dsk, wglu, mixer_layer = glu
        args += [yc, gm, dsk, wglu]
        in_specs += [chunks, _const_spec((1, d)), _const_spec((1, d)),
                     pl.BlockSpec((None, d, 2 * d), lambda bi, i: (mixer_layer, 0, 0),
                                  pipeline_mode=pl.Buffered(1))]
    args += [gain, wup, cw, cb, wdn]
    in_specs += [
        _const_spec((1, d)),
        pl.BlockSpec((None, d, F2), wsel, pipeline_mode=pl.Buffered(1)),
        pl.BlockSpec((None, 3, F2), wsel),
        pl.BlockSpec((None, 1, F2), wsel),
        pl.BlockSpec((None, D_FF, d), wsel, pipeline_mode=pl.Buffered(1)),
    ]
    out_specs = [tok, pl.BlockSpec((None, 2, F2), lambda bi, i: (bi, 0, 0))]
    out_shape = [jax.ShapeDtypeStruct((b, l, d), F32), jax.ShapeDtypeStruct((b, 2, F2), F32)]
    if next_gain is not None:
        args.append(next_gain)
        in_specs.append(_const_spec((1, d)))
        out_specs.append(chunks)
        out_shape.append(jax.ShapeDtypeStruct((b * l // CHUNK, CHUNK * d), BF16))
    return pl.pallas_call(
        functools.partial(_ffn_prompt_kernel, glu_in=glu is not None,
                          uc_out=next_gain is not None),
        grid=(b, nt),
        in_specs=in_specs,
        out_specs=out_specs,
        out_shape=out_shape,
        scratch_shapes=[
            pltpu.VMEM((CARRY, F2), F32),
            pltpu.VMEM((2, 2, FF_BLK // LANES, CARRY + FFN_TM, LANES), F32),
            pltpu.VMEM((FFN_TM, D_FF), BF16),
            pltpu.VMEM((FFN_TM, d), BF16),
            pltpu.VMEM((d // LANES, FFN_TM, LANES), F32),
        ],
        compiler_params=_cparams(("arbitrary", "arbitrary")),
        name="ffn_prompt",
    )(*args)


FFS_BLK = D_FF // 2
N_FFS_BLK = D_FF // FFS_BLK


def _ffn_sample_kernel(x_ref, g_ref, w_ref, cw_ref, cb_ref, wdn_ref, sc_ref,
                       xo_ref, cs_ref, gate):
    j = pl.program_id(0)

    @pl.when(j == 0)
    def _():
        xo_ref[...] = x_ref[...]

    h = _rms(x_ref[...], g_ref[...]).astype(BF16)
    up = jnp.dot(h, w_ref[...], preferred_element_type=F32)
    b0 = sc_ref[:, 0, :]
    b1 = sc_ref[:, 1, :]
    c = cb_ref[...]
    c = c + cw_ref[0:1, :] * b0
    c = c + cw_ref[1:2, :] * b1
    c = c + cw_ref[2:3, :] * up
    cs_ref[...] = jnp.stack([b1, up], axis=1)

    @pl.when(j < N_FFS_BLK)
    def _():
        gate[j] = jax.nn.silu(c)

    @pl.when(j >= N_FFS_BLK)
    def _():
        a = (gate[j - N_FFS_BLK] * c).astype(BF16)
        xo_ref[...] += jnp.dot(a, wdn_ref[...], preferred_element_type=F32)


def _ffn_sample(x, gain, wup, cw, cb, wdn, state, layer):
    n, d = x.shape
    nb = N_FFS_BLK
    sblk = pl.BlockSpec((None, n, 2, FFS_BLK), lambda j: (layer, 0, 0, j))
    in_specs = [
        _const_spec((n, d)),
        _const_spec((1, d)),
        pl.BlockSpec((None, d, FFS_BLK), lambda j: (layer, 0, j)),
        pl.BlockSpec((None, 3, FFS_BLK), lambda j: (layer, 0, j)),
        pl.BlockSpec((None, 1, FFS_BLK), lambda j: (layer, 0, j)),
        pl.BlockSpec((None, FFS_BLK, d), lambda j: (layer, jnp.maximum(j - nb, 0), 0)),
        sblk,
    ]
    args = [x, gain, wup, cw, cb, wdn, state]
    return pl.pallas_call(
        _ffn_sample_kernel,
        grid=(2 * nb,),
        in_specs=in_specs,
        out_specs=[_const_spec((n, d)), sblk],
        out_shape=[
            jax.ShapeDtypeStruct((n, d), F32),
            jax.ShapeDtypeStruct(state.shape, F32),
        ],
        scratch_shapes=[pltpu.VMEM((nb, n, FFS_BLK), F32)],
        input_output_aliases={len(args) - 1: 1},
        compiler_params=_cparams(("arbitrary",)),
        name="ffn_sample",
    )(*args)


def _tile_states(t):
    t2 = jnp.concatenate([t, t], axis=1)
    return jnp.concatenate([t2] * (GB_GROUPS // 2), axis=1)


def _ssm_prep_kernel(are_ref, aim_ref, dt_ref, bre_ref, bim_ref, cre_ref, cim_ref,
                     aflat_ref, iflat_ref, dtflat_ref,
                     w_ref, win_ref, cpt_ref, c0t_ref, a8_ref, a1_ref):
    a_re = are_ref[...]
    a_im = aim_ref[...]
    dt = jnp.exp(dt_ref[...])

    mag = jnp.exp(a_re * dt)
    ab_re, ab_im = mag * jnp.cos(a_im * dt), mag * jnp.sin(a_im * dt)
    pw = [(jnp.ones_like(ab_re), jnp.zeros_like(ab_re)), (ab_re, ab_im)]
    for _ in range(CHUNK - 1):
        p_re, p_im = pw[-1]
        pw.append((p_re * ab_re - p_im * ab_im, p_re * ab_im + p_im * ab_re))
    den = a_re * a_re + a_im * a_im
    z_re = ((ab_re - 1.0) * a_re + ab_im * a_im) / den
    z_im = (ab_im * a_re - (ab_re - 1.0) * a_im) / den
    b_re = bre_ref[...]
    b_im = bim_ref[...]
    bb_re = z_re * b_re - z_im * b_im
    bb_im = z_re * b_im + z_im * b_re
    c_re = cre_ref[...]
    c_im = cim_ref[...]

    rg = lax.broadcasted_iota(jnp.int32, (LANES, GB_ST2), 0) // SSM_GROUP
    lg = (lax.broadcasted_iota(jnp.int32, (LANES, GB_ST2), 1) % GB_ST) // SSM_STATE
    own = rg == lg

    def blockdiag(t_re, t_im):
        full = jnp.concatenate([_tile_states(t_re), _tile_states(t_im)], axis=1)
        return jnp.where(own, full, 0.0)

    c0t = blockdiag(c_re, -c_im)
    c0t_ref[...] = c0t.astype(BF16)

    lfull = []
    for d in range(CHUNK):
        if d == 0:
            l_re, l_im = bb_re, bb_im
        else:
            e_re, e_im = pw[d]
            l_re = e_re * bb_re - e_im * bb_im
            l_im = e_re * bb_im + e_im * bb_re
        lfull.append(blockdiag(l_re, l_im))

    def split(t):
        hi = t.astype(BF16)
        return hi, (t - hi.astype(F32)).astype(BF16)

    def dot_nt(a, b):
        return lax.dot_general(a, b, (((1,), (1,)), ((), ())), preferred_element_type=F32)

    c_hi, c_lo = split(c0t)
    kd = []
    for l in lfull:
        l_hi, l_lo = split(l)
        kd.append(dot_nt(l_hi, c_hi) + dot_nt(l_hi, c_lo) + dot_nt(l_lo, c_hi))
    zero = jnp.zeros((LANES, LANES), F32)
    for s in range(CHUNK):
        row = jnp.concatenate([kd[t - s] if t >= s else zero for t in range(CHUNK)], axis=1)
        w_ref[s * LANES:(s + 1) * LANES, :] = row.astype(BF16)
        win_ref[s * LANES:(s + 1) * LANES, :] = lfull[CHUNK - 1 - s].astype(BF16)
    for t in range(CHUNK):
        e_re, e_im = pw[t + 1]
        cpt_ref[t * LANES:(t + 1) * LANES, :] = blockdiag(
            c_re * e_re - c_im * e_im, -c_re * e_im - c_im * e_re).astype(BF16)

    af = aflat_ref[...]
    ai = iflat_ref[...]
    dtf = jnp.exp(dtflat_ref[...])
    magf = jnp.exp(af * dtf)
    f_re, f_im = magf * jnp.cos(ai * dtf), magf * jnp.sin(ai * dtf)
    a1_ref[...] = jnp.concatenate([f_re, f_im], axis=1)
    for _ in range(CHUNK.bit_length() - 1):
        f_re, f_im = f_re * f_re - f_im * f_im, 2.0 * f_re * f_im
    a8_ref[...] = jnp.concatenate([f_re, f_im], axis=1)


def _ssm_prep(a_re, a_im, log_dt, b_re, b_im, c_re, c_im):
    rep = lambda t: jnp.repeat(t, SSM_GROUP, axis=0)
    are_x = rep(a_re)
    aim_x = rep(a_im)
    dt_x = rep(log_dt[:, None])
    bt_re = jnp.transpose(b_re, (0, 2, 1)).reshape(N_GROUPS * SSM_GROUP, SSM_STATE)
    bt_im = jnp.transpose(b_im, (0, 2, 1)).reshape(N_GROUPS * SSM_GROUP, SSM_STATE)
    cr = c_re.reshape(N_GROUPS * SSM_GROUP, SSM_STATE)
    ci = c_im.reshape(N_GROUPS * SSM_GROUP, SSM_STATE)
    aflat = a_re.reshape(N_GB, 1, GB_ST)
    iflat = a_im.reshape(N_GB, 1, GB_ST)
    dtflat = jnp.repeat(log_dt, SSM_STATE).reshape(N_GB, 1, GB_ST)
    rows = pl.BlockSpec((LANES, SSM_STATE), lambda g: (g, 0))
    flat = pl.BlockSpec((None, 1, GB_ST), lambda g: (g, 0, 0))
    big = pl.BlockSpec((None, GB_IN, GB_ST2), lambda g: (g, 0, 0))
    vec = pl.BlockSpec((None, 1, GB_ST2), lambda g: (g, 0, 0))
    return pl.pallas_call(
        _ssm_prep_kernel,
        grid=(N_GB,),
        in_specs=[rows, rows, pl.BlockSpec((LANES, 1), lambda g: (g, 0)),
                  rows, rows, rows, rows, flat, flat, flat],
        out_specs=[big, big, big,
                   pl.BlockSpec((None, LANES, GB_ST2), lambda g: (g, 0, 0)), vec, vec],
        out_shape=[
            jax.ShapeDtypeStruct((N_GB, GB_IN, GB_IN), BF16),
            jax.ShapeDtypeStruct((N_GB, GB_IN, GB_ST2), BF16),
            jax.ShapeDtypeStruct((N_GB, GB_IN, GB_ST2), BF16),
            jax.ShapeDtypeStruct((N_GB, LANES, GB_ST2), BF16),
            jax.ShapeDtypeStruct((N_GB, 1, GB_ST2), F32),
            jax.ShapeDtypeStruct((N_GB, 1, GB_ST2), F32),
        ],
        compiler_params=_cparams(("arbitrary",)),
        name="ssm_prep",
    )(are_x, aim_x, dt_x, bt_re, bt_im, cr, ci, aflat, iflat, dtflat)


SSM_RT = MXU_W
SCAN_PAD = SUBLANES


def _ssm_core_kernel(uc_ref, w_ref, win_ref, cpt_ref, a8_ref, yc_ref, sre_ref, sim_ref, zs,
                     *, n_seq, n_chunk):
    n_rows = n_seq * n_chunk
    nlb = GB_ST2 // LANES
    half = nlb // 2
    pitch = n_chunk + SCAN_PAD

    def zrows(r):
        seq, off = divmod(r * SSM_RT, n_chunk)
        return slice(seq * pitch + off, seq * pitch + off + SSM_RT)

    for r in range(n_rows // SSM_RT):
        rs = slice(r * SSM_RT, (r + 1) * SSM_RT)
        z = jnp.dot(uc_ref[rs, :], win_ref[...], preferred_element_type=F32)
        for c in range(nlb):
            zs[c, zrows(r), :] = z[:, c * LANES:(c + 1) * LANES]

    a8 = a8_ref[...]
    a_bl = [jnp.broadcast_to(a8[:, c * LANES:(c + 1) * LANES], (n_seq, LANES))
            for c in range(nlb)]

    def step(k, carry):
        rows = pl.ds(k, n_seq, stride=pitch)
        new = []
        for c in range(half):
            s_re, s_im = carry[c], carry[half + c]
            z_re = zs[c, rows, :]
            z_im = zs[half + c, rows, :]
            zs[c, rows, :] = s_re
            zs[half + c, rows, :] = s_im
            new.append((a_bl[c] * s_re - a_bl[half + c] * s_im + z_re,
                        a_bl[c] * s_im + a_bl[half + c] * s_re + z_im))
        return tuple(n[0] for n in new) + tuple(n[1] for n in new)

    n_rt = n_rows // SSM_RT
    steps_per = n_chunk // n_rt
    fin = (jnp.zeros((n_seq, LANES), F32),) * nlb
    for r in range(n_rt):
        rs = slice(r * SSM_RT, (r + 1) * SSM_RT)
        for m in range(CHUNK // 2):
            kin = (2 * m + 2) * LANES
            oc = slice(2 * m * LANES, (2 * m + 2) * LANES)
            yc_ref[rs, oc] = jnp.dot(uc_ref[rs, :kin], w_ref[:kin, oc],
                                     preferred_element_type=F32)
        for k in range(r * steps_per, (r + 1) * steps_per):
            fin = step(k, fin)
    sre_ref[...] = jnp.concatenate(fin[:half], axis=1)
    sim_ref[...] = jnp.concatenate(fin[half:], axis=1)

    for r in range(n_rt):
        rs = slice(r * SSM_RT, (r + 1) * SSM_RT)
        sp = jnp.concatenate([zs[c, zrows(r), :] for c in range(nlb)], axis=1).astype(BF16)
        yc_ref[rs, :] += lax.dot_general(sp, cpt_ref[...], (((1,), (1,)), ((), ())),
                                         preferred_element_type=F32)


def _ssm_core(uc, w, win, cpt, a8, n_seq):
    n_rows = uc.shape[0]
    n_chunk = n_rows // n_seq
    mat = lambda r: pl.BlockSpec((None, r, GB_ST2), lambda g: (g, 0, 0))
    blk = pl.BlockSpec((n_rows, GB_IN), lambda g: (0, g))
    st = pl.BlockSpec((n_seq, GB_ST), lambda g: (0, g))
    return pl.pallas_call(
        functools.partial(_ssm_core_kernel, n_seq=n_seq, n_chunk=n_chunk),
        grid=(N_GB,),
        in_specs=[blk, mat(GB_IN), mat(GB_IN), mat(GB_IN), mat(1)],
        out_specs=[blk, st, st],
        out_shape=[
            jax.ShapeDtypeStruct((n_rows, N_GB * GB_IN), F32),
            jax.ShapeDtypeStruct((n_seq, N_GROUPS * SSM_STATE), F32),
            jax.ShapeDtypeStruct((n_seq, N_GROUPS * SSM_STATE), F32),
        ],
        scratch_shapes=[pltpu.VMEM((GB_ST2 // LANES, n_seq * (n_chunk + SCAN_PAD), LANES), F32)],
        compiler_params=_cparams(("arbitrary",)),
        name="ssm_core",
    )(uc, w, win, cpt, a8)


def _glu_tail(x, y, g_ref, dsk_ref, wglu_ref):
    h = _rms(x, g_ref[...])
    z = jax.nn.gelu(y + dsk_ref[...] * h).astype(BF16)
    zz = jnp.dot(z, wglu_ref[...], preferred_element_type=F32)
    return x + zz[:, :D_MODEL] * jax.nn.sigmoid(zz[:, D_MODEL:])


def _ssm_sample_kernel(x_ref, g_ref, bb_ref, c0t_ref, a1_ref, hre_ref, him_ref,
                       y_ref, sre_ref, sim_ref):
    gb = pl.program_id(0)
    x = x_ref[...]
    ms = jnp.mean(x * x, axis=-1, keepdims=True)
    cols = pl.ds(pl.multiple_of(gb * LANES, LANES), LANES)
    u = (x_ref[:, cols] * lax.rsqrt(ms + RMS_EPS) * g_ref[:, cols]).astype(BF16)
    bu = jnp.dot(u, bb_ref[...], preferred_element_type=F32)
    a1 = a1_ref[...]
    a_re = a1[:, :GB_ST]
    a_im = a1[:, GB_ST:]
    h_re = hre_ref[...]
    h_im = him_ref[...]
    s_re = a_re * h_re - a_im * h_im + bu[:, :GB_ST]
    s_im = a_re * h_im + a_im * h_re + bu[:, GB_ST:]
    sre_ref[...] = s_re
    sim_ref[...] = s_im
    s = jnp.concatenate([s_re, s_im], axis=1).astype(BF16)
    y_ref[...] = lax.dot_general(s, c0t_ref[...], (((1,), (1,)), ((), ())),
                                 preferred_element_type=F32)


def _ssm_sample(x, gain, win, c0t, a1, h_re, h_im):
    n, d = x.shape
    st = pl.BlockSpec((n, GB_ST), lambda g: (0, g))
    return pl.pallas_call(
        _ssm_sample_kernel,
        grid=(N_GB,),
        in_specs=[
            _const_spec((n, d)),
            _const_spec((1, d)),
            pl.BlockSpec((None, LANES, GB_ST2), lambda g: (g, CHUNK - 1, 0)),
            pl.BlockSpec((None, LANES, GB_ST2), lambda g: (g, 0, 0)),
            pl.BlockSpec((None, 1, GB_ST2), lambda g: (g, 0, 0)),
            st, st,
        ],
        out_specs=[pl.BlockSpec((n, LANES), lambda g: (0, g)), st, st],
        out_shape=[
            jax.ShapeDtypeStruct((n, d), F32),
            jax.ShapeDtypeStruct((n, N_GROUPS * SSM_STATE), F32),
            jax.ShapeDtypeStruct((n, N_GROUPS * SSM_STATE), F32),
        ],
        compiler_params=_cparams(("arbitrary",)),
        name="ssm_sample",
    )(x, gain, win, c0t, a1, h_re, h_im)


def _glu_sample_kernel(x_ref, y_ref, g_ref, dsk_ref, wglu_ref, xo_ref):
    xo_ref[...] = _glu_tail(x_ref[...], y_ref[...], g_ref, dsk_ref, wglu_ref)


def _glu_sample(x, y, gain, dsk, wglu, layer):
    n, d = x.shape
    return pl.pallas_call(
        _glu_sample_kernel,
        grid=(1,),
        in_specs=[_const_spec((n, d)), _const_spec((n, d)), _const_spec((1, d)),
                  _const_spec((1, d)), pl.BlockSpec((None, d, 2 * d), lambda i: (layer, 0, 0))],
        out_specs=_const_spec((n, d)),
        out_shape=jax.ShapeDtypeStruct((n, d), F32),
        compiler_params=_cparams(("arbitrary",)),
        name="glu_sample",
    )(x, y, gain, dsk, wglu)


def kernel(x_prompt, x_sample, cache_k, cache_v, state_ssm_re, state_ssm_im, state_conv,
           norm_mix, norm_ffn, w_qkv, w_o, q_norm, k_norm, sinks, ssm_a_re, ssm_a_im,
           ssm_log_dt, ssm_b_re, ssm_b_im, ssm_c_re, ssm_c_im, ssm_d, w_glu, w_up, conv_w,
           conv_b, w_down):
    depth = norm_mix.shape[0]
    nb, _, _ = x_prompt.shape
    ns = x_sample.shape[0]
    xp = x_prompt
    xs = x_sample.reshape(ns, D_MODEL)
    row = lambda t: t.reshape(1, -1)

    wqkv = w_qkv.astype(BF16)
    wo = w_o.astype(BF16)
    wo_t = jnp.swapaxes(wo, 1, 2)
    wglu = w_glu.astype(BF16)
    wup = w_up.astype(BF16)
    wdn = w_down.astype(BF16)
    cb = conv_b.reshape(depth, 1, F2)
    n_att = wqkv.shape[0]
    wq_g = wqkv[:, :, :D_MODEL].reshape(n_att, D_MODEL, N_KV, GQA, HEAD_DIM)
    wq_g = jnp.transpose(wq_g, (0, 3, 1, 2, 4)).reshape(n_att, GQA, D_MODEL, KV_DIM)
    wo_g = wo.reshape(n_att, N_KV, GQA, HEAD_DIM, D_MODEL)
    wo_g = jnp.transpose(wo_g, (0, 2, 1, 3, 4)).reshape(n_att, GQA, KV_DIM, D_MODEL)
    ck = cache_k.reshape(n_att, ns, WINDOW, KV_DIM)
    cv = cache_v.reshape(n_att, ns, WINDOW, KV_DIM)

    kps, vps = [], []
    srp, sip, srs, sis = [], [], [], []
    cps = []
    conv_s = state_conv
    glu = None
    for i in range(depth):
        j = i // 2
        gm = row(norm_mix[i])
        if i % 2 == 0:
            qg, kg = row(q_norm[j]), row(k_norm[j])
            xp, kp, vp = _attn_prompt(xp, gm, wqkv, wo_t, jnp.tile(qg, (1, N_HEADS)),
                                      jnp.tile(kg, (1, N_KV)), sinks[j], j)
            xs, ck, cv = _attn_sample(xs, gm, wq_g, wqkv, wo_g, qg, kg, sinks[j], ck, cv, j)
            kps.append(kp.reshape(nb, WINDOW, N_KV, HEAD_DIM))
            vps.append(vp.reshape(nb, WINDOW, N_KV, HEAD_DIM))
        else:
            w, win, cpt, c0t, a8, a1 = _ssm_prep(
                ssm_a_re[j], ssm_a_im[j], ssm_log_dt[j], ssm_b_re[j], ssm_b_im[j],
                ssm_c_re[j], ssm_c_im[j])
            dsk = row(ssm_d[j])
            yc, s_re, s_im = _ssm_core(uc, w, win, cpt, a8, nb)
            glu = (yc, gm, dsk, wglu, j)
            srp.append(s_re.reshape(nb, N_GROUPS, SSM_STATE))
            sip.append(s_im.reshape(nb, N_GROUPS, SSM_STATE))
            ys, t_re, t_im = _ssm_sample(
                xs, gm, win, c0t, a1,
                state_ssm_re[j].reshape(ns, N_GROUPS * SSM_STATE),
                state_ssm_im[j].reshape(ns, N_GROUPS * SSM_STATE))
            xs = _glu_sample(xs, ys, gm, dsk, wglu, j)
            srs.append(t_re.reshape(ns, N_GROUPS, SSM_STATE))
            sis.append(t_im.reshape(ns, N_GROUPS, SSM_STATE))

        gf = row(norm_ffn[i])
        next_gain = row(norm_mix[i + 1]) if (i % 2 == 0 and i + 1 < depth) else None
        res = _ffn_prompt(xp, gf, wup, conv_w, cb, wdn, i, glu=glu, next_gain=next_gain)
        glu = None
        xp, cp = res[0], res[1]
        if next_gain is not None:
            uc = res[2]
        xs, conv_s = _ffn_sample(xs, gf, wup, conv_w, cb, wdn, conv_s, i)
        cps.append(cp)

    k_s, v_s = (t.reshape(n_att, ns, WINDOW, N_KV, HEAD_DIM) for t in (ck, cv))
    return (xp, xs.reshape(ns, 1, D_MODEL),
            jnp.stack(kps), jnp.stack(vps), k_s, v_s,
            jnp.stack(srp), jnp.stack(sip), jnp.stack(srs), jnp.stack(sis),
            jnp.stack(cps), conv_s)
```

```python
import functools

import jax
import jax.numpy as jnp
from jax import lax
from jax.experimental import pallas as pl
from jax.experimental.pallas import tpu as pltpu

F32 = jnp.float32
BF16 = jnp.bfloat16

D_MODEL = 1024
HEAD_DIM = 64
N_HEADS = 16
N_KV = 4
GQA = 4
WINDOW = 128
KV_DIM = N_KV * HEAD_DIM
QKV_DIM = D_MODEL + 2 * KV_DIM
D_FF = 2816
F2 = 2 * D_FF
FF_BLK = 256
N_FF_BLK = D_FF // FF_BLK
SSM_GROUP = 16
N_GROUPS = 64
SSM_STATE = 64
RMS_EPS = 1e-6

LANES = 128
SUBLANES = 8
BF16_ROWS = 16
MXU_W = 256
CHUNK = SUBLANES
assert CHUNK & (CHUNK - 1) == 0 and FF_BLK == MXU_W
GB_GROUPS = LANES // SSM_GROUP
N_GB = N_GROUPS // GB_GROUPS
GB_IN = CHUNK * LANES
GB_ST = GB_GROUPS * SSM_STATE
GB_ST2 = 2 * GB_ST

VMEM_LIMIT = 56 * 1024 * 1024


def _cparams(sem):
    return pltpu.CompilerParams(dimension_semantics=sem, vmem_limit_bytes=VMEM_LIMIT)


def _rms(x, g):
    ms = jnp.mean(x * x, axis=-1, keepdims=True)
    return x * lax.rsqrt(ms + RMS_EPS) * g


def _head_rms(t, n, g):
    return jnp.concatenate(
        [_rms(t[:, j * HEAD_DIM:(j + 1) * HEAD_DIM], g) for j in range(n)], axis=1)


def _const_spec(shape):
    nd = len(shape)
    return pl.BlockSpec(shape, lambda *_: (0,) * nd)


ATT_TQ = 1024
ATT_SUB = 512
SEG_W = MXU_W
LOG2E = 1.4426950408889634


def _seg_mean_sq(t, seg):
    sq = t * t
    hi = sq.astype(BF16)
    lo = (sq - hi.astype(F32)).astype(BF16)
    out = []
    for c in range(t.shape[1] // SEG_W):
        sl = slice(c * SEG_W, (c + 1) * SEG_W)
        out.append(jnp.dot(hi[:, sl], seg, preferred_element_type=F32)
                   + jnp.dot(lo[:, sl], seg, preferred_element_type=F32))
    return jnp.concatenate(out, axis=1) * (1.0 / HEAD_DIM)


VT_ROWS = HEAD_DIM + BF16_ROWS


def _attn_prompt_kernel(sink_ref, x_ref, g_ref, wqkv_ref, wot_ref, qg_ref, kg_ref,
                        xo_ref, kl_ref, vl_ref, kbuf, krol, vt, lo_scr, ot, qs):
    i = pl.program_id(1)
    keys = 2 * WINDOW

    @pl.when(i == 0)
    def _():
        kbuf[0:WINDOW, :] = jnp.zeros((WINDOW, KV_DIM), BF16)
        krol[0:WINDOW, :] = jnp.zeros((WINDOW, KV_DIM), BF16)
        vt[:, 0:HEAD_DIM, 0:WINDOW] = jnp.zeros((N_KV, HEAD_DIM, WINDOW), BF16)
        vt[:, HEAD_DIM:, :] = jnp.ones((N_KV, VT_ROWS - HEAD_DIM, WINDOW + ATT_TQ), BF16)
        lo_scr[...] = jnp.full(lo_scr.shape, WINDOW, jnp.int32)

    @pl.when(i == 1)
    def _():
        lo_scr[...] = jnp.zeros(lo_scr.shape, jnp.int32)

    si = lax.broadcasted_iota(jnp.int32, (SEG_W, SEG_W), 0) // HEAD_DIM
    sj = lax.broadcasted_iota(jnp.int32, (SEG_W, SEG_W), 1) // HEAD_DIM
    seg = jnp.where(si == sj, 1.0, 0.0).astype(BF16)

    kc = lax.broadcasted_iota(jnp.int32, (keys, 2 * WINDOW), 0)
    qi = lax.broadcasted_iota(jnp.int32, (keys, 2 * WINDOW), 1) % WINDOW
    band = (kc > qi) & (kc <= qi + WINDOW)
    low = lax.broadcasted_iota(jnp.int32, (WINDOW, LANES), 1) < HEAD_DIM
    nt = (((1,), (1,)), ((), ()))

    def project(h0):
        rows = slice(h0, h0 + ATT_SUB)
        krows = slice(WINDOW + h0, WINDOW + h0 + ATT_SUB)
        h = _rms(x_ref[rows, :], g_ref[...]).astype(BF16)
        qkv = jnp.dot(h, wqkv_ref[...], preferred_element_type=F32)
        q = qkv[:, :D_MODEL]
        k = qkv[:, D_MODEL:D_MODEL + KV_DIM]
        v = qkv[:, D_MODEL + KV_DIM:]
        qs[rows, :] = (q * lax.rsqrt(_seg_mean_sq(q, seg) + RMS_EPS) * qg_ref[...]
                       * (HEAD_DIM ** -0.5 * LOG2E))
        kn = k * lax.rsqrt(_seg_mean_sq(k, seg) + RMS_EPS) * kg_ref[...]
        if h0 + ATT_SUB == ATT_TQ:
            kl_ref[...] = kn[ATT_SUB - WINDOW:]
            vl_ref[...] = v[ATT_SUB - WINDOW:]
        kbuf[krows, :] = kn.astype(BF16)
        for c in range(KV_DIM // LANES):
            sl = slice(c * LANES, (c + 1) * LANES)
            krol[krows, sl] = pltpu.roll(kn[:, sl], HEAD_DIM, 1).astype(BF16)
        v_t = v.T
        for kvh in range(N_KV):
            vt[kvh, 0:HEAD_DIM, krows] = v_t[kvh * HEAD_DIM:(kvh + 1) * HEAD_DIM].astype(BF16)

    def attend(r0):
        valid = band & (kc >= lo_scr[...]) if r0 == 0 else band
        combos = []
        for kvh in range(N_KV):
            khalf = kvh % 2
            combos.append((kvh, (khalf, khalf + 2), kbuf))
            combos.append((kvh, (1 - khalf, 3 - khalf), krol))
        scores = []
        for kvh, pair, kref in combos:
            kcol = slice((kvh // 2) * LANES, (kvh // 2 + 1) * LANES)
            qm = []
            for g in pair:
                qcol = slice((kvh * 2 + g // 2) * LANES, (kvh * 2 + g // 2 + 1) * LANES)
                qm.append(jnp.where(low if g % 2 == 0 else ~low, qs[r0:r0 + WINDOW, qcol], 0.0))
            scores.append(lax.dot_general(kref[r0:r0 + keys, kcol],
                                          jnp.concatenate(qm, axis=0).astype(BF16), nt,
                                          preferred_element_type=F32))
        probs = []
        for (kvh, pair, _), s in zip(combos, scores):
            sink_row = jnp.concatenate(
                [jnp.full((1, WINDOW), sink_ref[kvh * GQA + g] * LOG2E, F32) for g in pair],
                axis=1)
            s = jnp.where(valid, s, -jnp.inf)
            m = jnp.maximum(jnp.max(s, axis=0, keepdims=True), sink_row)
            probs.append((jnp.exp2(s - m).astype(BF16), jnp.exp2(sink_row - m)))
        outs = [jnp.dot(vt[kvh, :, r0:r0 + keys], e, preferred_element_type=F32)
                for (kvh, _, _), (e, _) in zip(combos, probs)]
        for (kvh, pair, _), (_, e_sink), o in zip(combos, probs, outs):
            rden = 1.0 / (o[HEAD_DIM:HEAD_DIM + 1] + e_sink)
            on = (o[:HEAD_DIM] * rden).astype(BF16)
            for n, g in enumerate(pair):
                hd = kvh * GQA + g
                ot[hd * HEAD_DIM:(hd + 1) * HEAD_DIM, r0:r0 + WINDOW] = (
                    on[:, n * WINDOW:(n + 1) * WINDOW])

    def output(h0):
        rows = slice(h0, h0 + ATT_SUB)
        out_t = jnp.dot(wot_ref[...], ot[:, rows], preferred_element_type=F32)
        xo_ref[rows, :] = x_ref[rows, :] + out_t.T

    n_sub = ATT_TQ // ATT_SUB
    blocks = ATT_SUB // WINDOW
    project(0)
    for t in range(n_sub):
        h0 = t * ATT_SUB
        for blk in range(blocks):
            attend(h0 + blk * WINDOW)
            if blk == 0 and t + 1 < n_sub:
                project(h0 + ATT_SUB)
            if blk == 1 and t > 0:
                output(h0 - ATT_SUB)
    output(ATT_TQ - ATT_SUB)

    kbuf[0:WINDOW, :] = kbuf[ATT_TQ:, :]
    krol[0:WINDOW, :] = krol[ATT_TQ:, :]
    vt[:, 0:HEAD_DIM, 0:WINDOW] = vt[:, 0:HEAD_DIM, ATT_TQ:]


def _attn_prompt(x, gain, wqkv, wo_t, qg, kg, sinks, layer):
    b, l, d = x.shape
    wsel = lambda bi, i: (layer, 0, 0)
    tok = pl.BlockSpec((None, ATT_TQ, d), lambda bi, i: (bi, i, 0))
    last = pl.BlockSpec((None, WINDOW, KV_DIM), lambda bi, i: (bi, 0, 0))
    kvbuf = pltpu.VMEM((WINDOW + ATT_TQ, KV_DIM), BF16)
    return pl.pallas_call(
        _attn_prompt_kernel,
        grid=(b, l // ATT_TQ),
        in_specs=[
            pl.BlockSpec(memory_space=pltpu.SMEM),
            tok,
            _const_spec((1, d)),
            pl.BlockSpec((None, d, QKV_DIM), wsel),
            pl.BlockSpec((None, d, d), wsel),
            _const_spec((1, d)),
            _const_spec((1, KV_DIM)),
        ],
        out_specs=[tok, last, last],
        out_shape=[
            jax.ShapeDtypeStruct((b, l, d), F32),
            jax.ShapeDtypeStruct((b, WINDOW, KV_DIM), F32),
            jax.ShapeDtypeStruct((b, WINDOW, KV_DIM), F32),
        ],
        scratch_shapes=[
            kvbuf, kvbuf,
            pltpu.VMEM((N_KV, VT_ROWS, WINDOW + ATT_TQ), BF16),
            pltpu.VMEM((2 * WINDOW, 2 * WINDOW), jnp.int32),
            pltpu.VMEM((d, ATT_TQ), BF16),
            pltpu.VMEM((ATT_TQ, d), F32),
        ],
        compiler_params=_cparams(("arbitrary", "arbitrary")),
        name="attn_prompt",
    )(sinks, x, gain, wqkv, wo_t, qg, kg)


ATS_TB = 16


def _attn_sample_kernel(sink_ref, x_ref, g_ref, wq_ref, wkv_ref, wo_ref, qg_ref, kg_ref,
                        ck_ref, cv_ref, xo_ref, ko_ref, vo_ref, r_scr):
    x = x_ref[...]
    h = _rms(x, g_ref[...]).astype(BF16)
    kv = jnp.dot(h, wkv_ref[...], preferred_element_type=F32)
    kn = _head_rms(kv[:, :KV_DIM], N_KV, kg_ref[...])
    v = kv[:, KV_DIM:]
    qg = qg_ref[...]
    q_g = [_head_rms(jnp.dot(h, wq_ref[g], preferred_element_type=F32), N_KV, qg)
           * (HEAD_DIM ** -0.5) for g in range(GQA)]

    rows = GQA * N_KV
    rkv = lax.broadcasted_iota(jnp.int32, (rows, KV_DIM), 0) % N_KV
    lkv = lax.broadcasted_iota(jnp.int32, (rows, KV_DIM), 1) // HEAD_DIM
    own = rkv == lkv
    colj = lax.broadcasted_iota(jnp.int32, (rows, WINDOW), 1)
    sink_col = jnp.concatenate(
        [jnp.full((1, 1), sink_ref[(r % N_KV) * GQA + r // N_KV], F32) for r in range(rows)],
        axis=0)

    for n in range(ATS_TB):
        ko_ref[n, 0:WINDOW - 1, :] = ck_ref[n, 1:WINDOW, :]
        ko_ref[n, WINDOW - 1:WINDOW, :] = kn[n:n + 1]
        vo_ref[n, 0:WINDOW - 1, :] = cv_ref[n, 1:WINDOW, :]
        vo_ref[n, WINDOW - 1:WINDOW, :] = v[n:n + 1]
    qbds = []
    for n in range(ATS_TB):
        qbd = jnp.concatenate(
            [jnp.broadcast_to(q_g[g][n:n + 1], (N_KV, KV_DIM)) for g in range(GQA)], axis=0)
        qbds.append(jnp.where(own, qbd, 0.0))
    scores = [lax.dot_general(qbds[n].astype(BF16), ck_ref[n].astype(BF16),
                              (((1,), (1,)), ((), ())), preferred_element_type=F32)
              for n in range(ATS_TB)]
    probs = []
    for n in range(ATS_TB):
        s = jnp.where(colj >= 1, scores[n], -jnp.inf)
        s_new = jnp.sum(qbds[n] * kn[n:n + 1], axis=-1, keepdims=True)
        m = jnp.maximum(jnp.maximum(jnp.max(s, axis=-1, keepdims=True), s_new), sink_col)
        e = jnp.exp(s - m)
        e_new = jnp.exp(s_new - m)
        rden = 1.0 / (jnp.sum(e, axis=-1, keepdims=True) + e_new + jnp.exp(sink_col - m))
        probs.append(((e * rden).astype(BF16), e_new * rden))
    outs = [jnp.dot(probs[n][0], cv_ref[n].astype(BF16), preferred_element_type=F32)
            for n in range(ATS_TB)]
    for n in range(ATS_TB):
        o = jnp.where(own, outs[n] + probs[n][1] * v[n:n + 1], 0.0)
        for g in range(GQA):
            r_scr[g, n:n + 1, :] = jnp.sum(o[g * N_KV:(g + 1) * N_KV], axis=0, keepdims=True)

    out = x
    for g in range(GQA):
        out = out + jnp.dot(r_scr[g].astype(BF16), wo_ref[g], preferred_element_type=F32)
    xo_ref[...] = out


def _attn_sample(x, gain, wq_g, wqkv, wo_g, qg, kg, sinks, cache_k, cache_v, layer):
    n, d = x.shape
    tok = pl.BlockSpec((ATS_TB, d), lambda i: (i, 0))
    cache = pl.BlockSpec((None, ATS_TB, WINDOW, KV_DIM), lambda i: (layer, i, 0, 0))
    in_specs = [
        pl.BlockSpec(memory_space=pltpu.SMEM),
        tok,
        _const_spec((1, d)),
        pl.BlockSpec((None, GQA, d, KV_DIM), lambda i: (layer, 0, 0, 0)),
        pl.BlockSpec((None, d, 2 * KV_DIM), lambda i: (layer, 0, D_MODEL // (2 * KV_DIM))),
        pl.BlockSpec((None, GQA, KV_DIM, d), lambda i: (layer, 0, 0, 0)),
        _const_spec((1, HEAD_DIM)),
        _const_spec((1, HEAD_DIM)),
        cache,
        cache,
    ]
    args = [sinks, x, gain, wq_g, wqkv, wo_g, qg, kg, cache_k, cache_v]
    return pl.pallas_call(
        _attn_sample_kernel,
        grid=(n // ATS_TB,),
        in_specs=in_specs,
        out_specs=[tok, cache, cache],
        out_shape=[
            jax.ShapeDtypeStruct((n, d), F32),
            jax.ShapeDtypeStruct(cache_k.shape, F32),
            jax.ShapeDtypeStruct(cache_v.shape, F32),
        ],
        scratch_shapes=[pltpu.VMEM((GQA, ATS_TB, KV_DIM), F32)],
        input_output_aliases={len(args) - 2: 1, len(args) - 1: 2},
        compiler_params=_cparams(("arbitrary",)),
        name="attn_sample",
    )(*args)


FFN_TM = 512
SSM_TC = FFN_TM // CHUNK
CARRY = SUBLANES
CONV_RC = 64


def _ffn_prompt_kernel(*refs, glu_in, uc_out):
    refs = list(refs)
    x_ref = refs.pop(0)
    if glu_in:
        yc_ref, gm_ref, dsk_ref, wglu_ref = (refs.pop(0) for _ in range(4))
    g_ref, wup_ref, cw_ref, cb_ref, wdn_ref = (refs.pop(0) for _ in range(5))
    if uc_out:
        gn_ref = refs.pop(0)
    xo_ref, cs_ref = refs.pop(0), refs.pop(0)
    if uc_out:
        uc_ref = refs.pop(0)
    carry, upbuf, act, hs, oscr = refs

    i = pl.program_id(1)
    hm = FFN_TM // 2
    nlb = FF_BLK // LANES
    dlb = D_MODEL // LANES

    @pl.when(i == 0)
    def _():
        carry[...] = jnp.zeros_like(carry)

    if glu_in:
        for gb in range(N_GB):
            for t in range(CHUNK):
                oscr[gb, pl.ds(t, SSM_TC, stride=CHUNK), :] = (
                    yc_ref[:, gb * GB_IN + t * LANES:gb * GB_IN + (t + 1) * LANES])
        y = jnp.concatenate([oscr[gb] for gb in range(dlb)], axis=1)
        xo_ref[...] = _glu_tail(x_ref[...], y, gm_ref, dsk_ref, wglu_ref)
        xin_ref = xo_ref
    else:
        xin_ref = x_ref

    hs[...] = _rms(xin_ref[...], g_ref[...]).astype(BF16)

    def up_into(j, slot):
        for half, c0 in enumerate((j * FF_BLK, D_FF + j * FF_BLK)):
            cols = pl.ds(c0, FF_BLK)
            up = jnp.dot(hs[...], wup_ref[:, cols], preferred_element_type=F32)
            for lb in range(nlb):
                lcols = pl.ds(c0 + lb * LANES, LANES)
                buf = upbuf.at[slot, half, lb]
                buf[0:CARRY, :] = carry[:, lcols]
                buf[CARRY:, :] = up[:, lb * LANES:(lb + 1) * LANES]
                carry[:, lcols] = buf[FFN_TM:, :]

    def conv(slot, half, lb, c0, m0):
        lcols = pl.ds(c0 + lb * LANES, LANES)
        buf = upbuf.at[slot, half, lb]
        tap = lambda off: buf[pl.ds(CARRY + off + 2 * m0, CONV_RC, stride=2), :]
        em2, om1, e0, o1 = tap(-2), tap(-1), tap(0), tap(1)
        w0, w1, w2 = cw_ref[0:1, lcols], cw_ref[1:2, lcols], cw_ref[2:3, lcols]
        cb = cb_ref[:, lcols]
        return (cb + w0 * em2 + w1 * om1 + w2 * e0, cb + w0 * om1 + w1 * e0 + w2 * o1)

    for j in range(N_FF_BLK):
        up_into(j, j % 2)
        for m0 in range(0, hm, CONV_RC):
            for lb in range(nlb):
                cg = conv(j % 2, 0, lb, j * FF_BLK, m0)
                cv = conv(j % 2, 1, lb, D_FF + j * FF_BLK, m0)
                acol = pl.ds(j * FF_BLK + lb * LANES, LANES)
                for p in range(2):
                    act[p * hm + m0:p * hm + m0 + CONV_RC, acol] = (
                        jax.nn.silu(cg[p]) * cv[p]).astype(BF16)

    out = jnp.dot(act[...], wdn_ref[...], preferred_element_type=F32)
    for lb in range(dlb):
        for p in range(2):
            oscr[lb, pl.ds(p, hm, stride=2), :] = out[p * hm:(p + 1) * hm,
                                                      lb * LANES:(lb + 1) * LANES]
    xo = xin_ref[...] + jnp.concatenate([oscr[lb] for lb in range(dlb)], axis=1)
    xo_ref[...] = xo
    cs_ref[...] = carry[CARRY - 2:CARRY, :]

    if uc_out:
        h2 = _rms(xo, gn_ref[...])
        for gb in range(N_GB):
            oscr[gb] = h2[:, gb * LANES:(gb + 1) * LANES]
        for gb in range(N_GB):
            for t in range(CHUNK):
                uc_ref[:, gb * GB_IN + t * LANES:gb * GB_IN + (t + 1) * LANES] = (
                    oscr[gb, pl.ds(t, SSM_TC, stride=CHUNK), :].astype(BF16))


def _ffn_prompt(x, gain, wup, cw, cb, wdn, layer, glu=None, next_gain=None):
    b, l, d = x.shape
    nt = l // FFN_TM
    tok = pl.BlockSpec((None, FFN_TM, d), lambda bi, i: (bi, i, 0))
    chunks = pl.BlockSpec((SSM_TC, CHUNK * d), lambda bi, i: (bi * nt + i, 0))
    wsel = lambda bi, i: (layer, 0, 0)
    args, in_specs = [x], [tok]
    if glu is not None:
        yc, gm, dsk, wglu, mixer_layer = glu
        args += [yc, gm, dsk, wglu]
        in_specs += [chunks, _const_spec((1, d)), _const_spec((1, d)),
                     pl.BlockSpec((None, d, 2 * d), lambda bi, i: (mixer_layer, 0, 0),
                                  pipeline_mode=pl.Buffered(1))]
    args += [gain, wup, cw, cb, wdn]
    in_specs += [
        _const_spec((1, d)),
        pl.BlockSpec((None, d, F2), wsel, pipeline_mode=pl.Buffered(1)),
        pl.BlockSpec((None, 3, F2), wsel),
        pl.BlockSpec((None, 1, F2), wsel),
        pl.BlockSpec((None, D_FF, d), wsel, pipeline_mode=pl.Buffered(1)),
    ]
    out_specs = [tok, pl.BlockSpec((None, 2, F2), lambda bi, i: (bi, 0, 0))]
    out_shape = [jax.ShapeDtypeStruct((b, l, d), F32), jax.ShapeDtypeStruct((b, 2, F2), F32)]
    if next_gain is not None:
        args.append(next_gain)
        in_specs.append(_const_spec((1, d)))
        out_specs.append(chunks)
        out_shape.append(jax.ShapeDtypeStruct((b * l // CHUNK, CHUNK * d), BF16))
    return pl.pallas_call(
        functools.partial(_ffn_prompt_kernel, glu_in=glu is not None,
                          uc_out=next_gain is not None),
        grid=(b, nt),
        in_specs=in_specs,
        out_specs=out_specs,
        out_shape=out_shape,
        scratch_shapes=[
            pltpu.VMEM((CARRY, F2), F32),
            pltpu.VMEM((2, 2, FF_BLK // LANES, CARRY + FFN_TM, LANES), F32),
            pltpu.VMEM((FFN_TM, D_FF), BF16),
            pltpu.VMEM((FFN_TM, d), BF16),
            pltpu.VMEM((d // LANES, FFN_TM, LANES), F32),
        ],
        compiler_params=_cparams(("arbitrary", "arbitrary")),
        name="ffn_prompt",
    )(*args)


FFS_BLK = D_FF // 2
N_FFS_BLK = D_FF // FFS_BLK


def _ffn_sample_kernel(x_ref, g_ref, w_ref, cw_ref, cb_ref, wdn_ref, sc_ref,
                       xo_ref, cs_ref, gate):
    j = pl.program_id(0)

    @pl.when(j == 0)
    def _():
        xo_ref[...] = x_ref[...]

    h = _rms(x_ref[...], g_ref[...]).astype(BF16)
    up = jnp.dot(h, w_ref[...], preferred_element_type=F32)
    b0 = sc_ref[:, 0, :]
    b1 = sc_ref[:, 1, :]
    c = cb_ref[...]
    c = c + cw_ref[0:1, :] * b0
    c = c + cw_ref[1:2, :] * b1
    c = c + cw_ref[2:3, :] * up
    cs_ref[...] = jnp.stack([b1, up], axis=1)

    @pl.when(j < N_FFS_BLK)
    def _():
        gate[j] = jax.nn.silu(c)

    @pl.when(j >= N_FFS_BLK)
    def _():
        a = (gate[j - N_FFS_BLK] * c).astype(BF16)
        xo_ref[...] += jnp.dot(a, wdn_ref[...], preferred_element_type=F32)


def _ffn_sample(x, gain, wup, cw, cb, wdn, state, layer):
    n, d = x.shape
    nb = N_FFS_BLK
    sblk = pl.BlockSpec((None, n, 2, FFS_BLK), lambda j: (layer, 0, 0, j))
    in_specs = [
        _const_spec((n, d)),
        _const_spec((1, d)),
        pl.BlockSpec((None, d, FFS_BLK), lambda j: (layer, 0, j)),
        pl.BlockSpec((None, 3, FFS_BLK), lambda j: (layer, 0, j)),
        pl.BlockSpec((None, 1, FFS_BLK), lambda j: (layer, 0, j)),
        pl.BlockSpec((None, FFS_BLK, d), lambda j: (layer, jnp.maximum(j - nb, 0), 0)),
        sblk,
    ]
    args = [x, gain, wup, cw, cb, wdn, state]
    return pl.pallas_call(
        _ffn_sample_kernel,
        grid=(2 * nb,),
        in_specs=in_specs,
        out_specs=[_const_spec((n, d)), sblk],
        out_shape=[
            jax.ShapeDtypeStruct((n, d), F32),
            jax.ShapeDtypeStruct(state.shape, F32),
        ],
        scratch_shapes=[pltpu.VMEM((nb, n, FFS_BLK), F32)],
        input_output_aliases={len(args) - 1: 1},
        compiler_params=_cparams(("arbitrary",)),
        name="ffn_sample",
    )(*args)


def _tile_states(t):
    t2 = jnp.concatenate([t, t], axis=1)
    return jnp.concatenate([t2] * (GB_GROUPS // 2), axis=1)


def _ssm_prep_kernel(are_ref, aim_ref, dt_ref, bre_ref, bim_ref, cre_ref, cim_ref,
                     aflat_ref, iflat_ref, dtflat_ref,
                     w_ref, win_ref, cpt_ref, c0t_ref, a8_ref, a1_ref):
    a_re = are_ref[...]
    a_im = aim_ref[...]
    dt = jnp.exp(dt_ref[...])

    mag = jnp.exp(a_re * dt)
    ab_re, ab_im = mag * jnp.cos(a_im * dt), mag * jnp.sin(a_im * dt)
    pw = [(jnp.ones_like(ab_re), jnp.zeros_like(ab_re)), (ab_re, ab_im)]
    for _ in range(CHUNK - 1):
        p_re, p_im = pw[-1]
        pw.append((p_re * ab_re - p_im * ab_im, p_re * ab_im + p_im * ab_re))
    den = a_re * a_re + a_im * a_im
    z_re = ((ab_re - 1.0) * a_re + ab_im * a_im) / den
    z_im = (ab_im * a_re - (ab_re - 1.0) * a_im) / den
    b_re = bre_ref[...]
    b_im = bim_ref[...]
    bb_re = z_re * b_re - z_im * b_im
    bb_im = z_re * b_im + z_im * b_re
    c_re = cre_ref[...]
    c_im = cim_ref[...]

    rg = lax.broadcasted_iota(jnp.int32, (LANES, GB_ST2), 0) // SSM_GROUP
    lg = (lax.broadcasted_iota(jnp.int32, (LANES, GB_ST2), 1) % GB_ST) // SSM_STATE
    own = rg == lg

    def blockdiag(t_re, t_im):
        full = jnp.concatenate([_tile_states(t_re), _tile_states(t_im)], axis=1)
        return jnp.where(own, full, 0.0)

    c0t = blockdiag(c_re, -c_im)
    c0t_ref[...] = c0t.astype(BF16)

    lfull = []
    for d in range(CHUNK):
        if d == 0:
            l_re, l_im = bb_re, bb_im
        else:
            e_re, e_im = pw[d]
            l_re = e_re * bb_re - e_im * bb_im
            l_im = e_re * bb_im + e_im * bb_re
        lfull.append(blockdiag(l_re, l_im))

    def split(t):
        hi = t.astype(BF16)
        return hi, (t - hi.astype(F32)).astype(BF16)

    def dot_nt(a, b):
        return lax.dot_general(a, b, (((1,), (1,)), ((), ())), preferred_element_type=F32)

    c_hi, c_lo = split(c0t)
    kd = []
    for l in lfull:
        l_hi, l_lo = split(l)
        kd.append(dot_nt(l_hi, c_hi) + dot_nt(l_hi, c_lo) + dot_nt(l_lo, c_hi))
    zero = jnp.zeros((LANES, LANES), F32)
    for s in range(CHUNK):
        row = jnp.concatenate([kd[t - s] if t >= s else zero for t in range(CHUNK)], axis=1)
        w_ref[s * LANES:(s + 1) * LANES, :] = row.astype(BF16)
        win_ref[s * LANES:(s + 1) * LANES, :] = lfull[CHUNK - 1 - s].astype(BF16)
    for t in range(CHUNK):
        e_re, e_im = pw[t + 1]
        cpt_ref[t * LANES:(t + 1) * LANES, :] = blockdiag(
            c_re * e_re - c_im * e_im, -c_re * e_im - c_im * e_re).astype(BF16)

    af = aflat_ref[...]
    ai = iflat_ref[...]
    dtf = jnp.exp(dtflat_ref[...])
    magf = jnp.exp(af * dtf)
    f_re, f_im = magf * jnp.cos(ai * dtf), magf * jnp.sin(ai * dtf)
    a1_ref[...] = jnp.concatenate([f_re, f_im], axis=1)
    for _ in range(CHUNK.bit_length() - 1):
        f_re, f_im = f_re * f_re - f_im * f_im, 2.0 * f_re * f_im
    a8_ref[...] = jnp.concatenate([f_re, f_im], axis=1)


def _ssm_param_views(a_re, a_im, log_dt, b_re, b_im, c_re, c_im):
    rep = lambda t: jnp.repeat(t, SSM_GROUP, axis=0)
    rows = pl.BlockSpec((LANES, SSM_STATE), lambda g: (g, 0))
    flat = pl.BlockSpec((None, 1, GB_ST), lambda g: (g, 0, 0))
    args = [
        rep(a_re), rep(a_im), rep(log_dt[:, None]),
        jnp.transpose(b_re, (0, 2, 1)).reshape(N_GROUPS * SSM_GROUP, SSM_STATE),
        jnp.transpose(b_im, (0, 2, 1)).reshape(N_GROUPS * SSM_GROUP, SSM_STATE),
        c_re.reshape(N_GROUPS * SSM_GROUP, SSM_STATE),
        c_im.reshape(N_GROUPS * SSM_GROUP, SSM_STATE),
        a_re.reshape(N_GB, 1, GB_ST), a_im.reshape(N_GB, 1, GB_ST),
        jnp.repeat(log_dt, SSM_STATE).reshape(N_GB, 1, GB_ST),
    ]
    specs = [rows, rows, pl.BlockSpec((LANES, 1), lambda g: (g, 0)),
             rows, rows, rows, rows, flat, flat, flat]
    return args, specs


SSM_RT = MXU_W
SCAN_PAD = SUBLANES


def _ssm_core_kernel(uc_ref, w_ref, win_ref, cpt_ref, a8_ref, yc_ref, sre_ref, sim_ref, zs,
                     *, n_seq, n_chunk):
    n_rows = n_seq * n_chunk
    nlb = GB_ST2 // LANES
    half = nlb // 2
    pitch = n_chunk + SCAN_PAD

    def zrows(r):
        seq, off = divmod(r * SSM_RT, n_chunk)
        return slice(seq * pitch + off, seq * pitch + off + SSM_RT)

    for r in range(n_rows // SSM_RT):
        rs = slice(r * SSM_RT, (r + 1) * SSM_RT)
        z = jnp.dot(uc_ref[rs, :], win_ref[...], preferred_element_type=F32)
        for c in range(nlb):
            zs[c, zrows(r), :] = z[:, c * LANES:(c + 1) * LANES]

    a8 = a8_ref[...]
    a_bl = [jnp.broadcast_to(a8[:, c * LANES:(c + 1) * LANES], (n_seq, LANES))
            for c in range(nlb)]

    def step(k, carry):
        rows = pl.ds(k, n_seq, stride=pitch)
        new = []
        for c in range(half):
            s_re, s_im = carry[c], carry[half + c]
            z_re = zs[c, rows, :]
            z_im = zs[half + c, rows, :]
            zs[c, rows, :] = s_re
            zs[half + c, rows, :] = s_im
            new.append((a_bl[c] * s_re - a_bl[half + c] * s_im + z_re,
                        a_bl[c] * s_im + a_bl[half + c] * s_re + z_im))
        return tuple(n[0] for n in new) + tuple(n[1] for n in new)

    n_rt = n_rows // SSM_RT
    steps_per = n_chunk // n_rt
    fin = (jnp.zeros((n_seq, LANES), F32),) * nlb
    for r in range(n_rt):
        rs = slice(r * SSM_RT, (r + 1) * SSM_RT)
        for m in range(CHUNK // 2):
            kin = (2 * m + 2) * LANES
            oc = slice(2 * m * LANES, (2 * m + 2) * LANES)
            yc_ref[rs, oc] = jnp.dot(uc_ref[rs, :kin], w_ref[:kin, oc],
                                     preferred_element_type=F32)
        for k in range(r * steps_per, (r + 1) * steps_per):
            fin = step(k, fin)
    sre_ref[...] = jnp.concatenate(fin[:half], axis=1)
    sim_ref[...] = jnp.concatenate(fin[half:], axis=1)

    for r in range(n_rt):
        rs = slice(r * SSM_RT, (r + 1) * SSM_RT)
        sp = jnp.concatenate([zs[c, zrows(r), :] for c in range(nlb)], axis=1).astype(BF16)
        yc_ref[rs, :] += lax.dot_general(sp, cpt_ref[...], (((1,), (1,)), ((), ())),
                                         preferred_element_type=F32)


def _ssm_mixer_kernel(*refs, n_seq, n_chunk):
    params, uc_ref = refs[:10], refs[10]
    yc_ref, sre_ref, sim_ref, bb_ref, c0t_ref, a1_ref = refs[11:17]
    w_scr, win_scr, cpt_scr, a8_scr, zs = refs[17:]
    _ssm_prep_kernel(*params, w_scr, win_scr, cpt_scr, c0t_ref, a8_scr, a1_ref)
    bb_ref[...] = win_scr[(CHUNK - 1) * LANES:, :]
    _ssm_core_kernel(uc_ref, w_scr, win_scr, cpt_scr, a8_scr, yc_ref, sre_ref, sim_ref, zs,
                     n_seq=n_seq, n_chunk=n_chunk)


def _ssm_mixer(uc, params, n_seq):
    n_rows = uc.shape[0]
    n_chunk = n_rows // n_seq
    p_args, p_specs = _ssm_param_views(*params)
    blk = pl.BlockSpec((n_rows, GB_IN), lambda g: (0, g))
    st = pl.BlockSpec((n_seq, GB_ST), lambda g: (0, g))
    op = pl.BlockSpec((None, LANES, GB_ST2), lambda g: (g, 0, 0))
    mat = pltpu.VMEM((GB_IN, GB_ST2), BF16)
    return pl.pallas_call(
        functools.partial(_ssm_mixer_kernel, n_seq=n_seq, n_chunk=n_chunk),
        grid=(N_GB,),
        in_specs=p_specs + [blk],
        out_specs=[blk, st, st, op, op, pl.BlockSpec((None, 1, GB_ST2), lambda g: (g, 0, 0))],
        out_shape=[
            jax.ShapeDtypeStruct((n_rows, N_GB * GB_IN), F32),
            jax.ShapeDtypeStruct((n_seq, N_GROUPS * SSM_STATE), F32),
            jax.ShapeDtypeStruct((n_seq, N_GROUPS * SSM_STATE), F32),
            jax.ShapeDtypeStruct((N_GB, LANES, GB_ST2), BF16),
            jax.ShapeDtypeStruct((N_GB, LANES, GB_ST2), BF16),
            jax.ShapeDtypeStruct((N_GB, 1, GB_ST2), F32),
        ],
        scratch_shapes=[mat, mat, mat, pltpu.VMEM((1, GB_ST2), F32),
                        pltpu.VMEM((GB_ST2 // LANES, n_seq * (n_chunk + SCAN_PAD), LANES), F32)],
        compiler_params=_cparams(("arbitrary",)),
        name="ssm_mixer",
    )(*p_args, uc)


def _glu_tail(x, y, g_ref, dsk_ref, wglu_ref):
    h = _rms(x, g_ref[...])
    z = jax.nn.gelu(y + dsk_ref[...] * h).astype(BF16)
    zz = jnp.dot(z, wglu_ref[...], preferred_element_type=F32)
    return x + zz[:, :D_MODEL] * jax.nn.sigmoid(zz[:, D_MODEL:])


def _ssm_sample_kernel(x_ref, g_ref, bb_ref, c0t_ref, a1_ref, hre_ref, him_ref,
                       y_ref, sre_ref, sim_ref):
    gb = pl.program_id(0)
    x = x_ref[...]
    ms = jnp.mean(x * x, axis=-1, keepdims=True)
    cols = pl.ds(pl.multiple_of(gb * LANES, LANES), LANES)
    u = (x_ref[:, cols] * lax.rsqrt(ms + RMS_EPS) * g_ref[:, cols]).astype(BF16)
    bu = jnp.dot(u, bb_ref[...], preferred_element_type=F32)
    a1 = a1_ref[...]
    a_re = a1[:, :GB_ST]
    a_im = a1[:, GB_ST:]
    h_re = hre_ref[...]
    h_im = him_ref[...]
    s_re = a_re * h_re - a_im * h_im + bu[:, :GB_ST]
    s_im = a_re * h_im + a_im * h_re + bu[:, GB_ST:]
    sre_ref[...] = s_re
    sim_ref[...] = s_im
    s = jnp.concatenate([s_re, s_im], axis=1).astype(BF16)
    y_ref[...] = lax.dot_general(s, c0t_ref[...], (((1,), (1,)), ((), ())),
                                 preferred_element_type=F32)


def _ssm_sample(x, gain, bb, c0t, a1, h_re, h_im):
    n, d = x.shape
    st = pl.BlockSpec((n, GB_ST), lambda g: (0, g))
    return pl.pallas_call(
        _ssm_sample_kernel,
        grid=(N_GB,),
        in_specs=[
            _const_spec((n, d)),
            _const_spec((1, d)),
            pl.BlockSpec((None, LANES, GB_ST2), lambda g: (g, 0, 0)),
            pl.BlockSpec((None, LANES, GB_ST2), lambda g: (g, 0, 0)),
            pl.BlockSpec((None, 1, GB_ST2), lambda g: (g, 0, 0)),
            st, st,
        ],
        out_specs=[pl.BlockSpec((n, LANES), lambda g: (0, g)), st, st],
        out_shape=[
            jax.ShapeDtypeStruct((n, d), F32),
            jax.ShapeDtypeStruct((n, N_GROUPS * SSM_STATE), F32),
            jax.ShapeDtypeStruct((n, N_GROUPS * SSM_STATE), F32),
        ],
        compiler_params=_cparams(("arbitrary",)),
        name="ssm_sample",
    )(x, gain, bb, c0t, a1, h_re, h_im)


def _glu_sample_kernel(x_ref, y_ref, g_ref, dsk_ref, wglu_ref, xo_ref):
    xo_ref[...] = _glu_tail(x_ref[...], y_ref[...], g_ref, dsk_ref, wglu_ref)


def _glu_sample(x, y, gain, dsk, wglu, layer):
    n, d = x.shape
    return pl.pallas_call(
        _glu_sample_kernel,
        grid=(1,),
        in_specs=[_const_spec((n, d)), _const_spec((n, d)), _const_spec((1, d)),
                  _const_spec((1, d)), pl.BlockSpec((None, d, 2 * d), lambda i: (layer, 0, 0))],
        out_specs=_const_spec((n, d)),
        out_shape=jax.ShapeDtypeStruct((n, d), F32),
        compiler_params=_cparams(("arbitrary",)),
        name="glu_sample",
    )(x, y, gain, dsk, wglu)


def kernel(x_prompt, x_sample, cache_k, cache_v, state_ssm_re, state_ssm_im, state_conv,
           norm_mix, norm_ffn, w_qkv, w_o, q_norm, k_norm, sinks, ssm_a_re, ssm_a_im,
           ssm_log_dt, ssm_b_re, ssm_b_im, ssm_c_re, ssm_c_im, ssm_d, w_glu, w_up, conv_w,
           conv_b, w_down):
    depth = norm_mix.shape[0]
    nb, _, _ = x_prompt.shape
    ns = x_sample.shape[0]
    xp = x_prompt
    xs = x_sample.reshape(ns, D_MODEL)
    row = lambda t: t.reshape(1, -1)

    wqkv = w_qkv.astype(BF16)
    wo = w_o.astype(BF16)
    wo_t = jnp.swapaxes(wo, 1, 2)
    wglu = w_glu.astype(BF16)
    wup = w_up.astype(BF16)
    wdn = w_down.astype(BF16)
    cb = conv_b.reshape(depth, 1, F2)
    n_att = wqkv.shape[0]
    wq_g = wqkv[:, :, :D_MODEL].reshape(n_att, D_MODEL, N_KV, GQA, HEAD_DIM)
    wq_g = jnp.transpose(wq_g, (0, 3, 1, 2, 4)).reshape(n_att, GQA, D_MODEL, KV_DIM)
    wo_g = wo.reshape(n_att, N_KV, GQA, HEAD_DIM, D_MODEL)
    wo_g = jnp.transpose(wo_g, (0, 2, 1, 3, 4)).reshape(n_att, GQA, KV_DIM, D_MODEL)
    ck = cache_k.reshape(n_att, ns, WINDOW, KV_DIM)
    cv = cache_v.reshape(n_att, ns, WINDOW, KV_DIM)

    kps, vps = [], []
    srp, sip, srs, sis = [], [], [], []
    cps = []
    conv_s = state_conv
    glu = None
    for i in range(depth):
        j = i // 2
        gm = row(norm_mix[i])
        if i % 2 == 0:
            qg, kg = row(q_norm[j]), row(k_norm[j])
            xp, kp, vp = _attn_prompt(xp, gm, wqkv, wo_t, jnp.tile(qg, (1, N_HEADS)),
                                      jnp.tile(kg, (1, N_KV)), sinks[j], j)
            xs, ck, cv = _attn_sample(xs, gm, wq_g, wqkv, wo_g, qg, kg, sinks[j], ck, cv, j)
            kps.append(kp.reshape(nb, WINDOW, N_KV, HEAD_DIM))
            vps.append(vp.reshape(nb, WINDOW, N_KV, HEAD_DIM))
        else:
            params = (ssm_a_re[j], ssm_a_im[j], ssm_log_dt[j], ssm_b_re[j], ssm_b_im[j],
                      ssm_c_re[j], ssm_c_im[j])
            dsk = row(ssm_d[j])
            yc, s_re, s_im, bb, c0t, a1 = _ssm_mixer(uc, params, nb)
            glu = (yc, gm, dsk, wglu, j)
            srp.append(s_re.reshape(nb, N_GROUPS, SSM_STATE))
            sip.append(s_im.reshape(nb, N_GROUPS, SSM_STATE))
            ys, t_re, t_im = _ssm_sample(
                xs, gm, bb, c0t, a1,
                state_ssm_re[j].reshape(ns, N_GROUPS * SSM_STATE),
                state_ssm_im[j].reshape(ns, N_GROUPS * SSM_STATE))
            xs = _glu_sample(xs, ys, gm, dsk, wglu, j)
            srs.append(t_re.reshape(ns, N_GROUPS, SSM_STATE))
            sis.append(t_im.reshape(ns, N_GROUPS, SSM_STATE))

        gf = row(norm_ffn[i])
        next_gain = row(norm_mix[i + 1]) if (i % 2 == 0 and i + 1 < depth) else None
        res = _ffn_prompt(xp, gf, wup, conv_w, cb, wdn, i, glu=glu, next_gain=next_gain)
        glu = None
        xp, cp = res[0], res[1]
        if next_gain is not None:
            uc = res[2]
        xs, conv_s = _ffn_sample(xs, gf, wup, conv_w, cb, wdn, conv_s, i)
        cps.append(cp)

    k_s, v_s = (t.reshape(n_att, ns, WINDOW, N_KV, HEAD_DIM) for t in (ck, cv))
    return (xp, xs.reshape(ns, 1, D_MODEL),
            jnp.stack(kps), jnp.stack(vps), k_s, v_s,
            jnp.stack(srp), jnp.stack(sip), jnp.stack(srs), jnp.stack(sis),
            jnp.stack(cps), conv_s)
```
